```python
import jax
import jax.numpy as jnp
from jax import lax
import numpy as np

D_MODEL = 1024
BATCH = 2
SEQ = 8192
DEPTH = 1
DEC_BATCH = 128
DEC_SEQ = 8
PAST_LEN = 16384
PAGE_SIZE = 128

D_MIX = D_MODEL
HEAD_OUT = 128
MLA_HEADS = D_MIX // (2 * HEAD_OUT)
ML_HEADS = D_MIX // (2 * HEAD_OUT)
MLA_NOPE = 128
MLA_ROPE = 64
MLA_VDIM = HEAD_OUT
Q_LORA = 384
KV_LORA = 256
ROPE_BASE = 10000.0
SM_SCALE = (MLA_NOPE + MLA_ROPE) ** -0.5
Q_BLOCK = 128
ML_QK = 64
ML_V = HEAD_OUT
ML_CHUNK = 64
IGATE_CAP = 15.0
N_EXPERTS = 32
TOP_K = 4
D_EXPERT = D_MODEL
SWIGLU_LIMIT = 7.0
SWIGLU_ALPHA = 1.702
MOE_BLOCK = 128
EPS = 1e-6
MLA_OUT = MLA_HEADS * MLA_VDIM
ML_OUT = ML_HEADS * ML_V
IN_SPLITS = (Q_LORA, KV_LORA, MLA_ROPE, ML_HEADS * ML_QK, ML_HEADS * ML_QK, ML_OUT, ML_HEADS, ML_HEADS, ML_OUT)
D_IN = Q_LORA + KV_LORA + MLA_ROPE + 2 * ML_HEADS * ML_QK + ML_OUT + 2 * ML_HEADS + ML_OUT

kernel_name = 'hymba_mla_mlstm_moe_step'


def _rmsnorm(x, g):
    xf = x.astype(jnp.float32)
    y = xf * lax.rsqrt(jnp.mean(xf * xf, axis=-1, keepdims=True) + EPS)
    return (y * g.astype(jnp.float32)).astype(x.dtype)


def _rope(x, pos):
    half = x.shape[-1] // 2
    inv_freq = ROPE_BASE ** (-jnp.arange(half, dtype=jnp.float32) / half)
    ang = pos.astype(jnp.float32)[:, None] * inv_freq[None, :]
    cos = jnp.cos(ang)[None, :, None, :]
    sin = jnp.sin(ang)[None, :, None, :]
    xf = x.astype(jnp.float32)
    x1, x2 = xf[..., :half], xf[..., half:]
    return jnp.concatenate([x1 * cos - x2 * sin, x1 * sin + x2 * cos], axis=-1).astype(x.dtype)


def _modulation(c, p):
    mod = jnp.einsum('bd,de->be', jax.nn.silu(c), p['w_ada']) + p['b_ada']
    mod = mod.reshape(c.shape[0], 6, 1, D_MODEL)
    return [mod[:, i] for i in range(6)]


def _project(h, pos, p):
    B, T = h.shape[0], h.shape[1]
    z = jnp.einsum('btd,de->bte', h, p['w_in'])
    cols = []
    start = 0
    for size in IN_SPLITS:
        cols.append(z[..., start:start + size])
        start += size
    q_lat, ckv, kpe, mq, mk, mv, ig, fg, og = cols
    q = jnp.einsum('btr,rhe->bthe', _rmsnorm(q_lat, p['g_qlat']), p['w_q_up'])
    q_nope = q[..., :MLA_NOPE]
    q_rope = _rope(q[..., MLA_NOPE:], pos)
    ckv = _rmsnorm(ckv, p['g_kvlat'])
    kpe = _rope(kpe[:, :, None, :], pos)[:, :, 0, :]
    mq = mq.reshape(B, T, ML_HEADS, ML_QK) * (ML_QK ** -0.5)
    mk = mk.reshape(B, T, ML_HEADS, ML_QK)
    mv = mv.reshape(B, T, ML_HEADS, ML_V)
    ig = ig.astype(jnp.float32) + p['b_igate'].astype(jnp.float32)
    ig = IGATE_CAP * jnp.tanh(ig / IGATE_CAP)
    lf = jax.nn.log_sigmoid(fg.astype(jnp.float32) + p['b_fgate'].astype(jnp.float32))
    og = jax.nn.sigmoid(og)
    return q_nope, q_rope, ckv, kpe, mq, mk, mv, ig, lf, og


def _causal_block_attention(q, k, v):
    B, S, H, _ = q.shape
    nb = S // Q_BLOCK
    qb = q.reshape(B, nb, Q_BLOCK, H, q.shape[-1]).transpose(1, 0, 2, 3, 4)
    kpos = jnp.arange(S)

    def one(args):
        qi, i = args
        s = jnp.einsum('bqhd,bkhd->bhqk', qi, k, preferred_element_type=jnp.float32) * SM_SCALE
        qpos = i * Q_BLOCK + jnp.arange(Q_BLOCK)
        s = jnp.where(kpos[None, :] <= qpos[:, None], s, -jnp.inf)
        probs = jax.nn.softmax(s, axis=-1)
        return jnp.einsum('bhqk,bkhd->bqhd', probs.astype(v.dtype), v)

    out = lax.map(one, (qb, jnp.arange(nb)))
    return out.transpose(1, 0, 2, 3, 4).reshape(B, S, H, v.shape[-1])


def _mla_prompt(qn, qr, ckv, kpe, p):
    B, S, H, _ = qn.shape
    k_nope = jnp.einsum('bsc,chn->bshn', ckv, p['w_uk'])
    v = jnp.einsum('bsc,chd->bshd', ckv, p['w_uv'])
    k = jnp.concatenate([k_nope, jnp.broadcast_to(kpe[:, :, None, :], (B, S, H, MLA_ROPE))], axis=-1)
    q = jnp.concatenate([qn, qr], axis=-1)
    return _causal_block_attention(q, k, v)


def _mla_sample(qn, qr, ckv, kpe, past_ckv, past_kpe, p):
    T = qn.shape[1]
    f32 = jnp.float32
    q_lat = jnp.einsum('bthn,chn->bthc', qn, p['w_uk'])
    s_past = (jnp.einsum('bthc,bpc->bhtp', q_lat, past_ckv, preferred_element_type=f32)
              + jnp.einsum('bthr,bpr->bhtp', qr, past_kpe, preferred_element_type=f32)) * SM_SCALE
    s_new = (jnp.einsum('bthc,bsc->bhts', q_lat, ckv, preferred_element_type=f32)
             + jnp.einsum('bthr,bsr->bhts', qr, kpe, preferred_element_type=f32)) * SM_SCALE
    causal = jnp.tril(jnp.ones((T, T), dtype=bool))
    s_new = jnp.where(causal, s_new, -jnp.inf)
    probs = jax.nn.softmax(jnp.concatenate([s_past, s_new], axis=-1), axis=-1)
    n_past = past_ckv.shape[1]
    o_lat = (jnp.einsum('bhtp,bpc->bthc', probs[..., :n_past], past_ckv)
             + jnp.einsum('bhts,bsc->bthc', probs[..., n_past:], ckv))
    return jnp.einsum('bthc,chd->bthd', o_lat, p['w_uv']).astype(qn.dtype)


def _mlstm_chunk(carry, xs):
    C, n, m = carry
    q, k, v, ig, lf = xs
    L = q.shape[2]
    b = jnp.cumsum(lf, axis=-1)
    log_d = b[..., :, None] - b[..., None, :] + ig[..., None, :]
    causal = jnp.tril(jnp.ones((L, L), dtype=bool))
    log_d = jnp.where(causal, log_d, -jnp.inf)
    log_inter = b + m[..., None]
    m_t = jnp.maximum(log_inter, jnp.max(log_d, axis=-1))
    d = jnp.exp(log_d - m_t[..., None])
    inter = jnp.exp(log_inter - m_t)
    s = jnp.einsum('bhtk,bhsk->bhts', q, k) * d
    num = inter[..., None] * jnp.einsum('bhvk,bhtk->bhtv', C, q) + jnp.einsum('bhts,bhsv->bhtv', s, v)
    den = inter * jnp.einsum('bhk,bhtk->bht', n, q) + jnp.sum(s, axis=-1)
    h = num / jnp.maximum(jnp.abs(den), jnp.exp(-m_t))[..., None]
    m_new = m_t[..., -1]
    w_state = jnp.exp(b[..., -1:] - b + ig - m_new[..., None])
    decay = jnp.exp(b[..., -1] + m - m_new)
    C_new = decay[..., None, None] * C + jnp.einsum('bhs,bhsv,bhsk->bhvk', w_state, v, k)
    n_new = decay[..., None] * n + jnp.einsum('bhs,bhsk->bhk', w_state, k)
    return (C_new, n_new, m_new), h


def _mlstm(mq, mk, mv, ig, lf, C0, n0, m0):
    B, T, H = mq.shape[0], mq.shape[1], mq.shape[2]
    L = ML_CHUNK if T % ML_CHUNK == 0 else T
    nc = T // L

    def chunks(a):
        a = jnp.moveaxis(a.astype(jnp.float32), 2, 1)
        a = a.reshape((B, H, nc, L) + a.shape[3:])
        return jnp.moveaxis(a, 2, 0)

    carry0 = (C0.astype(jnp.float32), n0.astype(jnp.float32), m0.astype(jnp.float32))
    (C, n, m), h = lax.scan(_mlstm_chunk, carry0, (chunks(mq), chunks(mk), chunks(mv), chunks(ig), chunks(lf)))
    h = jnp.moveaxis(h, 0, 2).reshape(B, H, T, ML_V)
    return jnp.moveaxis(h, 1, 2).astype(mq.dtype), C, n, m


def _merge(o_mla, h_ml, og, p):
    B, T = o_mla.shape[0], o_mla.shape[1]
    heads = jnp.concatenate([o_mla, h_ml.astype(o_mla.dtype)], axis=2)
    heads = _rmsnorm(heads, p['g_heads'])
    mla_part = heads[:, :, :MLA_HEADS].reshape(B, T, MLA_OUT)
    ml_part = heads[:, :, MLA_HEADS:].reshape(B, T, ML_OUT) * og
    return jnp.einsum('bte,ed->btd', jnp.concatenate([mla_part, ml_part], axis=-1), p['w_out'])


def _moe(h, p):
    B, T, D = h.shape
    x = h.reshape(B * T, D)
    N = x.shape[0]
    logits = jnp.einsum('nd,de->ne', x.astype(jnp.float32), p['w_router'].astype(jnp.float32)) + p['b_router'].astype(jnp.float32)
    top_val, top_idx = lax.top_k(logits, TOP_K)
    gates = jax.nn.softmax(top_val, axis=-1)
    M = N * TOP_K
    flat_e = top_idx.reshape(M)
    flat_tok = jnp.arange(M, dtype=jnp.int32) // TOP_K
    order = jnp.argsort(flat_e)
    sorted_e = flat_e[order]
    counts = jnp.bincount(flat_e, length=N_EXPERTS)
    padded = (counts + MOE_BLOCK - 1) // MOE_BLOCK * MOE_BLOCK
    starts = jnp.cumsum(counts) - counts
    pends = jnp.cumsum(padded)
    pstarts = pends - padded
    dest_sorted = pstarts[sorted_e] + jnp.arange(M, dtype=jnp.int32) - starts[sorted_e]
    dest = jnp.zeros((M,), jnp.int32).at[order].set(dest_sorted.astype(jnp.int32))
    n_blocks = -(-M // MOE_BLOCK) + N_EXPERTS
    m_pad = n_blocks * MOE_BLOCK
    row_tok = jnp.full((m_pad,), N, jnp.int32).at[dest].set(flat_tok)
    x_ext = jnp.concatenate([x, jnp.zeros((1, D), x.dtype)], axis=0)
    xb = x_ext[row_tok].reshape(n_blocks, MOE_BLOCK, D)
    block_e = jnp.minimum(jnp.searchsorted(pends, jnp.arange(n_blocks) * MOE_BLOCK, side='right'), N_EXPERTS - 1)

    def expert_block(args):
        xi, e = args
        gu = xi @ p['w_gate_up'][e] + p['b_gate_up'][e]
        g = jnp.minimum(gu[:, :D_EXPERT], SWIGLU_LIMIT)
        u = jnp.clip(gu[:, D_EXPERT:], -SWIGLU_LIMIT, SWIGLU_LIMIT)
        act = g * jax.nn.sigmoid(SWIGLU_ALPHA * g) * (u + 1.0)
        return act @ p['w_down'][e] + p['b_down'][e]

    yb = lax.map(expert_block, (xb, block_e)).reshape(m_pad, D)
    y = jnp.einsum('nk,nkd->nd', gates.astype(h.dtype), yb[dest].reshape(N, TOP_K, D))
    return y.reshape(B, T, D)


def _finish(x, mods, mix_out, p):
    x = x + mods[2] * _rmsnorm(mix_out, p['g_post_mix'])
    h = _rmsnorm(x, p['g_pre_ffn']) * (1.0 + mods[4]) + mods[3]
    return x + mods[5] * _rmsnorm(_moe(h, p), p['g_post_ffn'])


def setup_inputs(seed: int = 0) -> dict:
    key = jax.random.key(seed)
    ks = jax.random.split(key, 32)
    f32 = jnp.float32
    n_pages = PAST_LEN // PAGE_SIZE
    n_phys = (5 * DEC_BATCH * n_pages) // 4
    L = DEPTH

    def nrm(k, shape, scale):
        return scale * jax.random.normal(k, shape, f32)

    def gain(k, shape):
        return 1.0 + 0.05 * jax.random.normal(k, shape, f32)

    page_table = jax.random.permutation(ks[4], n_phys)[: DEC_BATCH * n_pages].reshape(DEC_BATCH, n_pages).astype(jnp.int32)
    return {
        'x_prompt': nrm(ks[0], (BATCH, SEQ, D_MODEL), 1.0),
        'x_sample': nrm(ks[1], (DEC_BATCH, DEC_SEQ, D_MODEL), 1.0),
        'cache_ckv': nrm(ks[2], (L, n_phys, PAGE_SIZE, KV_LORA), 1.0),
        'cache_kpe': nrm(ks[3], (L, n_phys, PAGE_SIZE, MLA_ROPE), 1.0),
        'page_table': page_table,
        'state_C': nrm(ks[5], (L, DEC_BATCH, ML_HEADS, ML_V, ML_QK), 0.5),
        'state_n': nrm(ks[6], (L, DEC_BATCH, ML_HEADS, ML_QK), 0.5),
        'state_m': nrm(ks[7], (L, DEC_BATCH, ML_HEADS), 0.5),
        'c_prompt': nrm(ks[8], (BATCH, D_MODEL), 1.0),
        'c_sample': nrm(ks[9], (DEC_BATCH, D_MODEL), 1.0),
        'w_ada': nrm(ks[10], (L, D_MODEL, 6 * D_MODEL), 0.5 * D_MODEL ** -0.5),
        'b_ada': nrm(ks[11], (L, 6 * D_MODEL), 0.02),
        'g_pre_mix': gain(ks[12], (L, D_MODEL)),
        'g_post_mix': gain(ks[13], (L, D_MODEL)),
        'g_pre_ffn': gain(ks[14], (L, D_MODEL)),
        'g_post_ffn': gain(ks[15], (L, D_MODEL)),
        'w_in': nrm(ks[16], (L, D_MODEL, D_IN), D_MODEL ** -0.5),
        'g_qlat': gain(ks[17], (L, Q_LORA)),
        'w_q_up': nrm(ks[18], (L, Q_LORA, MLA_HEADS, MLA_NOPE + MLA_ROPE), Q_LORA ** -0.5),
        'g_kvlat': gain(ks[19], (L, KV_LORA)),
        'w_uk': nrm(ks[20], (L, KV_LORA, MLA_HEADS, MLA_NOPE), KV_LORA ** -0.5),
        'w_uv': nrm(ks[21], (L, KV_LORA, MLA_HEADS, MLA_VDIM), KV_LORA ** -0.5),
        'b_igate': nrm(ks[22], (L, ML_HEADS), 0.1),
        'b_fgate': 3.0 + nrm(ks[23], (L, ML_HEADS), 0.5),
        'g_heads': gain(ks[24], (L, MLA_HEADS + ML_HEADS, HEAD_OUT)),
        'w_out': nrm(ks[25], (L, D_MIX, D_MODEL), D_MIX ** -0.5),
        'w_router': nrm(ks[26], (L, D_MODEL, N_EXPERTS), D_MODEL ** -0.5),
        'b_router': nrm(ks[27], (L, N_EXPERTS), 0.01),
        'w_gate_up': nrm(ks[28], (L, N_EXPERTS, D_MODEL, 2 * D_EXPERT), D_MODEL ** -0.5),
        'b_gate_up': nrm(ks[29], (L, N_EXPERTS, 2 * D_EXPERT), 0.02),
        'w_down': nrm(ks[30], (L, N_EXPERTS, D_EXPERT, D_MODEL), D_EXPERT ** -0.5),
        'b_down': nrm(ks[31], (L, N_EXPERTS, D_MODEL), 0.02),
    }


def reference(x_prompt, x_sample, cache_ckv, cache_kpe, page_table, state_C, state_n, state_m,
              c_prompt, c_sample, w_ada, b_ada, g_pre_mix, g_post_mix, g_pre_ffn, g_post_ffn,
              w_in, g_qlat, w_q_up, g_kvlat, w_uk, w_uv, b_igate, b_fgate, g_heads, w_out,
              w_router, b_router, w_gate_up, b_gate_up, w_down, b_down):
    S = x_prompt.shape[1]
    T = x_sample.shape[1]
    DB = x_sample.shape[0]
    past_len = page_table.shape[1] * PAGE_SIZE
    pos_p = jnp.arange(S)
    pos_s = past_len + jnp.arange(T)
    dt = x_prompt.dtype
    xp, xs = x_prompt, x_sample
    ckv_p, kpe_p, C_p, n_p, m_p = [], [], [], [], []
    ckv_s, kpe_s, C_s, n_s, m_s = [], [], [], [], []
    for l in range(DEPTH):
        p = {'w_ada': w_ada[l], 'b_ada': b_ada[l], 'g_post_mix': g_post_mix[l], 'g_pre_ffn': g_pre_ffn[l],
             'g_post_ffn': g_post_ffn[l], 'w_in': w_in[l], 'g_qlat': g_qlat[l], 'w_q_up': w_q_up[l],
             'g_kvlat': g_kvlat[l], 'w_uk': w_uk[l], 'w_uv': w_uv[l], 'b_igate': b_igate[l], 'b_fgate': b_fgate[l],
             'g_heads': g_heads[l], 'w_out': w_out[l], 'w_router': w_router[l], 'b_router': b_router[l],
             'w_gate_up': w_gate_up[l], 'b_gate_up': b_gate_up[l], 'w_down': w_down[l], 'b_down': b_down[l]}
        mods = _modulation(c_prompt, p)
        h = _rmsnorm(xp, g_pre_mix[l]) * (1.0 + mods[1]) + mods[0]
        qn, qr, ckv, kpe, mq, mk, mv, ig, lf, og = _project(h, pos_p, p)
        o_mla = _mla_prompt(qn, qr, ckv, kpe, p)
        Bp = xp.shape[0]
        h_ml, C, n, m = _mlstm(mq, mk, mv, ig, lf,
                               jnp.zeros((Bp, ML_HEADS, ML_V, ML_QK), jnp.float32),
                               jnp.zeros((Bp, ML_HEADS, ML_QK), jnp.float32),
                               jnp.zeros((Bp, ML_HEADS), jnp.float32))
        xp = _finish(xp, mods, _merge(o_mla, h_ml, og, p), p)
        ckv_p.append(ckv)
        kpe_p.append(kpe)
        C_p.append(C.astype(dt))
        n_p.append(n.astype(dt))
        m_p.append(m.astype(dt))
        mods = _modulation(c_sample, p)
        h = _rmsnorm(xs, g_pre_mix[l]) * (1.0 + mods[1]) + mods[0]
        qn, qr, ckv, kpe, mq, mk, mv, ig, lf, og = _project(h, pos_s, p)
        past_ckv = cache_ckv[l, page_table].reshape(DB, past_len, KV_LORA)
        past_kpe = cache_kpe[l, page_table].reshape(DB, past_len, MLA_ROPE)
        o_mla = _mla_sample(qn, qr, ckv, kpe, past_ckv, past_kpe, p)
        h_ml, C, n, m = _mlstm(mq, mk, mv, ig, lf, state_C[l], state_n[l], state_m[l])
        xs = _finish(xs, mods, _merge(o_mla, h_ml, og, p), p)
        ckv_s.append(ckv)
        kpe_s.append(kpe)
        C_s.append(C.astype(dt))
        n_s.append(n.astype(dt))
        m_s.append(m.astype(dt))
    return (xp, xs, jnp.stack(ckv_p), jnp.stack(kpe_p), jnp.stack(C_p), jnp.stack(n_p), jnp.stack(m_p),
            jnp.stack(ckv_s), jnp.stack(kpe_s), jnp.stack(C_s), jnp.stack(n_s), jnp.stack(m_s))
```

```python
import functools
import math

import jax
import jax.numpy as jnp
from jax import lax
from jax.experimental import pallas as pl
from jax.experimental.pallas import tpu as pltpu

F32 = jnp.float32
BF16 = jnp.bfloat16

HEAD_OUT = 128
MLA_HEADS = 4
ML_HEADS = 4
MLA_NOPE = 128
MLA_ROPE = 64
Q_LORA = 384
KV_LORA = 256
ROPE_BASE = 10000.0
SM_SCALE = (MLA_NOPE + MLA_ROPE) ** -0.5
ML_QK = 64
ML_V = 128
ML_CHUNK = 64
IGATE_CAP = 15.0
N_EXPERTS = 32
TOP_K = 4
SWIGLU_LIMIT = 7.0
SWIGLU_ALPHA = 1.702
EPS = 1e-6
PAGE_SIZE = 128

LANES = 128
LOG2E = 1.4426950408889634
QK_SCALE = SM_SCALE * LOG2E
NEG_INF = float("-inf")

TOKEN_TILE = 512
ATTN_BLOCK = 512
PAGES_PER_STEP = 32
ATTN_SUB = 1024
MOE_ROWS = 256
COMBINE_TILE = 128
VMEM_LIMIT = 56 * 1024 * 1024

NT_DIMS = (((1,), (1,)), ((), ()))
TN_DIMS = (((0,), (0,)), ((), ()))


def _cparams(sem):
    return pltpu.CompilerParams(dimension_semantics=sem, vmem_limit_bytes=VMEM_LIMIT)


def _rms(x, g):
    return x * lax.rsqrt(jnp.mean(x * x, axis=-1, keepdims=True) + EPS) * g


def _dot(a, b):
    return jnp.dot(a, b, preferred_element_type=F32)


def _mod_kernel(c_ref, w_ref, b_ref, o_ref):
    c = c_ref[...]
    s = c * jax.nn.sigmoid(c)
    o_ref[...] = _dot(s.astype(BF16), w_ref[...].astype(BF16)) + b_ref[...]


def _modulation(c, w_ada, b_ada):
    nseq, d = c.shape
    n_out = w_ada.shape[1]
    return pl.pallas_call(
        _mod_kernel,
        out_shape=jax.ShapeDtypeStruct((nseq, n_out), F32),
        grid=(n_out // d,),
        in_specs=[pl.BlockSpec((nseq, d), lambda j: (0, 0)),
                  pl.BlockSpec((d, d), lambda j: (0, j)),
                  pl.BlockSpec((1, d), lambda j: (0, j))],
        out_specs=pl.BlockSpec((nseq, d), lambda j: (0, j)),
        compiler_params=_cparams(("arbitrary",)),
        name="modulation",
    )(c, w_ada, b_ada.reshape(1, n_out))


_C_QLAT = 0
_C_CKV = _C_QLAT + Q_LORA
_C_KPE = _C_CKV + KV_LORA
_C_KROT = _C_KPE + LANES
_C_MQ = _C_KROT + LANES
_C_MK = _C_MQ + ML_HEADS * ML_QK
_C_MV = _C_MK + ML_HEADS * ML_QK
_C_OG = _C_MV + ML_HEADS * ML_V
_C_GATE = _C_OG + ML_HEADS * ML_V
_C_END = _C_GATE + LANES


def _rot_half_cols(w):
    half = w.shape[-1] // 2
    return jnp.concatenate([-w[..., half:], w[..., :half]], axis=-1)


def _prep_proj_weights(w_in, w_q_up, w_uk, w_uv, b_igate, b_fgate):
    d = w_in.shape[0]
    o = 0
    parts = {}
    for name, size in (("qlat", Q_LORA), ("ckv", KV_LORA), ("kpe", MLA_ROPE), ("mq", ML_HEADS * ML_QK),
                       ("mk", ML_HEADS * ML_QK), ("mv", ML_HEADS * ML_V), ("ig", ML_HEADS), ("fg", ML_HEADS),
                       ("og", ML_HEADS * ML_V)):
        parts[name] = w_in[:, o:o + size]
        o += size
    z64 = jnp.zeros((d, LANES - MLA_ROPE), F32)
    zg = jnp.zeros((d, LANES - 2 * ML_HEADS), F32)
    w_in_r = jnp.concatenate(
        [parts["qlat"], parts["ckv"], parts["kpe"], z64, _rot_half_cols(parts["kpe"]), z64,
         parts["mq"], parts["mk"], parts["mv"], parts["og"], parts["ig"], parts["fg"], zg], axis=1).astype(BF16)
    b_gate = jnp.concatenate([b_igate, b_fgate, jnp.zeros((LANES - 2 * ML_HEADS,), F32)]).reshape(1, LANES)
    w_nope = w_q_up[:, :, :MLA_NOPE].reshape(Q_LORA, MLA_HEADS * MLA_NOPE)
    w_rope = w_q_up[:, :, MLA_NOPE:]
    zr = jnp.zeros((Q_LORA, MLA_HEADS, LANES - MLA_ROPE), F32)
    w_qr = jnp.concatenate([w_rope, zr], axis=-1).reshape(Q_LORA, MLA_HEADS * LANES)
    w_qt = jnp.concatenate([_rot_half_cols(w_rope), zr], axis=-1).reshape(Q_LORA, MLA_HEADS * LANES)
    w_uk2 = w_uk.reshape(KV_LORA, MLA_HEADS * MLA_NOPE).astype(BF16)
    w_uv2 = w_uv.reshape(KV_LORA, MLA_HEADS * HEAD_OUT).astype(BF16)
    w_ukT = jnp.transpose(w_uk, (1, 2, 0)).astype(BF16)
    return dict(w_in=w_in_r, b_gate=b_gate, w_qn=w_nope.astype(BF16), w_qr=w_qr.astype(BF16),
                w_qt=w_qt.astype(BF16), w_uk=w_uk2, w_uv=w_uv2, w_ukT=w_ukT)


def _rope_tables(pos):
    half = MLA_ROPE // 2
    inv_freq = ROPE_BASE ** (-jnp.arange(half, dtype=F32) / half)
    ang = pos.astype(F32)[:, None] * inv_freq[None, :]
    z = jnp.zeros((pos.shape[0], LANES - MLA_ROPE), F32)
    cos = jnp.concatenate([jnp.cos(ang), jnp.cos(ang), z], axis=1)
    sin = jnp.concatenate([jnp.sin(ang), jnp.sin(ang), z], axis=1)
    return cos, sin


def _chunk_tril(tm, chunk):
    r = jnp.arange(tm)
    same = (r[:, None] // chunk) == (r[None, :] // chunk)
    return (same & (r[None, :] <= r[:, None])).astype(BF16)


def _split3(x):
    hi = x.astype(BF16)
    r1 = x - hi.astype(F32)
    mid = r1.astype(BF16)
    lo = (r1 - mid.astype(F32)).astype(BF16)
    return hi, mid, lo


def _proj_kernel(*refs, is_prompt):
    (x_ref, mod_ref, gpre_ref, win_ref, bg_ref, gq_ref, wqn_ref, wqr_ref, wqt_ref, gkv_ref,
     cos_ref, sin_ref, tril_ref) = refs[:13]
    if is_prompt:
        wuk_ref, wuv_ref = refs[13:15]
        q_ref, k_ref, v_ref, ckv_ref, kpe_ref, mq_ref, mk_ref, mv_ref, g_ref, cum_ref, og_ref = refs[15:]
    else:
        wukT_ref = refs[13]
        qa_ref, qr_ref, ckv_ref, kpe_ref, mq_ref, mk_ref, mv_ref, g_ref, cum_ref, og_ref = refs[14:]

    x = x_ref[...]
    mod = mod_ref[...]
    nb, t, d = x.shape
    h = _rms(x, gpre_ref[...]) * (1.0 + mod[:, 1:2, :]) + mod[:, 0:1, :]
    hb = h.reshape(nb * t, d).astype(BF16)

    def zcols(a, b):
        return _dot(hb, win_ref[:, a:b])

    cos = cos_ref[...]
    sin = sin_ref[...]

    qn = _rms(zcols(_C_QLAT, _C_CKV), gq_ref[...]).astype(BF16)
    q_nope = _dot(qn, wqn_ref[...]) * QK_SCALE
    q_r = _dot(qn, wqr_ref[...])
    q_t = _dot(qn, wqt_ref[...])
    ckv = _rms(zcols(_C_CKV, _C_KPE), gkv_ref[...])
    ckv_ref[...] = ckv
    kpe = zcols(_C_KPE, _C_KROT) * cos + zcols(_C_KROT, _C_MQ) * sin
    kpe_ref[...] = kpe[:, :MLA_ROPE]
    for hd in range(MLA_HEADS):
        sl = slice(hd * LANES, (hd + 1) * LANES)
        roped = (q_r[:, sl] * cos + q_t[:, sl] * sin) * QK_SCALE
        if is_prompt:
            q_ref[:, 2 * hd * LANES:(2 * hd + 1) * LANES] = q_nope[:, sl].astype(BF16)
            q_ref[:, (2 * hd + 1) * LANES:(2 * hd + 2) * LANES] = roped.astype(BF16)
        else:
            qa_ref[:, hd * KV_LORA:(hd + 1) * KV_LORA] = _dot(q_nope[:, sl].astype(BF16), wukT_ref[hd])
            qr_ref[:, sl] = roped
    if is_prompt:
        cb = ckv.astype(BF16)
        k_nope = _dot(cb, wuk_ref[...])
        kpe_b = kpe.astype(BF16)
        for hd in range(MLA_HEADS):
            k_ref[:, 2 * hd * LANES:(2 * hd + 1) * LANES] = k_nope[:, hd * LANES:(hd + 1) * LANES].astype(BF16)
            k_ref[:, (2 * hd + 1) * LANES:(2 * hd + 2) * LANES] = kpe_b
        v_ref[...] = _dot(cb, wuv_ref[...]).astype(BF16)

    mq_ref[...] = zcols(_C_MQ, _C_MK) * (ML_QK ** -0.5)
    mk_ref[...] = zcols(_C_MK, _C_MV)
    mv_ref[...] = zcols(_C_MV, _C_OG)
    og_ref[...] = jax.nn.sigmoid(zcols(_C_OG, _C_GATE))
    zg = zcols(_C_GATE, _C_END) + bg_ref[...]
    ig = IGATE_CAP * jnp.tanh(zg / IGATE_CAP)
    lf = -(jnp.maximum(-zg, 0.0) + jnp.log1p(jnp.exp(-jnp.abs(zg))))
    lane = lax.broadcasted_iota(jnp.int32, zg.shape, 1)
    g = jnp.where(lane < ML_HEADS, ig, lf)
    g_ref[...] = g
    tril = tril_ref[...]
    hi, mid, lo = _split3(g)
    cum_ref[...] = _dot(tril, hi) + _dot(tril, mid) + _dot(tril, lo)


def _project(x, mod, pos_tab, chunk, g_pre, g_qlat, g_kvlat, pw, is_prompt):
    nseq, T, d = x.shape
    n = nseq * T
    if is_prompt:
        nb, t = 1, min(TOKEN_TILE, T)
    else:
        t = T
        nb = max(1, min(nseq, TOKEN_TILE // T))
    assert T % t == 0 and nseq % nb == 0
    tm = nb * t
    tiles_per_seq = T // t
    grid = (n // tm,)
    cos, sin = pos_tab
    if is_prompt:
        x_map = lambda i: (i // tiles_per_seq, i % tiles_per_seq, 0)
        mod_map = lambda i: (i // tiles_per_seq, 0, 0)
        tab_map = lambda i: (i % tiles_per_seq, 0)
    else:
        x_map = lambda i: (i, 0, 0)
        mod_map = lambda i: (i, 0, 0)
        tab_map = lambda i: (0, 0)
        cos = jnp.tile(cos, (nb, 1))
        sin = jnp.tile(sin, (nb, 1))
    const = lambda i: (0, 0)
    row = lambda i: (i, 0)
    tril = _chunk_tril(tm, chunk)
    in_specs = [
        pl.BlockSpec((nb, t, d), x_map),
        pl.BlockSpec((nb, 6, d), mod_map),
        pl.BlockSpec((1, d), const),
        pl.BlockSpec((d, _C_END), const),
        pl.BlockSpec((1, LANES), const),
        pl.BlockSpec((1, Q_LORA), const),
        pl.BlockSpec((Q_LORA, MLA_HEADS * MLA_NOPE), const),
        pl.BlockSpec((Q_LORA, MLA_HEADS * LANES), const),
        pl.BlockSpec((Q_LORA, MLA_HEADS * LANES), const),
        pl.BlockSpec((1, KV_LORA), const),
        pl.BlockSpec((tm, LANES), tab_map),
        pl.BlockSpec((tm, LANES), tab_map),
        pl.BlockSpec((tm, tm), const),
    ]
    args = [x, mod, g_pre.reshape(1, d), pw["w_in"], pw["b_gate"], g_qlat.reshape(1, Q_LORA), pw["w_qn"],
            pw["w_qr"], pw["w_qt"], g_kvlat.reshape(1, KV_LORA), cos, sin, tril]

    def out(cols, dtype=F32):
        return jax.ShapeDtypeStruct((n, cols), dtype), pl.BlockSpec((tm, cols), row)

    if is_prompt:
        in_specs += [pl.BlockSpec((KV_LORA, MLA_HEADS * MLA_NOPE), const),
                     pl.BlockSpec((KV_LORA, MLA_HEADS * HEAD_OUT), const)]
        args += [pw["w_uk"], pw["w_uv"]]
        outs = [out(MLA_HEADS * 2 * LANES, BF16), out(MLA_HEADS * 2 * LANES, BF16), out(MLA_HEADS * HEAD_OUT, BF16)]
    else:
        in_specs += [pl.BlockSpec((MLA_HEADS, MLA_NOPE, KV_LORA), lambda i: (0, 0, 0))]
        args += [pw["w_ukT"]]
        outs = [out(MLA_HEADS * KV_LORA), out(MLA_HEADS * LANES)]
    outs += [out(KV_LORA), out(MLA_ROPE), out(ML_HEADS * ML_QK), out(ML_HEADS * ML_QK), out(ML_HEADS * ML_V),
             out(LANES), out(LANES), out(ML_HEADS * ML_V)]
    return pl.pallas_call(
        functools.partial(_proj_kernel, is_prompt=is_prompt),
        out_shape=[o[0] for o in outs],
        grid=grid,
        in_specs=in_specs,
        out_specs=[o[1] for o in outs],
        compiler_params=_cparams(("arbitrary",)),
        name="proj_prompt" if is_prompt else "proj_sample",
    )(*args)


def _attn_kernel(qi_ref, ki_ref, q_ref, k_ref, v_ref, o_ref, m_ref, l_ref, acc_ref):
    p = pl.program_id(1)
    qi = qi_ref[p]
    ki = ki_ref[p]
    tq = q_ref.shape[0]
    tk = k_ref.shape[0]

    @pl.when(ki == 0)
    def _():
        m_ref[...] = jnp.full(m_ref.shape, NEG_INF, F32)
        l_ref[...] = jnp.zeros(l_ref.shape, F32)
        acc_ref[...] = jnp.zeros(acc_ref.shape, F32)

    def update(masked):
        for hd in range(MLA_HEADS):
            qh = q_ref[:, hd * 2 * LANES:(hd + 1) * 2 * LANES]
            kh = k_ref[:, hd * 2 * LANES:(hd + 1) * 2 * LANES]
            s = lax.dot_general(qh, kh, NT_DIMS, preferred_element_type=F32)
            if masked:
                row = lax.broadcasted_iota(jnp.int32, s.shape, 0)
                col = lax.broadcasted_iota(jnp.int32, s.shape, 1)
                s = jnp.where(col <= row, s, NEG_INF)
            m_prev = m_ref[hd]
            m_next = jnp.maximum(m_prev, jnp.max(s, axis=1, keepdims=True))
            alpha = jnp.exp2(m_prev - m_next)
            pr = jnp.exp2(s - jnp.concatenate([m_next] * (tk // LANES), axis=1))
            l_ref[hd] = alpha * l_ref[hd] + jnp.sum(pr, axis=1, keepdims=True)
            acc_ref[hd] = alpha * acc_ref[hd] + _dot(pr.astype(BF16), v_ref[:, hd * HEAD_OUT:(hd + 1) * HEAD_OUT])
            m_ref[hd] = m_next

    @pl.when(ki < qi)
    def _():
        update(False)

    @pl.when(ki == qi)
    def _():
        update(True)
        for hd in range(MLA_HEADS):
            o_ref[:, hd * HEAD_OUT:(hd + 1) * HEAD_OUT] = acc_ref[hd] / l_ref[hd]


def _prompt_attention(q, k, v, nseq, S):
    blk = min(ATTN_BLOCK, S)
    nq = S // blk
    pairs = [(i, j) for i in range(nq) for j in range(i + 1)]
    qi_tab = jnp.asarray([p[0] for p in pairs], jnp.int32)
    ki_tab = jnp.asarray([p[1] for p in pairs], jnp.int32)
    wq = MLA_HEADS * 2 * LANES
    wv = MLA_HEADS * HEAD_OUT
    return pl.pallas_call(
        _attn_kernel,
        out_shape=jax.ShapeDtypeStruct((nseq * S, wv), F32),
        grid_spec=pltpu.PrefetchScalarGridSpec(
            num_scalar_prefetch=2,
            grid=(nseq, len(pairs)),
            in_specs=[pl.BlockSpec((blk, wq), lambda b, p, qi, ki: (b * nq + qi[p], 0)),
                      pl.BlockSpec((blk, wq), lambda b, p, qi, ki: (b * nq + ki[p], 0)),
                      pl.BlockSpec((blk, wv), lambda b, p, qi, ki: (b * nq + ki[p], 0))],
            out_specs=pl.BlockSpec((blk, wv), lambda b, p, qi, ki: (b * nq + qi[p], 0)),
            scratch_shapes=[pltpu.VMEM((MLA_HEADS, blk, LANES), F32),
                            pltpu.VMEM((MLA_HEADS, blk, LANES), F32),
                            pltpu.VMEM((MLA_HEADS, blk, HEAD_OUT), F32)]),
        compiler_params=_cparams(("arbitrary", "arbitrary")),
        name="prompt_attention",
    )(qi_tab, ki_tab, q, k, v)


def _sattn_kernel(pt_ref, qa_ref, qr_ref, ckvn_ref, kpen_ref, wuv_ref, cckv_hbm, ckpe_hbm, o_ref,
                  ckv_buf, kpe_buf, sem, q_s, qr_s, kn_s, pn_s, m_ref, l_ref, acc_ref, *, layer, npg, nchunks, sub):
    b = pl.program_id(0)
    c = pl.program_id(1)
    step = b * nchunks + c
    nsteps = pl.num_programs(0) * nchunks
    slot = step % 2
    T = qa_ref.shape[0]
    rows = MLA_HEADS * T

    def page_copies(page_of, slt):
        cps = []
        for j in range(npg):
            pg = page_of(j)
            dst = pl.ds(j * PAGE_SIZE, PAGE_SIZE)
            cps.append(pltpu.make_async_copy(cckv_hbm.at[layer, pg], ckv_buf.at[slt, dst, :], sem.at[0, slt]))
            cps.append(pltpu.make_async_copy(ckpe_hbm.at[layer, pg], kpe_buf.at[slt, dst, :], sem.at[1, slt]))
        return cps

    def fetch(stp, slt):
        for cp in page_copies(lambda j: pt_ref[stp * npg + j], slt):
            cp.start()

    @pl.when(step == 0)
    def _():
        fetch(0, 0)

    @pl.when(step + 1 < nsteps)
    def _():
        fetch(step + 1, 1 - slot)

    @pl.when(c == 0)
    def _():
        m_ref[...] = jnp.full(m_ref.shape, NEG_INF, F32)
        l_ref[...] = jnp.zeros(l_ref.shape, F32)
        acc_ref[...] = jnp.zeros(acc_ref.shape, F32)
        qa = qa_ref[...]
        qr = qr_ref[...]
        q_s[...] = jnp.concatenate([qa[:, hd * KV_LORA:(hd + 1) * KV_LORA] for hd in range(MLA_HEADS)],
                                   axis=0).astype(BF16)
        qr_s[...] = jnp.concatenate([qr[:, hd * LANES:(hd + 1) * LANES] for hd in range(MLA_HEADS)],
                                    axis=0).astype(BF16)

    for cp in page_copies(lambda j: 0, slot):
        cp.wait()

    q = q_s[...]
    qr = qr_s[:, :MLA_ROPE]

    def update(kc, kp, mask):
        s = (lax.dot_general(q, kc, NT_DIMS, preferred_element_type=F32)
             + lax.dot_general(qr, kp, NT_DIMS, preferred_element_type=F32))
        if mask is not None:
            s = jnp.where(mask, s, NEG_INF)
        m_prev = m_ref[...]
        m_next = jnp.maximum(m_prev, jnp.max(s, axis=1, keepdims=True))
        alpha = jnp.exp2(m_prev - m_next)
        pr = jnp.exp2(s - jnp.concatenate([m_next] * (s.shape[1] // LANES), axis=1))
        l_ref[...] = alpha * l_ref[...] + jnp.sum(pr, axis=1, keepdims=True)
        acc_ref[...] = (jnp.concatenate([alpha] * (KV_LORA // LANES), axis=1) * acc_ref[...]
                        + _dot(pr.astype(BF16), kc))
        m_ref[...] = m_next

    for sb in range(npg * PAGE_SIZE // sub):
        kc = ckv_buf[slot, sb * sub:(sb + 1) * sub, :].astype(BF16)
        kp = kpe_buf[slot, sb * sub:(sb + 1) * sub, :].astype(BF16)
        update(kc, kp, None)

    @pl.when(c == nchunks - 1)
    def _():
        kn_s[...] = jnp.zeros(kn_s.shape, F32)
        pn_s[...] = jnp.zeros(pn_s.shape, F32)
        kn_s[0:T, :] = ckvn_ref[...]
        pn_s[0:T, :] = kpen_ref[...]
        row = lax.broadcasted_iota(jnp.int32, (rows, LANES), 0)
        col = lax.broadcasted_iota(jnp.int32, (rows, LANES), 1)
        update(kn_s[...].astype(BF16), pn_s[...].astype(BF16), col <= row % T)
        o_lat = acc_ref[...] / jnp.concatenate([l_ref[...]] * (KV_LORA // LANES), axis=1)
        for hd in range(MLA_HEADS):
            o_ref[:, hd * HEAD_OUT:(hd + 1) * HEAD_OUT] = _dot(o_lat[hd * T:(hd + 1) * T, :], wuv_ref[hd])


def _sample_attention(qa, qr, ckv_new, kpe_new, cache_ckv, cache_kpe, layer, page_table, w_uv):
    nseq, n_pages = page_table.shape
    T = qa.shape[0] // nseq
    npg = min(PAGES_PER_STEP, n_pages)
    assert n_pages % npg == 0
    nchunks = n_pages // npg
    sub = min(ATTN_SUB, npg * PAGE_SIZE)
    rows = MLA_HEADS * T
    seq = lambda b, c, pt: (b, 0)
    kern = functools.partial(_sattn_kernel, layer=layer, npg=npg, nchunks=nchunks, sub=sub)
    return pl.pallas_call(
        kern,
        out_shape=jax.ShapeDtypeStruct((nseq * T, MLA_HEADS * HEAD_OUT), F32),
        grid_spec=pltpu.PrefetchScalarGridSpec(
            num_scalar_prefetch=1,
            grid=(nseq, nchunks),
            in_specs=[pl.BlockSpec((T, MLA_HEADS * KV_LORA), seq),
                      pl.BlockSpec((T, MLA_HEADS * LANES), seq),
                      pl.BlockSpec((T, KV_LORA), seq),
                      pl.BlockSpec((T, MLA_ROPE), seq),
                      pl.BlockSpec((MLA_HEADS, KV_LORA, HEAD_OUT), lambda b, c, pt: (0, 0, 0)),
                      pl.BlockSpec(memory_space=pl.ANY),
                      pl.BlockSpec(memory_space=pl.ANY)],
            out_specs=pl.BlockSpec((T, MLA_HEADS * HEAD_OUT), seq),
            scratch_shapes=[pltpu.VMEM((2, npg * PAGE_SIZE, KV_LORA), F32),
                            pltpu.VMEM((2, npg * PAGE_SIZE, MLA_ROPE), F32),
                            pltpu.SemaphoreType.DMA((2, 2)),
                            pltpu.VMEM((rows, KV_LORA), BF16),
                            pltpu.VMEM((rows, LANES), BF16),
                            pltpu.VMEM((LANES, KV_LORA), F32),
                            pltpu.VMEM((LANES, MLA_ROPE), F32),
                            pltpu.VMEM((rows, LANES), F32),
                            pltpu.VMEM((rows, LANES), F32),
                            pltpu.VMEM((rows, KV_LORA), F32)]),
        compiler_params=_cparams(("arbitrary", "arbitrary")),
        name="sample_attention",
    )(page_table.reshape(-1), qa, qr, ckv_new, kpe_new, w_uv, cache_ckv, cache_kpe)


def _mlstm_kernel(mq_ref, mk_ref, mv_ref, g_ref, cum_ref, c0_ref, n0_ref, m0_ref,
                  h_ref, c_ref, n_ref, m_ref, *, unroll_seqs, dot_dtype):
    ci = pl.program_id(1)
    nbs, L = mq_ref.shape[0], mq_ref.shape[1]

    @pl.when(ci == 0)
    def _():
        c_ref[...] = c0_ref[...]
        n_ref[...] = n0_ref[...]
        m_ref[...] = m0_ref[...]

    trow = lax.broadcasted_iota(jnp.int32, (L, L), 0)
    tcol = lax.broadcasted_iota(jnp.int32, (L, L), 1)
    causal = tcol <= trow

    def one_seq(i):
        g = g_ref[i]
        cum = cum_ref[i]
        g_t = g.T
        cum_t = cum.T
        mq = mq_ref[i]
        mk = mk_ref[i]
        mv = mv_ref[i]
        for hd in range(ML_HEADS):
            q = mq[:, hd * ML_QK:(hd + 1) * ML_QK]
            k = mk[:, hd * ML_QK:(hd + 1) * ML_QK]
            v = mv[:, hd * ML_V:(hd + 1) * ML_V]
            ig_col = g[:, hd:hd + 1]
            ig_row = g_t[hd:hd + 1, :]
            b_col = cum[:, ML_HEADS + hd:ML_HEADS + hd + 1]
            b_row = cum_t[ML_HEADS + hd:ML_HEADS + hd + 1, :]
            m_prev = m_ref[i, hd:hd + 1, 0:1]
            c_prev = c_ref[i, hd]
            n_prev = n_ref[i, hd:hd + 1, :]
            log_d = jnp.where(causal, b_col - b_row + ig_row, NEG_INF)
            log_inter = b_col + m_prev
            m_t = jnp.maximum(log_inter, jnp.max(log_d, axis=1, keepdims=True))
            d = jnp.exp(log_d - m_t)
            inter = jnp.exp(log_inter - m_t)
            qd = q.astype(dot_dtype)
            kd = k.astype(dot_dtype)
            s = lax.dot_general(qd, kd, NT_DIMS, preferred_element_type=F32) * d
            num = (inter * lax.dot_general(qd, c_prev.astype(dot_dtype), NT_DIMS, preferred_element_type=F32)
                   + _dot(s.astype(dot_dtype), v.astype(dot_dtype)))
            den = inter * jnp.sum(q * n_prev, axis=1, keepdims=True) + jnp.sum(s, axis=1, keepdims=True)
            h_ref[i, :, hd * ML_V:(hd + 1) * ML_V] = num / jnp.maximum(jnp.abs(den), jnp.exp(-m_t))
            m_new = m_t[L - 1:L, :]
            b_last = b_col[L - 1:L, :]
            w_col = jnp.exp(b_last - b_col + ig_col - m_new)
            decay = jnp.exp(b_last + m_prev - m_new)
            c_ref[i, hd] = decay * c_prev + lax.dot_general((w_col * v).astype(dot_dtype), kd, TN_DIMS,
                                                            preferred_element_type=F32)
            n_ref[i, hd:hd + 1, :] = decay * n_prev + jnp.sum(w_col * k, axis=0, keepdims=True)
            m_ref[i, hd:hd + 1, :] = jnp.broadcast_to(m_new, (1, LANES))

    if unroll_seqs:
        for i in range(nbs):
            one_seq(i)
    else:
        def body(i, carry):
            one_seq(i)
            return carry
        lax.fori_loop(0, nbs, body, 0)


def _mlstm(mq, mk, mv, g, cum, c0, n0, m0, nseq, T):
    L = ML_CHUNK if T % ML_CHUNK == 0 else T
    nc = T // L
    nbs = nseq if nseq <= 2 else 16
    assert nseq % nbs == 0
    r3 = lambda a: a.reshape(nseq, T, a.shape[-1])
    tok = lambda gi, ci: (gi, ci, 0)
    st4 = lambda gi, ci: (gi, 0, 0, 0)
    st3 = lambda gi, ci: (gi, 0, 0)
    m0b = jnp.broadcast_to(m0.astype(F32)[:, :, None], (nseq, ML_HEADS, LANES))
    kern = functools.partial(_mlstm_kernel, unroll_seqs=nbs <= 2, dot_dtype=BF16 if L >= 16 else F32)
    h, c, n, m = pl.pallas_call(
        kern,
        out_shape=[jax.ShapeDtypeStruct((nseq, T, ML_HEADS * ML_V), F32),
                   jax.ShapeDtypeStruct((nseq, ML_HEADS, ML_V, ML_QK), F32),
                   jax.ShapeDtypeStruct((nseq, ML_HEADS, ML_QK), F32),
                   jax.ShapeDtypeStruct((nseq, ML_HEADS, LANES), F32)],
        grid=(nseq // nbs, nc),
        in_specs=[pl.BlockSpec((nbs, L, ML_HEADS * ML_QK), tok),
                  pl.BlockSpec((nbs, L, ML_HEADS * ML_QK), tok),
                  pl.BlockSpec((nbs, L, ML_HEADS * ML_V), tok),
                  pl.BlockSpec((nbs, L, LANES), tok),
                  pl.BlockSpec((nbs, L, LANES), tok),
                  pl.BlockSpec((nbs, ML_HEADS, ML_V, ML_QK), st4),
                  pl.BlockSpec((nbs, ML_HEADS, ML_QK), st3),
                  pl.BlockSpec((nbs, ML_HEADS, LANES), st3)],
        out_specs=[pl.BlockSpec((nbs, L, ML_HEADS * ML_V), tok),
                   pl.BlockSpec((nbs, ML_HEADS, ML_V, ML_QK), st4),
                   pl.BlockSpec((nbs, ML_HEADS, ML_QK), st3),
                   pl.BlockSpec((nbs, ML_HEADS, LANES), st3)],
        compiler_params=_cparams(("arbitrary", "arbitrary")),
        name="mlstm_prompt" if nseq <= 2 else "mlstm_sample",
    )(r3(mq), r3(mk), r3(mv), r3(g), r3(cum), c0.astype(F32), n0.astype(F32), m0b)
    return h.reshape(nseq * T, ML_HEADS * ML_V), c, n, m[:, :, 0]


def _merge_kernel(x_ref, mod_ref, omla_ref, hml_ref, og_ref, gh_ref, wout_ref, gpm_ref, gpf_ref,
                  wrh_ref, wrl_ref, br_ref, x1_ref, h2_ref, idx_ref, gate_ref):
    nb, t, d = x_ref.shape
    tm = nb * t
    gh = gh_ref[...]
    parts = []
    for hd in range(MLA_HEADS + ML_HEADS):
        if hd < MLA_HEADS:
            xh = omla_ref[:, hd * HEAD_OUT:(hd + 1) * HEAD_OUT]
        else:
            j = hd - MLA_HEADS
            xh = hml_ref[:, j * HEAD_OUT:(j + 1) * HEAD_OUT]
        yh = _rms(xh, gh[:, hd * HEAD_OUT:(hd + 1) * HEAD_OUT])
        if hd >= MLA_HEADS:
            yh = yh * og_ref[:, j * HEAD_OUT:(j + 1) * HEAD_OUT]
        parts.append(yh.astype(BF16))
    merged = jnp.concatenate(parts, axis=1)
    mix = _dot(merged, wout_ref[...])
    mod = mod_ref[...]
    x1 = x_ref[...] + mod[:, 2:3, :] * _rms(mix, gpm_ref[...]).reshape(nb, t, d)
    x1_ref[...] = x1
    h2 = (_rms(x1, gpf_ref[...]) * (1.0 + mod[:, 4:5, :]) + mod[:, 3:4, :]).reshape(tm, d)
    h2_ref[...] = h2
    hh = h2.astype(BF16)
    hl = (h2 - hh.astype(F32)).astype(BF16)
    wrh = wrh_ref[...]
    logits = (lax.dot_general(wrh, hh, NT_DIMS, preferred_element_type=F32)
              + lax.dot_general(wrh, hl, NT_DIMS, preferred_element_type=F32)
              + lax.dot_general(wrl_ref[...], hh, NT_DIMS, preferred_element_type=F32)) + br_ref[...]
    eidx = lax.broadcasted_iota(jnp.int32, logits.shape, 0)
    work = logits
    vals, idxs = [], []
    for _ in range(TOP_K):
        mx = jnp.max(work, axis=0, keepdims=True)
        sel = jnp.min(jnp.where(work == mx, eidx, N_EXPERTS), axis=0, keepdims=True)
        vals.append(mx)
        idxs.append(sel)
        work = jnp.where(eidx == sel, NEG_INF, work)
    ex = [jnp.exp(v - vals[0]) for v in vals]
    tot = ex[0] + ex[1] + ex[2] + ex[3]
    zi = jnp.zeros((8 - TOP_K, tm), jnp.int32)
    zf = jnp.zeros((8 - TOP_K, tm), F32)
    idx_ref[...] = jnp.concatenate(idxs + [zi], axis=0)
    gate_ref[...] = jnp.concatenate([e / tot for e in ex] + [zf], axis=0)


def _merge(x, mod, o_mla, h_ml, og, g_heads, w_out, g_post_mix, g_pre_ffn, wr_hi, wr_lo, b_router, is_prompt):
    nseq, T, d = x.shape
    n = nseq * T
    if is_prompt:
        nb, t = 1, min(TOKEN_TILE, T)
    else:
        t = T
        nb = max(1, min(nseq, TOKEN_TILE // T))
    tm = nb * t
    tiles_per_seq = T // t
    if is_prompt:
        x_map = lambda i: (i // tiles_per_seq, i % tiles_per_seq, 0)
        mod_map = lambda i: (i // tiles_per_seq, 0, 0)
    else:
        x_map = lambda i: (i, 0, 0)
        mod_map = lambda i: (i, 0, 0)
    const = lambda i: (0, 0)
    row = lambda i: (i, 0)
    col = lambda i: (0, i)
    wh = MLA_HEADS * HEAD_OUT
    return pl.pallas_call(
        _merge_kernel,
        out_shape=[jax.ShapeDtypeStruct((nseq, T, d), F32), jax.ShapeDtypeStruct((n, d), F32),
                   jax.ShapeDtypeStruct((8, n), jnp.int32), jax.ShapeDtypeStruct((8, n), F32)],
        grid=(n // tm,),
        in_specs=[pl.BlockSpec((nb, t, d), x_map),
                  pl.BlockSpec((nb, 6, d), mod_map),
                  pl.BlockSpec((tm, wh), row),
                  pl.BlockSpec((tm, wh), row),
                  pl.BlockSpec((tm, wh), row),
                  pl.BlockSpec((1, 2 * wh), const),
                  pl.BlockSpec((2 * wh, d), const),
                  pl.BlockSpec((1, d), const),
                  pl.BlockSpec((1, d), const),
                  pl.BlockSpec((N_EXPERTS, d), const),
                  pl.BlockSpec((N_EXPERTS, d), const),
                  pl.BlockSpec((N_EXPERTS, 1), const)],
        out_specs=[pl.BlockSpec((nb, t, d), x_map), pl.BlockSpec((tm, d), row),
                   pl.BlockSpec((8, tm), col), pl.BlockSpec((8, tm), col)],
        compiler_params=_cparams(("arbitrary",)),
        name="merge_prompt" if is_prompt else "merge_sample",
    )(x, mod, o_mla, h_ml, og, g_heads.reshape(1, 2 * wh), w_out, g_post_mix.reshape(1, d),
      g_pre_ffn.reshape(1, d), wr_hi, wr_lo, b_router.reshape(N_EXPERTS, 1))


def _moe_kernel(be_ref, rt_ref, nu_ref, h2_hbm, wgu_ref, bgu_ref, wd_ref, bd_ref, y_ref,
                xbuf, sem, wgu_s, wd_s):
    blk = pl.program_id(0)
    nblk = pl.num_programs(0)
    slot = blk % 2
    tb, d = y_ref.shape
    de = wd_ref.shape[0]
    n_used = nu_ref[0]

    def gather(bk, slt):
        def body(r, carry):
            tok = rt_ref[bk * tb + r]
            pltpu.make_async_copy(h2_hbm.at[pl.ds(tok, 1), :], xbuf.at[slt, pl.ds(r, 1), :], sem.at[slt]).start()
            return carry
        lax.fori_loop(0, tb, body, 0, unroll=8)

    @pl.when(jnp.logical_and(blk == 0, n_used > 0))
    def _():
        gather(0, 0)

    @pl.when(blk + 1 < n_used)
    def _():
        gather(blk + 1, 1 - slot)

    e = be_ref[blk]
    e_prev = be_ref[jnp.maximum(blk - 1, 0)]

    @pl.when(jnp.logical_and(blk < n_used, jnp.logical_or(blk == 0, e != e_prev)))
    def _():
        wgu_s[...] = wgu_ref[...].astype(BF16)
        wd_s[...] = wd_ref[...].astype(BF16)

    @pl.when(blk < n_used)
    def _():
        def wait_body(r, carry):
            pltpu.make_async_copy(h2_hbm.at[pl.ds(0, 1), :], xbuf.at[slot, pl.ds(r, 1), :], sem.at[slot]).wait()
            return carry
        lax.fori_loop(0, tb, wait_body, 0, unroll=8)
        xb = xbuf[slot].astype(BF16)
        gu = _dot(xb, wgu_s[...]) + bgu_ref[...]
        gt = jnp.minimum(gu[:, :de], SWIGLU_LIMIT)
        up = jnp.clip(gu[:, de:], -SWIGLU_LIMIT, SWIGLU_LIMIT)
        act = gt * jax.nn.sigmoid(SWIGLU_ALPHA * gt) * (up + 1.0)
        y_ref[...] = _dot(act.astype(BF16), wd_s[...]) + bd_ref[...]

    @pl.when(blk >= n_used)
    def _():
        y_ref[...] = jnp.zeros(y_ref.shape, F32)


def _moe_experts(h2, block_e, row_tok, n_used, w_gate_up, b_gate_up, w_down, b_down):
    n, d = h2.shape
    ne, _, de2 = w_gate_up.shape
    de = w_down.shape[1]
    n_blocks = block_e.shape[0]
    tb = MOE_ROWS
    return pl.pallas_call(
        _moe_kernel,
        out_shape=jax.ShapeDtypeStruct((n_blocks * tb, d), F32),
        grid_spec=pltpu.PrefetchScalarGridSpec(
            num_scalar_prefetch=3,
            grid=(n_blocks,),
            in_specs=[pl.BlockSpec(memory_space=pl.ANY),
                      pl.BlockSpec((None, d, de2), lambda i, be, rt, nu: (be[i], 0, 0)),
                      pl.BlockSpec((None, 1, de2), lambda i, be, rt, nu: (be[i], 0, 0)),
                      pl.BlockSpec((None, de, d), lambda i, be, rt, nu: (be[i], 0, 0)),
                      pl.BlockSpec((None, 1, d), lambda i, be, rt, nu: (be[i], 0, 0))],
            out_specs=pl.BlockSpec((tb, d), lambda i, be, rt, nu: (i, 0)),
            scratch_shapes=[pltpu.VMEM((2, tb, d), F32),
                            pltpu.SemaphoreType.DMA((2,)),
                            pltpu.VMEM((d, de2), BF16),
                            pltpu.VMEM((de, d), BF16)]),
        compiler_params=_cparams(("arbitrary",)),
        name="moe_experts",
    )(block_e, row_tok, n_used, h2, w_gate_up, b_gate_up.reshape(ne, 1, de2), w_down, b_down.reshape(ne, 1, d))


def _route(top_idx, n_blocks):
    n = top_idx.shape[0]
    m = n * TOP_K
    tb = MOE_ROWS
    flat_e = top_idx.reshape(m)
    onehot = (flat_e[:, None] == jnp.arange(N_EXPERTS, dtype=jnp.int32)[None, :]).astype(jnp.int32)
    csum = jnp.cumsum(onehot, axis=0)
    rank = jnp.take_along_axis(csum, flat_e[:, None], axis=1)[:, 0] - 1
    counts = csum[-1]
    padded = (counts + tb - 1) // tb * tb
    pends = jnp.cumsum(padded)
    pstarts = pends - padded
    dest = (pstarts[flat_e] + rank).astype(jnp.int32)
    row_tok = jnp.zeros((n_blocks * tb,), jnp.int32).at[dest].set(jnp.arange(m, dtype=jnp.int32) // TOP_K)
    block_e = jnp.minimum(jnp.searchsorted(pends, jnp.arange(n_blocks, dtype=jnp.int32) * tb, side="right"),
                          N_EXPERTS - 1).astype(jnp.int32)
    n_used = (pends[-1] // tb).astype(jnp.int32).reshape(1)
    return dest, row_tok, block_e, n_used


def _combine_kernel(dest_ref, yb_hbm, x1_ref, mod_ref, gate_ref, gpost_ref, o_ref, ybuf, sem, *, base):
    i = pl.program_id(0)
    nsteps = pl.num_programs(0)
    slot = i % 2
    nb, t, d = x1_ref.shape
    tm = nb * t

    def gather(step, slt):
        def body(r, carry):
            for kk in range(TOP_K):
                row = dest_ref[base + (step * tm + r) * TOP_K + kk]
                pltpu.make_async_copy(yb_hbm.at[pl.ds(row, 1), :], ybuf.at[slt, kk, pl.ds(r, 1), :],
                                      sem.at[slt]).start()
            return carry
        lax.fori_loop(0, tm, body, 0, unroll=4)

    @pl.when(i == 0)
    def _():
        gather(0, 0)

    @pl.when(i + 1 < nsteps)
    def _():
        gather(i + 1, 1 - slot)

    def wait_body(r, carry):
        for kk in range(TOP_K):
            pltpu.make_async_copy(yb_hbm.at[pl.ds(0, 1), :], ybuf.at[slot, kk, pl.ds(r, 1), :], sem.at[slot]).wait()
        return carry
    lax.fori_loop(0, tm, wait_body, 0, unroll=4)

    gate = gate_ref[...]
    y = gate[:, 0:1] * ybuf[slot, 0]
    for kk in range(1, TOP_K):
        y = y + gate[:, kk:kk + 1] * ybuf[slot, kk]
    mod = mod_ref[...]
    o_ref[...] = x1_ref[...] + mod[:, 5:6, :] * _rms(y, gpost_ref[...]).reshape(nb, t, d)


def _combine(yb, dest, base, x1, mod, gates, g_post_ffn, is_prompt):
    nseq, T, d = x1.shape
    n = nseq * T
    if is_prompt:
        nb, t = 1, min(COMBINE_TILE, T)
    else:
        t = T
        nb = max(1, min(nseq, COMBINE_TILE // T))
    tm = nb * t
    tiles_per_seq = T // t
    if is_prompt:
        x_map = lambda i, dr: (i // tiles_per_seq, i % tiles_per_seq, 0)
        mod_map = lambda i, dr: (i // tiles_per_seq, 0, 0)
    else:
        x_map = lambda i, dr: (i, 0, 0)
        mod_map = lambda i, dr: (i, 0, 0)
    return pl.pallas_call(
        functools.partial(_combine_kernel, base=base),
        out_shape=jax.ShapeDtypeStruct((nseq, T, d), F32),
        grid_spec=pltpu.PrefetchScalarGridSpec(
            num_scalar_prefetch=1,
            grid=(n // tm,),
            in_specs=[pl.BlockSpec(memory_space=pl.ANY),
                      pl.BlockSpec((nb, t, d), x_map),
                      pl.BlockSpec((nb, 6, d), mod_map),
                      pl.BlockSpec((tm, TOP_K), lambda i, dr: (i, 0)),
                      pl.BlockSpec((1, d), lambda i, dr: (0, 0))],
            out_specs=pl.BlockSpec((nb, t, d), x_map),
            scratch_shapes=[pltpu.VMEM((2, TOP_K, tm, d), F32),
                            pltpu.SemaphoreType.DMA((2,))]),
        compiler_params=_cparams(("arbitrary",)),
        name="combine_prompt" if is_prompt else "combine_sample",
    )(dest, yb, x1, mod, gates, g_post_ffn.reshape(1, d))


def _layer(xp, xs, cache_ckv, cache_kpe, layer, page_table, st_c, st_n, st_m, c_p, c_s, w):
    nbp, S, d = xp.shape
    nbs, T, _ = xs.shape
    past_len = page_table.shape[1] * PAGE_SIZE
    dt = xp.dtype

    mod = _modulation(jnp.concatenate([c_p, c_s], axis=0), w["w_ada"], w["b_ada"])
    mod = mod.reshape(nbp + nbs, 6, d)
    mod_p, mod_s = mod[:nbp], mod[nbp:]

    pw = _prep_proj_weights(w["w_in"], w["w_q_up"], w["w_uk"], w["w_uv"], w["b_igate"], w["b_fgate"])
    chunk_p = ML_CHUNK if S % ML_CHUNK == 0 else S
    chunk_s = ML_CHUNK if T % ML_CHUNK == 0 else T

    (q, k, v, ckv_p, kpe_p, mq, mk, mv, g, cum, og_p) = _project(
        xp, mod_p, _rope_tables(jnp.arange(S)), chunk_p, w["g_pre_mix"], w["g_qlat"], w["g_kvlat"], pw, True)
    o_mla_p = _prompt_attention(q, k, v, nbp, S)
    h_ml_p, c_new_p, n_new_p, m_new_p = _mlstm(
        mq, mk, mv, g, cum, jnp.zeros((nbp, ML_HEADS, ML_V, ML_QK), F32), jnp.zeros((nbp, ML_HEADS, ML_QK), F32),
        jnp.zeros((nbp, ML_HEADS), F32), nbp, S)

    (qa, qr, ckv_s, kpe_s, mq, mk, mv, g, cum, og_s) = _project(
        xs, mod_s, _rope_tables(past_len + jnp.arange(T)), chunk_s, w["g_pre_mix"], w["g_qlat"], w["g_kvlat"],
        pw, False)
    o_mla_s = _sample_attention(qa, qr, ckv_s, kpe_s, cache_ckv, cache_kpe, layer, page_table,
                                w["w_uv"].transpose(1, 0, 2))
    h_ml_s, c_new_s, n_new_s, m_new_s = _mlstm(mq, mk, mv, g, cum, st_c, st_n, st_m, nbs, T)

    w_out_b = w["w_out"].astype(BF16)
    wr_t = w["w_router"].T
    wr_hi = wr_t.astype(BF16)
    wr_lo = (wr_t - wr_hi.astype(F32)).astype(BF16)
    margs = (w["g_heads"], w_out_b, w["g_post_mix"], w["g_pre_ffn"], wr_hi, wr_lo, w["b_router"])
    x1_p, h2_p, idx_p, gate_p = _merge(xp, mod_p, o_mla_p, h_ml_p, og_p, *margs, True)
    x1_s, h2_s, idx_s, gate_s = _merge(xs, mod_s, o_mla_s, h_ml_s, og_s, *margs, False)

    n_p, n_s = nbp * S, nbs * T
    h2 = jnp.concatenate([h2_p, h2_s], axis=0)
    top_idx = jnp.concatenate([idx_p[:TOP_K], idx_s[:TOP_K]], axis=1).T
    gates = jnp.concatenate([gate_p[:TOP_K], gate_s[:TOP_K]], axis=1).T
    m_rows = (n_p + n_s) * TOP_K
    n_blocks = -(-m_rows // MOE_ROWS) + N_EXPERTS
    dest, row_tok, block_e, n_used = _route(top_idx, n_blocks)
    yb = _moe_experts(h2, block_e, row_tok, n_used, w["w_gate_up"], w["b_gate_up"], w["w_down"], w["b_down"])
    y_p = _combine(yb, dest, 0, x1_p, mod_p, gates[:n_p], w["g_post_ffn"], True)
    y_s = _combine(yb, dest, n_p * TOP_K, x1_s, mod_s, gates[n_p:], w["g_post_ffn"], False)

    new_p = (ckv_p.reshape(nbp, S, KV_LORA), kpe_p.reshape(nbp, S, MLA_ROPE), c_new_p.astype(dt),
             n_new_p.astype(dt), m_new_p.astype(dt))
    new_s = (ckv_s.reshape(nbs, T, KV_LORA), kpe_s.reshape(nbs, T, MLA_ROPE), c_new_s.astype(dt),
             n_new_s.astype(dt), m_new_s.astype(dt))
    return y_p, y_s, new_p, new_s


def kernel(x_prompt, x_sample, cache_ckv, cache_kpe, page_table, state_C, state_n, state_m, c_prompt, c_sample,
           w_ada, b_ada, g_pre_mix, g_post_mix, g_pre_ffn, g_post_ffn, w_in, g_qlat, w_q_up, g_kvlat, w_uk, w_uv,
           b_igate, b_fgate, g_heads, w_out, w_router, b_router, w_gate_up, b_gate_up, w_down, b_down):
    weights = dict(w_ada=w_ada, b_ada=b_ada, g_pre_mix=g_pre_mix, g_post_mix=g_post_mix, g_pre_ffn=g_pre_ffn,
                   g_post_ffn=g_post_ffn, w_in=w_in, g_qlat=g_qlat, w_q_up=w_q_up, g_kvlat=g_kvlat, w_uk=w_uk,
                   w_uv=w_uv, b_igate=b_igate, b_fgate=b_fgate, g_heads=g_heads, w_out=w_out, w_router=w_router,
                   b_router=b_router, w_gate_up=w_gate_up, b_gate_up=b_gate_up, w_down=w_down, b_down=b_down)
    depth = w_ada.shape[0]
    xp, xs = x_prompt, x_sample
    news_p, news_s = [], []
    for l in range(depth):
        wl = {name: val[l] for name, val in weights.items()}
        xp, xs, new_p, new_s = _layer(xp, xs, cache_ckv, cache_kpe, l, page_table, state_C[l], state_n[l],
                                      state_m[l], c_prompt, c_sample, wl)
        news_p.append(new_p)
        news_s.append(new_s)
    stack = lambda items, j: jnp.stack([it[j] for it in items])
    return ((xp, xs) + tuple(stack(news_p, j) for j in range(5)) + tuple(stack(news_s, j) for j in range(5)))
```

```python
import functools
import math

import jax
import jax.numpy as jnp
from jax import lax
from jax.experimental import pallas as pl
from jax.experimental.pallas import tpu as pltpu

F32 = jnp.float32
BF16 = jnp.bfloat16

HEAD_OUT = 128
MLA_HEADS = 4
ML_HEADS = 4
MLA_NOPE = 128
MLA_ROPE = 64
Q_LORA = 384
KV_LORA = 256
ROPE_BASE = 10000.0
SM_SCALE = (MLA_NOPE + MLA_ROPE) ** -0.5
ML_QK = 64
ML_V = 128
ML_CHUNK = 64
IGATE_CAP = 15.0
N_EXPERTS = 32
TOP_K = 4
SWIGLU_LIMIT = 7.0
SWIGLU_ALPHA = 1.702
EPS = 1e-6
PAGE_SIZE = 128

LANES = 128
LOG2E = 1.4426950408889634
QK_SCALE = SM_SCALE * LOG2E
NEG_INF = float("-inf")

TOKEN_TILE = 512
ATTN_BLOCK = 512
PAGES_PER_STEP = 64
ATTN_SUB = 1024
MOE_ROWS = 256
COMBINE_TILE = 128
VMEM_LIMIT = 56 * 1024 * 1024

NT_DIMS = (((1,), (1,)), ((), ()))
TN_DIMS = (((0,), (0,)), ((), ()))


def _cparams(sem):
    return pltpu.CompilerParams(dimension_semantics=sem, vmem_limit_bytes=VMEM_LIMIT)


def _rms(x, g):
    return x * lax.rsqrt(jnp.mean(x * x, axis=-1, keepdims=True) + EPS) * g


def _dot(a, b):
    return jnp.dot(a, b, preferred_element_type=F32)


def _mod_kernel(c_ref, w_ref, b_ref, o_ref):
    c = c_ref[...]
    s = c * jax.nn.sigmoid(c)
    o_ref[...] = _dot(s.astype(BF16), w_ref[...].astype(BF16)) + b_ref[...]


def _modulation(c, w_ada, b_ada):
    nseq, d = c.shape
    n_out = w_ada.shape[1]
    return pl.pallas_call(
        _mod_kernel,
        out_shape=jax.ShapeDtypeStruct((nseq, n_out), F32),
        grid=(n_out // d,),
        in_specs=[pl.BlockSpec((nseq, d), lambda j: (0, 0)),
                  pl.BlockSpec((d, d), lambda j: (0, j)),
                  pl.BlockSpec((1, d), lambda j: (0, j))],
        out_specs=pl.BlockSpec((nseq, d), lambda j: (0, j)),
        compiler_params=_cparams(("arbitrary",)),
        name="modulation",
    )(c, w_ada, b_ada.reshape(1, n_out))


_C_QLAT = 0
_C_CKV = _C_QLAT + Q_LORA
_C_KPE = _C_CKV + KV_LORA
_C_KROT = _C_KPE + LANES
_C_MQ = _C_KROT + LANES
_C_MK = _C_MQ + ML_HEADS * ML_QK
_C_MV = _C_MK + ML_HEADS * ML_QK
_C_OG = _C_MV + ML_HEADS * ML_V
_C_GATE = _C_OG + ML_HEADS * ML_V
_C_END = _C_GATE + LANES


def _rot_half_cols(w):
    half = w.shape[-1] // 2
    return jnp.concatenate([-w[..., half:], w[..., :half]], axis=-1)


def _prep_proj_weights(w_in, w_q_up, w_uk, w_uv, b_igate, b_fgate):
    d = w_in.shape[0]
    o = 0
    parts = {}
    for name, size in (("qlat", Q_LORA), ("ckv", KV_LORA), ("kpe", MLA_ROPE), ("mq", ML_HEADS * ML_QK),
                       ("mk", ML_HEADS * ML_QK), ("mv", ML_HEADS * ML_V), ("ig", ML_HEADS), ("fg", ML_HEADS),
                       ("og", ML_HEADS * ML_V)):
        parts[name] = w_in[:, o:o + size]
        o += size
    z64 = jnp.zeros((d, LANES - MLA_ROPE), F32)
    zg = jnp.zeros((d, LANES - 2 * ML_HEADS), F32)
    w_in_r = jnp.concatenate(
        [parts["qlat"], parts["ckv"], parts["kpe"], z64, _rot_half_cols(parts["kpe"]), z64,
         parts["mq"], parts["mk"], parts["mv"], parts["og"], parts["ig"], parts["fg"], zg], axis=1).astype(BF16)
    b_gate = jnp.concatenate([b_igate, b_fgate, jnp.zeros((LANES - 2 * ML_HEADS,), F32)]).reshape(1, LANES)
    w_nope = w_q_up[:, :, :MLA_NOPE].reshape(Q_LORA, MLA_HEADS * MLA_NOPE)
    w_rope = w_q_up[:, :, MLA_NOPE:]
    zr = jnp.zeros((Q_LORA, MLA_HEADS, LANES - MLA_ROPE), F32)
    w_qr = jnp.concatenate([w_rope, zr], axis=-1).reshape(Q_LORA, MLA_HEADS * LANES)
    w_qt = jnp.concatenate([_rot_half_cols(w_rope), zr], axis=-1).reshape(Q_LORA, MLA_HEADS * LANES)
    w_uk2 = w_uk.reshape(KV_LORA, MLA_HEADS * MLA_NOPE).astype(BF16)
    w_uv2 = w_uv.reshape(KV_LORA, MLA_HEADS * HEAD_OUT).astype(BF16)
    w_ukT = jnp.transpose(w_uk, (1, 2, 0)).astype(BF16)
    return dict(w_in=w_in_r, b_gate=b_gate, w_qn=w_nope.astype(BF16), w_qr=w_qr.astype(BF16),
                w_qt=w_qt.astype(BF16), w_uk=w_uk2, w_uv=w_uv2, w_ukT=w_ukT)


def _rope_tables(pos):
    half = MLA_ROPE // 2
    inv_freq = ROPE_BASE ** (-jnp.arange(half, dtype=F32) / half)
    ang = pos.astype(F32)[:, None] * inv_freq[None, :]
    z = jnp.zeros((pos.shape[0], LANES - MLA_ROPE), F32)
    cos = jnp.concatenate([jnp.cos(ang), jnp.cos(ang), z], axis=1)
    sin = jnp.concatenate([jnp.sin(ang), jnp.sin(ang), z], axis=1)
    return cos, sin


def _chunk_tril(tm, chunk):
    r = jnp.arange(tm)
    same = (r[:, None] // chunk) == (r[None, :] // chunk)
    return (same & (r[None, :] <= r[:, None])).astype(BF16)


def _split3(x):
    hi = x.astype(BF16)
    r1 = x - hi.astype(F32)
    mid = r1.astype(BF16)
    lo = (r1 - mid.astype(F32)).astype(BF16)
    return hi, mid, lo


def _proj_kernel(*refs, is_prompt):
    (x_ref, mod_ref, gpre_ref, win_ref, bg_ref, gq_ref, wqn_ref, wqr_ref, wqt_ref, gkv_ref,
     cos_ref, sin_ref, tril_ref) = refs[:13]
    if is_prompt:
        wuk_ref, wuv_ref = refs[13:15]
        q_ref, k_ref, v_ref, ckv_ref, kpe_ref, mq_ref, mk_ref, mv_ref, g_ref, cum_ref, og_ref = refs[15:]
    else:
        wukT_ref = refs[13]
        qa_ref, qr_ref, ckv_ref, kpe_ref, mq_ref, mk_ref, mv_ref, g_ref, cum_ref, og_ref = refs[14:]

    x = x_ref[...]
    mod = mod_ref[...]
    nb, t, d = x.shape
    h = _rms(x, gpre_ref[...]) * (1.0 + mod[:, 1:2, :]) + mod[:, 0:1, :]
    hb = h.reshape(nb * t, d).astype(BF16)

    def zcols(a, b):
        return _dot(hb, win_ref[:, a:b])

    cos = cos_ref[...]
    sin = sin_ref[...]

    qn = _rms(zcols(_C_QLAT, _C_CKV), gq_ref[...]).astype(BF16)
    q_nope = _dot(qn, wqn_ref[...]) * QK_SCALE
    q_r = _dot(qn, wqr_ref[...])
    q_t = _dot(qn, wqt_ref[...])
    ckv = _rms(zcols(_C_CKV, _C_KPE), gkv_ref[...])
    ckv_ref[...] = ckv
    kpe = zcols(_C_KPE, _C_KROT) * cos + zcols(_C_KROT, _C_MQ) * sin
    kpe_ref[...] = kpe[:, :MLA_ROPE]
    for hd in range(MLA_HEADS):
        sl = slice(hd * LANES, (hd + 1) * LANES)
        roped = (q_r[:, sl] * cos + q_t[:, sl] * sin) * QK_SCALE
        if is_prompt:
            q_ref[:, 2 * hd * LANES:(2 * hd + 1) * LANES] = q_nope[:, sl].astype(BF16)
            q_ref[:, (2 * hd + 1) * LANES:(2 * hd + 2) * LANES] = roped.astype(BF16)
        else:
            qa_ref[:, hd * KV_LORA:(hd + 1) * KV_LORA] = _dot(q_nope[:, sl].astype(BF16), wukT_ref[hd])
            qr_ref[:, sl] = roped
    if is_prompt:
        cb = ckv.astype(BF16)
        k_nope = _dot(cb, wuk_ref[...])
        kpe_b = kpe.astype(BF16)
        for hd in range(MLA_HEADS):
            k_ref[:, 2 * hd * LANES:(2 * hd + 1) * LANES] = k_nope[:, hd * LANES:(hd + 1) * LANES].astype(BF16)
            k_ref[:, (2 * hd + 1) * LANES:(2 * hd + 2) * LANES] = kpe_b
        v_ref[...] = _dot(cb, wuv_ref[...]).astype(BF16)

    mq_ref[...] = zcols(_C_MQ, _C_MK) * (ML_QK ** -0.5)
    mk_ref[...] = zcols(_C_MK, _C_MV)
    mv_ref[...] = zcols(_C_MV, _C_OG)
    og_ref[...] = jax.nn.sigmoid(zcols(_C_OG, _C_GATE))
    zg = zcols(_C_GATE, _C_END) + bg_ref[...]
    ig = IGATE_CAP * jnp.tanh(zg / IGATE_CAP)
    lf = -(jnp.maximum(-zg, 0.0) + jnp.log1p(jnp.exp(-jnp.abs(zg))))
    lane = lax.broadcasted_iota(jnp.int32, zg.shape, 1)
    g = jnp.where(lane < ML_HEADS, ig, lf)
    g_ref[...] = g
    tril = tril_ref[...]
    hi, mid, lo = _split3(g)
    cum_ref[...] = _dot(tril, hi) + _dot(tril, mid) + _dot(tril, lo)


def _project(x, mod, pos_tab, chunk, g_pre, g_qlat, g_kvlat, pw, is_prompt):
    nseq, T, d = x.shape
    n = nseq * T
    if is_prompt:
        nb, t = 1, min(TOKEN_TILE, T)
    else:
        t = T
        nb = max(1, min(nseq, TOKEN_TILE // T))
    assert T % t == 0 and nseq % nb == 0
    tm = nb * t
    tiles_per_seq = T // t
    grid = (n // tm,)
    cos, sin = pos_tab
    if is_prompt:
        x_map = lambda i: (i // tiles_per_seq, i % tiles_per_seq, 0)
        mod_map = lambda i: (i // tiles_per_seq, 0, 0)
        tab_map = lambda i: (i % tiles_per_seq, 0)
    else:
        x_map = lambda i: (i, 0, 0)
        mod_map = lambda i: (i, 0, 0)
        tab_map = lambda i: (0, 0)
        cos = jnp.tile(cos, (nb, 1))
        sin = jnp.tile(sin, (nb, 1))
    const = lambda i: (0, 0)
    row = lambda i: (i, 0)
    tril = _chunk_tril(tm, chunk)
    in_specs = [
        pl.BlockSpec((nb, t, d), x_map),
        pl.BlockSpec((nb, 6, d), mod_map),
        pl.BlockSpec((1, d), const),
        pl.BlockSpec((d, _C_END), const),
        pl.BlockSpec((1, LANES), const),
        pl.BlockSpec((1, Q_LORA), const),
        pl.BlockSpec((Q_LORA, MLA_HEADS * MLA_NOPE), const),
        pl.BlockSpec((Q_LORA, MLA_HEADS * LANES), const),
        pl.BlockSpec((Q_LORA, MLA_HEADS * LANES), const),
        pl.BlockSpec((1, KV_LORA), const),
        pl.BlockSpec((tm, LANES), tab_map),
        pl.BlockSpec((tm, LANES), tab_map),
        pl.BlockSpec((tm, tm), const),
    ]
    args = [x, mod, g_pre.reshape(1, d), pw["w_in"], pw["b_gate"], g_qlat.reshape(1, Q_LORA), pw["w_qn"],
            pw["w_qr"], pw["w_qt"], g_kvlat.reshape(1, KV_LORA), cos, sin, tril]

    def out(cols, dtype=F32):
        return jax.ShapeDtypeStruct((n, cols), dtype), pl.BlockSpec((tm, cols), row)

    if is_prompt:
        in_specs += [pl.BlockSpec((KV_LORA, MLA_HEADS * MLA_NOPE), const),
                     pl.BlockSpec((KV_LORA, MLA_HEADS * HEAD_OUT), const)]
        args += [pw["w_uk"], pw["w_uv"]]
        outs = [out(MLA_HEADS * 2 * LANES, BF16), out(MLA_HEADS * 2 * LANES, BF16), out(MLA_HEADS * HEAD_OUT, BF16)]
    else:
        in_specs += [pl.BlockSpec((MLA_HEADS, MLA_NOPE, KV_LORA), lambda i: (0, 0, 0))]
        args += [pw["w_ukT"]]
        outs = [out(MLA_HEADS * KV_LORA), out(MLA_HEADS * LANES)]
    outs += [out(KV_LORA), out(MLA_ROPE), out(ML_HEADS * ML_QK), out(ML_HEADS * ML_QK), out(ML_HEADS * ML_V),
             out(LANES), out(LANES), out(ML_HEADS * ML_V)]
    return pl.pallas_call(
        functools.partial(_proj_kernel, is_prompt=is_prompt),
        out_shape=[o[0] for o in outs],
        grid=grid,
        in_specs=in_specs,
        out_specs=[o[1] for o in outs],
        compiler_params=_cparams(("arbitrary",)),
        name="proj_prompt" if is_prompt else "proj_sample",
    )(*args)


def _attn_kernel(qi_ref, ki_ref, q_ref, k_ref, v_ref, o_ref, m_ref, l_ref, acc_ref):
    p = pl.program_id(1)
    qi = qi_ref[p]
    ki = ki_ref[p]
    tq = q_ref.shape[0]
    tk = k_ref.shape[0]

    @pl.when(ki == 0)
    def _():
        m_ref[...] = jnp.full(m_ref.shape, NEG_INF, F32)
        l_ref[...] = jnp.zeros(l_ref.shape, F32)
        acc_ref[...] = jnp.zeros(acc_ref.shape, F32)

    def update(masked):
        for hd in range(MLA_HEADS):
            qh = q_ref[:, hd * 2 * LANES:(hd + 1) * 2 * LANES]
            kh = k_ref[:, hd * 2 * LANES:(hd + 1) * 2 * LANES]
            s = lax.dot_general(qh, kh, NT_DIMS, preferred_element_type=F32)
            if masked:
                row = lax.broadcasted_iota(jnp.int32, s.shape, 0)
                col = lax.broadcasted_iota(jnp.int32, s.shape, 1)
                s = jnp.where(col <= row, s, NEG_INF)
            m_prev = m_ref[hd]
            m_next = jnp.maximum(m_prev, jnp.max(s, axis=1, keepdims=True))
            alpha = jnp.exp2(m_prev - m_next)
            pr = jnp.exp2(s - jnp.concatenate([m_next] * (tk // LANES), axis=1))
            l_ref[hd] = alpha * l_ref[hd] + jnp.sum(pr, axis=1, keepdims=True)
            acc_ref[hd] = alpha * acc_ref[hd] + _dot(pr.astype(BF16), v_ref[:, hd * HEAD_OUT:(hd + 1) * HEAD_OUT])
            m_ref[hd] = m_next

    @pl.when(ki < qi)
    def _():
        update(False)

    @pl.when(ki == qi)
    def _():
        update(True)
        for hd in range(MLA_HEADS):
            o_ref[:, hd * HEAD_OUT:(hd + 1) * HEAD_OUT] = acc_ref[hd] / l_ref[hd]


def _prompt_attention(q, k, v, nseq, S):
    blk = min(ATTN_BLOCK, S)
    nq = S // blk
    pairs = [(i, j) for i in range(nq) for j in range(i + 1)]
    qi_tab = jnp.asarray([p[0] for p in pairs], jnp.int32)
    ki_tab = jnp.asarray([p[1] for p in pairs], jnp.int32)
    wq = MLA_HEADS * 2 * LANES
    wv = MLA_HEADS * HEAD_OUT
    return pl.pallas_call(
        _attn_kernel,
        out_shape=jax.ShapeDtypeStruct((nseq * S, wv), F32),
        grid_spec=pltpu.PrefetchScalarGridSpec(
            num_scalar_prefetch=2,
            grid=(nseq, len(pairs)),
            in_specs=[pl.BlockSpec((blk, wq), lambda b, p, qi, ki: (b * nq + qi[p], 0)),
                      pl.BlockSpec((blk, wq), lambda b, p, qi, ki: (b * nq + ki[p], 0)),
                      pl.BlockSpec((blk, wv), lambda b, p, qi, ki: (b * nq + ki[p], 0))],
            out_specs=pl.BlockSpec((blk, wv), lambda b, p, qi, ki: (b * nq + qi[p], 0)),
            scratch_shapes=[pltpu.VMEM((MLA_HEADS, blk, LANES), F32),
                            pltpu.VMEM((MLA_HEADS, blk, LANES), F32),
                            pltpu.VMEM((MLA_HEADS, blk, HEAD_OUT), F32)]),
        compiler_params=_cparams(("arbitrary", "arbitrary")),
        name="prompt_attention",
    )(qi_tab, ki_tab, q, k, v)


def _sattn_kernel(pt_ref, qa_ref, qr_ref, ckvn_ref, kpen_ref, wuv_ref, cckv_hbm, ckpe_hbm, o_ref,
                  ckv_buf, kpe_buf, sem, kcb, q_s, qr_s, kn_s, pn_s, m_ref, l_ref, acc_ref,
                  *, layer, npg, nchunks, sub):
    b = pl.program_id(0)
    c = pl.program_id(1)
    step = b * nchunks + c
    nsteps = pl.num_programs(0) * nchunks
    slot = step % 2
    T = qa_ref.shape[0]
    rows = MLA_HEADS * T
    npos = npg * PAGE_SIZE

    def page_copies(page_of, slt):
        cps = []
        for j in range(npg):
            pg = page_of(j)
            dst = pl.ds(j * PAGE_SIZE, PAGE_SIZE)
            cps.append(pltpu.make_async_copy(cckv_hbm.at[layer, pg], ckv_buf.at[slt, dst, :], sem.at[0, slt]))
            cps.append(pltpu.make_async_copy(ckpe_hbm.at[layer, pg], kpe_buf.at[slt, :, dst], sem.at[1, slt]))
        return cps

    def fetch(stp, slt):
        for cp in page_copies(lambda j: pt_ref[stp * npg + j], slt):
            cp.start()

    @pl.when(step == 0)
    def _():
        fetch(0, 0)

    @pl.when(step + 1 < nsteps)
    def _():
        fetch(step + 1, 1 - slot)

    @pl.when(c == 0)
    def _():
        m_ref[...] = jnp.full(m_ref.shape, NEG_INF, F32)
        l_ref[...] = jnp.zeros(l_ref.shape, F32)
        acc_ref[...] = jnp.zeros(acc_ref.shape, F32)
        qa = qa_ref[...]
        qr = qr_ref[...]
        q_s[...] = jnp.concatenate([qa[:, hd * KV_LORA:(hd + 1) * KV_LORA] for hd in range(MLA_HEADS)],
                                   axis=0).astype(BF16)
        qr_s[...] = jnp.concatenate([qr[:, hd * LANES:(hd + 1) * LANES] for hd in range(MLA_HEADS)],
                                    axis=0).astype(BF16)

    for cp in page_copies(lambda j: 0, slot):
        cp.wait()

    q = q_s[...]
    qr = qr_s[:, :MLA_ROPE]

    def softmax_update(s, pv):
        m_prev = m_ref[...]
        m_next = jnp.maximum(m_prev, jnp.max(s, axis=1, keepdims=True))
        alpha = jnp.exp2(m_prev - m_next)
        pr = jnp.exp2(s - jnp.concatenate([m_next] * (s.shape[1] // LANES), axis=1))
        l_ref[...] = alpha * l_ref[...] + jnp.sum(pr, axis=1, keepdims=True)
        acc_ref[...] = jnp.concatenate([alpha] * (KV_LORA // LANES), axis=1) * acc_ref[...] + pv(pr.astype(BF16))
        m_ref[...] = m_next

    s_parts = []
    for sb in range(npos // sub):
        sl = slice(sb * sub, (sb + 1) * sub)
        kc = ckv_buf[slot, sl, :].astype(BF16)
        kcb[sl, :] = kc
        kp = kpe_buf[slot, :, sl].astype(BF16)
        s_parts.append(lax.dot_general(q, kc, NT_DIMS, preferred_element_type=F32) + _dot(qr, kp))

    def pv_chunk(p):
        out = _dot(p[:, 0:sub], kcb[0:sub, :])
        for sb in range(1, npos // sub):
            out = out + _dot(p[:, sb * sub:(sb + 1) * sub], kcb[sb * sub:(sb + 1) * sub, :])
        return out

    softmax_update(jnp.concatenate(s_parts, axis=1), pv_chunk)

    @pl.when(c == nchunks - 1)
    def _():
        kn_s[...] = jnp.zeros(kn_s.shape, F32)
        pn_s[...] = jnp.zeros(pn_s.shape, F32)
        kn_s[0:T, :] = ckvn_ref[...]
        pn_s[0:T, :] = kpen_ref[...]
        row = lax.broadcasted_iota(jnp.int32, (rows, LANES), 0)
        col = lax.broadcasted_iota(jnp.int32, (rows, LANES), 1)
        kn = kn_s[...].astype(BF16)
        s_new = (lax.dot_general(q, kn, NT_DIMS, preferred_element_type=F32)
                 + lax.dot_general(qr, pn_s[...].astype(BF16), NT_DIMS, preferred_element_type=F32))
        softmax_update(jnp.where(col <= row % T, s_new, NEG_INF), lambda p: _dot(p, kn))
        o_lat = acc_ref[...] / jnp.concatenate([l_ref[...]] * (KV_LORA // LANES), axis=1)
        for hd in range(MLA_HEADS):
            o_ref[:, hd * HEAD_OUT:(hd + 1) * HEAD_OUT] = _dot(o_lat[hd * T:(hd + 1) * T, :], wuv_ref[hd])


def _sample_attention(qa, qr, ckv_new, kpe_new, cache_ckv, cache_kpe, layer, page_table, w_uv):
    nseq, n_pages = page_table.shape
    T = qa.shape[0] // nseq
    npg = min(PAGES_PER_STEP, n_pages)
    assert n_pages % npg == 0
    nchunks = n_pages // npg
    sub = min(ATTN_SUB, npg * PAGE_SIZE)
    rows = MLA_HEADS * T
    seq = lambda b, c, pt: (b, 0)
    kern = functools.partial(_sattn_kernel, layer=layer, npg=npg, nchunks=nchunks, sub=sub)
    return pl.pallas_call(
        kern,
        out_shape=jax.ShapeDtypeStruct((nseq * T, MLA_HEADS * HEAD_OUT), F32),
        grid_spec=pltpu.PrefetchScalarGridSpec(
            num_scalar_prefetch=1,
            grid=(nseq, nchunks),
            in_specs=[pl.BlockSpec((T, MLA_HEADS * KV_LORA), seq),
                      pl.BlockSpec((T, MLA_HEADS * LANES), seq),
                      pl.BlockSpec((T, KV_LORA), seq),
                      pl.BlockSpec((T, MLA_ROPE), seq),
                      pl.BlockSpec((MLA_HEADS, KV_LORA, HEAD_OUT), lambda b, c, pt: (0, 0, 0)),
                      pl.BlockSpec(memory_space=pl.ANY),
                      pl.BlockSpec(memory_space=pl.ANY)],
            out_specs=pl.BlockSpec((T, MLA_HEADS * HEAD_OUT), seq),
            scratch_shapes=[pltpu.VMEM((2, npg * PAGE_SIZE, KV_LORA), F32),
                            pltpu.VMEM((2, MLA_ROPE, npg * PAGE_SIZE), F32),
                            pltpu.SemaphoreType.DMA((2, 2)),
                            pltpu.VMEM((npg * PAGE_SIZE, KV_LORA), BF16),
                            pltpu.VMEM((rows, KV_LORA), BF16),
                            pltpu.VMEM((rows, LANES), BF16),
                            pltpu.VMEM((LANES, KV_LORA), F32),
                            pltpu.VMEM((LANES, MLA_ROPE), F32),
                            pltpu.VMEM((rows, LANES), F32),
                            pltpu.VMEM((rows, LANES), F32),
                            pltpu.VMEM((rows, KV_LORA), F32)]),
        compiler_params=_cparams(("arbitrary", "arbitrary")),
        name="sample_attention",
    )(page_table.reshape(-1), qa, qr, ckv_new, kpe_new, w_uv, cache_ckv, jnp.swapaxes(cache_kpe, 2, 3))


def _mlstm_kernel(mq_ref, mk_ref, mv_ref, g_ref, cum_ref, c0_ref, n0_ref, m0_ref,
                  h_ref, c_ref, n_ref, m_ref, *, unroll_seqs, dot_dtype):
    ci = pl.program_id(1)
    nbs, L = mq_ref.shape[0], mq_ref.shape[1]

    @pl.when(ci == 0)
    def _():
        c_ref[...] = c0_ref[...]
        n_ref[...] = n0_ref[...]
        m_ref[...] = m0_ref[...]

    trow = lax.broadcasted_iota(jnp.int32, (L, L), 0)
    tcol = lax.broadcasted_iota(jnp.int32, (L, L), 1)
    causal = tcol <= trow

    def one_seq(i):
        g = g_ref[i]
        cum = cum_ref[i]
        g_t = g.T
        cum_t = cum.T
        mq = mq_ref[i]
        mk = mk_ref[i]
        mv = mv_ref[i]
        for hd in range(ML_HEADS):
            q = mq[:, hd * ML_QK:(hd + 1) * ML_QK]
            k = mk[:, hd * ML_QK:(hd + 1) * ML_QK]
            v = mv[:, hd * ML_V:(hd + 1) * ML_V]
            ig_col = g[:, hd:hd + 1]
            ig_row = g_t[hd:hd + 1, :]
            b_col = cum[:, ML_HEADS + hd:ML_HEADS + hd + 1]
            b_row = cum_t[ML_HEADS + hd:ML_HEADS + hd + 1, :]
            m_prev = m_ref[i, hd:hd + 1, 0:1]
            c_prev = c_ref[i, hd]
            n_prev = n_ref[i, hd:hd + 1, :]
            log_d = jnp.where(causal, b_col - b_row + ig_row, NEG_INF)
            log_inter = b_col + m_prev
            m_t = jnp.maximum(log_inter, jnp.max(log_d, axis=1, keepdims=True))
            d = jnp.exp(log_d - m_t)
            inter = jnp.exp(log_inter - m_t)
            qd = q.astype(dot_dtype)
            kd = k.astype(dot_dtype)
            s = lax.dot_general(qd, kd, NT_DIMS, preferred_element_type=F32) * d
            num = (inter * lax.dot_general(qd, c_prev.astype(dot_dtype), NT_DIMS, preferred_element_type=F32)
                   + _dot(s.astype(dot_dtype), v.astype(dot_dtype)))
            den = inter * jnp.sum(q * n_prev, axis=1, keepdims=True) + jnp.sum(s, axis=1, keepdims=True)
            h_ref[i, :, hd * ML_V:(hd + 1) * ML_V] = num / jnp.maximum(jnp.abs(den), jnp.exp(-m_t))
            m_new = m_t[L - 1:L, :]
            b_last = b_col[L - 1:L, :]
            w_col = jnp.exp(b_last - b_col + ig_col - m_new)
            decay = jnp.exp(b_last + m_prev - m_new)
            c_ref[i, hd] = decay * c_prev + lax.dot_general((w_col * v).astype(dot_dtype), kd, TN_DIMS,
                                                            preferred_element_type=F32)
            n_ref[i, hd:hd + 1, :] = decay * n_prev + jnp.sum(w_col * k, axis=0, keepdims=True)
            m_ref[i, hd:hd + 1, :] = jnp.broadcast_to(m_new, (1, LANES))

    if unroll_seqs:
        for i in range(nbs):
            one_seq(i)
    else:
        def body(i, carry):
            one_seq(i)
            return carry
        lax.fori_loop(0, nbs, body, 0)


def _mlstm(mq, mk, mv, g, cum, c0, n0, m0, nseq, T):
    L = ML_CHUNK if T % ML_CHUNK == 0 else T
    nc = T // L
    nbs = nseq if nseq <= 2 else 16
    assert nseq % nbs == 0
    r3 = lambda a: a.reshape(nseq, T, a.shape[-1])
    tok = lambda gi, ci: (gi, ci, 0)
    st4 = lambda gi, ci: (gi, 0, 0, 0)
    st3 = lambda gi, ci: (gi, 0, 0)
    m0b = jnp.broadcast_to(m0.astype(F32)[:, :, None], (nseq, ML_HEADS, LANES))
    kern = functools.partial(_mlstm_kernel, unroll_seqs=nbs <= 2, dot_dtype=BF16 if L >= 16 else F32)
    h, c, n, m = pl.pallas_call(
        kern,
        out_shape=[jax.ShapeDtypeStruct((nseq, T, ML_HEADS * ML_V), F32),
                   jax.ShapeDtypeStruct((nseq, ML_HEADS, ML_V, ML_QK), F32),
                   jax.ShapeDtypeStruct((nseq, ML_HEADS, ML_QK), F32),
                   jax.ShapeDtypeStruct((nseq, ML_HEADS, LANES), F32)],
        grid=(nseq // nbs, nc),
        in_specs=[pl.BlockSpec((nbs, L, ML_HEADS * ML_QK), tok),
                  pl.BlockSpec((nbs, L, ML_HEADS * ML_QK), tok),
                  pl.BlockSpec((nbs, L, ML_HEADS * ML_V), tok),
                  pl.BlockSpec((nbs, L, LANES), tok),
                  pl.BlockSpec((nbs, L, LANES), tok),
                  pl.BlockSpec((nbs, ML_HEADS, ML_V, ML_QK), st4),
                  pl.BlockSpec((nbs, ML_HEADS, ML_QK), st3),
                  pl.BlockSpec((nbs, ML_HEADS, LANES), st3)],
        out_specs=[pl.BlockSpec((nbs, L, ML_HEADS * ML_V), tok),
                   pl.BlockSpec((nbs, ML_HEADS, ML_V, ML_QK), st4),
                   pl.BlockSpec((nbs, ML_HEADS, ML_QK), st3),
                   pl.BlockSpec((nbs, ML_HEADS, LANES), st3)],
        compiler_params=_cparams(("arbitrary", "arbitrary")),
        name="mlstm_prompt" if nseq <= 2 else "mlstm_sample",
    )(r3(mq), r3(mk), r3(mv), r3(g), r3(cum), c0.astype(F32), n0.astype(F32), m0b)
    return h.reshape(nseq * T, ML_HEADS * ML_V), c, n, m[:, :, 0]


def _merge_kernel(x_ref, mod_ref, omla_ref, hml_ref, og_ref, gh_ref, wout_ref, gpm_ref, gpf_ref,
                  wrh_ref, wrl_ref, br_ref, x1_ref, h2_ref, idx_ref, gate_ref):
    nb, t, d = x_ref.shape
    tm = nb * t
    gh = gh_ref[...]
    parts = []
    for hd in range(MLA_HEADS + ML_HEADS):
        if hd < MLA_HEADS:
            xh = omla_ref[:, hd * HEAD_OUT:(hd + 1) * HEAD_OUT]
        else:
            j = hd - MLA_HEADS
            xh = hml_ref[:, j * HEAD_OUT:(j + 1) * HEAD_OUT]
        yh = _rms(xh, gh[:, hd * HEAD_OUT:(hd + 1) * HEAD_OUT])
        if hd >= MLA_HEADS:
            yh = yh * og_ref[:, j * HEAD_OUT:(j + 1) * HEAD_OUT]
        parts.append(yh.astype(BF16))
    merged = jnp.concatenate(parts, axis=1)
    mix = _dot(merged, wout_ref[...])
    mod = mod_ref[...]
    x1 = x_ref[...] + mod[:, 2:3, :] * _rms(mix, gpm_ref[...]).reshape(nb, t, d)
    x1_ref[...] = x1
    h2 = (_rms(x1, gpf_ref[...]) * (1.0 + mod[:, 4:5, :]) + mod[:, 3:4, :]).reshape(tm, d)
    h2_ref[...] = h2
    hh = h2.astype(BF16)
    hl = (h2 - hh.astype(F32)).astype(BF16)
    wrh = wrh_ref[...]
    logits = (lax.dot_general(wrh, hh, NT_DIMS, preferred_element_type=F32)
              + lax.dot_general(wrh, hl, NT_DIMS, preferred_element_type=F32)
              + lax.dot_general(wrl_ref[...], hh, NT_DIMS, preferred_element_type=F32)) + br_ref[...]
    eidx = lax.broadcasted_iota(jnp.int32, logits.shape, 0)
    work = logits
    vals, idxs = [], []
    for _ in range(TOP_K):
        mx = jnp.max(work, axis=0, keepdims=True)
        sel = jnp.min(jnp.where(work == mx, eidx, N_EXPERTS), axis=0, keepdims=True)
        vals.append(mx)
        idxs.append(sel)
        work = jnp.where(eidx == sel, NEG_INF, work)
    ex = [jnp.exp(v - vals[0]) for v in vals]
    tot = ex[0] + ex[1] + ex[2] + ex[3]
    zi = jnp.zeros((8 - TOP_K, tm), jnp.int32)
    zf = jnp.zeros((8 - TOP_K, tm), F32)
    idx_ref[...] = jnp.concatenate(idxs + [zi], axis=0)
    gate_ref[...] = jnp.concatenate([e / tot for e in ex] + [zf], axis=0)


def _merge(x, mod, o_mla, h_ml, og, g_heads, w_out, g_post_mix, g_pre_ffn, wr_hi, wr_lo, b_router, is_prompt):
    nseq, T, d = x.shape
    n = nseq * T
    if is_prompt:
        nb, t = 1, min(TOKEN_TILE, T)
    else:
        t = T
        nb = max(1, min(nseq, TOKEN_TILE // T))
    tm = nb * t
    tiles_per_seq = T // t
    if is_prompt:
        x_map = lambda i: (i // tiles_per_seq, i % tiles_per_seq, 0)
        mod_map = lambda i: (i // tiles_per_seq, 0, 0)
    else:
        x_map = lambda i: (i, 0, 0)
        mod_map = lambda i: (i, 0, 0)
    const = lambda i: (0, 0)
    row = lambda i: (i, 0)
    col = lambda i: (0, i)
    wh = MLA_HEADS * HEAD_OUT
    return pl.pallas_call(
        _merge_kernel,
        out_shape=[jax.ShapeDtypeStruct((nseq, T, d), F32), jax.ShapeDtypeStruct((n, d), F32),
                   jax.ShapeDtypeStruct((8, n), jnp.int32), jax.ShapeDtypeStruct((8, n), F32)],
        grid=(n // tm,),
        in_specs=[pl.BlockSpec((nb, t, d), x_map),
                  pl.BlockSpec((nb, 6, d), mod_map),
                  pl.BlockSpec((tm, wh), row),
                  pl.BlockSpec((tm, wh), row),
                  pl.BlockSpec((tm, wh), row),
                  pl.BlockSpec((1, 2 * wh), const),
                  pl.BlockSpec((2 * wh, d), const),
                  pl.BlockSpec((1, d), const),
                  pl.BlockSpec((1, d), const),
                  pl.BlockSpec((N_EXPERTS, d), const),
                  pl.BlockSpec((N_EXPERTS, d), const),
                  pl.BlockSpec((N_EXPERTS, 1), const)],
        out_specs=[pl.BlockSpec((nb, t, d), x_map), pl.BlockSpec((tm, d), row),
                   pl.BlockSpec((8, tm), col), pl.BlockSpec((8, tm), col)],
        compiler_params=_cparams(("arbitrary",)),
        name="merge_prompt" if is_prompt else "merge_sample",
    )(x, mod, o_mla, h_ml, og, g_heads.reshape(1, 2 * wh), w_out, g_post_mix.reshape(1, d),
      g_pre_ffn.reshape(1, d), wr_hi, wr_lo, b_router.reshape(N_EXPERTS, 1))


def _moe_kernel(be_ref, tok_ref, sh_ref, nu_ref, h2_hbm, wgu_ref, bgu_ref, wd_ref, bd_ref, y_ref,
                xbuf, sem, wgu_s, wd_s):
    blk = pl.program_id(0)
    nblk = pl.num_programs(0)
    slot = blk % 2
    tb, d = y_ref.shape
    de = wd_ref.shape[0]
    n_used = nu_ref[0]

    def row_copy(tok, r, slt):
        return pltpu.make_async_copy(h2_hbm.at[pl.ds(tok, 1), :], xbuf.at[slt, pl.ds(r, 1), :], sem.at[slt])

    def rows_base(bk):
        return bk * tb + sh_ref[be_ref[jnp.minimum(bk, nblk - 1)]]

    def wait_rows(slt):
        pltpu.make_async_copy(h2_hbm.at[pl.ds(0, tb), :], xbuf.at[slt], sem.at[slt]).wait()

    @pl.when(blk == 0)
    def _():
        base = rows_base(0)

        def body(r, carry):
            row_copy(tok_ref[base + r], r, 0).start()
            return carry
        lax.fori_loop(0, tb, body, 0, unroll=8)

    hot = blk < n_used
    e = be_ref[blk]
    e_prev = be_ref[jnp.maximum(blk - 1, 0)]

    @pl.when(jnp.logical_and(hot, jnp.logical_or(blk == 0, e != e_prev)))
    def _():
        wgu_s[...] = wgu_ref[...].astype(BF16)
        wd_s[...] = wd_ref[...].astype(BF16)

    @pl.when(hot)
    def _():
        wait_rows(slot)
        xb = xbuf[slot].astype(BF16)
        base = rows_base(blk + 1)
        for r in range(tb):
            row_copy(tok_ref[base + r], r, 1 - slot).start()
        gu = _dot(xb, wgu_s[...]) + bgu_ref[...]
        gt = jnp.minimum(gu[:, :de], SWIGLU_LIMIT)
        up = jnp.clip(gu[:, de:], -SWIGLU_LIMIT, SWIGLU_LIMIT)
        act = gt * jax.nn.sigmoid(SWIGLU_ALPHA * gt) * (up + 1.0)
        y_ref[...] = _dot(act.astype(BF16), wd_s[...]) + bd_ref[...]

    @pl.when(blk == n_used)
    def _():
        wait_rows(slot)

    @pl.when(jnp.logical_and(hot, blk == nblk - 1))
    def _():
        wait_rows(1 - slot)

    @pl.when(jnp.logical_not(hot))
    def _():
        y_ref[...] = jnp.zeros(y_ref.shape, F32)


def _moe_experts(h2, block_e, tok_sorted, shift, n_used, w_gate_up, b_gate_up, w_down, b_down):
    n, d = h2.shape
    ne, _, de2 = w_gate_up.shape
    de = w_down.shape[1]
    n_blocks = block_e.shape[0]
    tb = MOE_ROWS
    wmap = lambda i, be, tk, sh, nu: (be[i], 0, 0)
    return pl.pallas_call(
        _moe_kernel,
        out_shape=jax.ShapeDtypeStruct((n_blocks * tb, d), F32),
        grid_spec=pltpu.PrefetchScalarGridSpec(
            num_scalar_prefetch=4,
            grid=(n_blocks,),
            in_specs=[pl.BlockSpec(memory_space=pl.ANY),
                      pl.BlockSpec((None, d, de2), wmap),
                      pl.BlockSpec((None, 1, de2), wmap),
                      pl.BlockSpec((None, de, d), wmap),
                      pl.BlockSpec((None, 1, d), wmap)],
            out_specs=pl.BlockSpec((tb, d), lambda i, be, tk, sh, nu: (i, 0)),
            scratch_shapes=[pltpu.VMEM((2, tb, d), F32),
                            pltpu.SemaphoreType.DMA((2,)),
                            pltpu.VMEM((d, de2), BF16),
                            pltpu.VMEM((de, d), BF16)]),
        compiler_params=_cparams(("arbitrary",)),
        name="moe_experts",
    )(block_e, tok_sorted, shift, n_used, h2, w_gate_up, b_gate_up.reshape(ne, 1, de2), w_down,
      b_down.reshape(ne, 1, d))


def _route(top_idx, n_blocks):
    n = top_idx.shape[0]
    m = n * TOP_K
    tb = MOE_ROWS
    flat_e = top_idx.reshape(m)
    pair = jnp.arange(m, dtype=jnp.int32)
    order = lax.sort(flat_e * m + pair) % m
    tok_sorted = jnp.concatenate([order // TOP_K, jnp.zeros((2 * tb,), jnp.int32)])
    onehot = (flat_e[:, None] == jnp.arange(N_EXPERTS, dtype=jnp.int32)[None, :]).astype(jnp.int32)
    csum = jnp.cumsum(onehot, axis=0)
    counts = csum[-1]
    padded = (counts + tb - 1) // tb * tb
    pends = jnp.cumsum(padded)
    pstarts = pends - padded
    starts = jnp.cumsum(counts) - counts
    dest = jnp.sum(onehot * (csum - 1 + pstarts[None, :]), axis=1).astype(jnp.int32)
    blk_row = jnp.arange(n_blocks, dtype=jnp.int32) * tb
    block_e = jnp.minimum(jnp.sum((pends[None, :] <= blk_row[:, None]).astype(jnp.int32), axis=1), N_EXPERTS - 1)
    n_used = (pends[-1] // tb).astype(jnp.int32).reshape(1)
    return dest, tok_sorted, (starts - pstarts).astype(jnp.int32), block_e.astype(jnp.int32), n_used


def _combine_kernel(dest_ref, yb_hbm, x1_ref, mod_ref, gate_ref, gpost_ref, o_ref, ybuf, sem, *, base):
    i = pl.program_id(0)
    nsteps = pl.num_programs(0)
    slot = i % 2
    nb, t, d = x1_ref.shape
    tm = nb * t

    def gather(step, slt):
        def body(r, carry):
            for kk in range(TOP_K):
                row = dest_ref[base + (step * tm + r) * TOP_K + kk]
                pltpu.make_async_copy(yb_hbm.at[pl.ds(row, 1), :], ybuf.at[slt, kk, pl.ds(r, 1), :],
                                      sem.at[slt]).start()
            return carry
        lax.fori_loop(0, tm, body, 0, unroll=4)

    @pl.when(i == 0)
    def _():
        gather(0, 0)

    @pl.when(i + 1 < nsteps)
    def _():
        gather(i + 1, 1 - slot)

    def wait_body(r, carry):
        for kk in range(TOP_K):
            pltpu.make_async_copy(yb_hbm.at[pl.ds(0, 1), :], ybuf.at[slot, kk, pl.ds(r, 1), :], sem.at[slot]).wait()
        return carry
    lax.fori_loop(0, tm, wait_body, 0, unroll=4)

    gate = gate_ref[...]
    y = gate[:, 0:1] * ybuf[slot, 0]
    for kk in range(1, TOP_K):
        y = y + gate[:, kk:kk + 1] * ybuf[slot, kk]
    mod = mod_ref[...]
    o_ref[...] = x1_ref[...] + mod[:, 5:6, :] * _rms(y, gpost_ref[...]).reshape(nb, t, d)


def _combine(yb, dest, base, x1, mod, gates, g_post_ffn, is_prompt):
    nseq, T, d = x1.shape
    n = nseq * T
    if is_prompt:
        nb, t = 1, min(COMBINE_TILE, T)
    else:
        t = T
        nb = max(1, min(nseq, COMBINE_TILE // T))
    tm = nb * t
    tiles_per_seq = T // t
    if is_prompt:
        x_map = lambda i, dr: (i // tiles_per_seq, i % tiles_per_seq, 0)
        mod_map = lambda i, dr: (i // tiles_per_seq, 0, 0)
    else:
        x_map = lambda i, dr: (i, 0, 0)
        mod_map = lambda i, dr: (i, 0, 0)
    return pl.pallas_call(
        functools.partial(_combine_kernel, base=base),
        out_shape=jax.ShapeDtypeStruct((nseq, T, d), F32),
        grid_spec=pltpu.PrefetchScalarGridSpec(
            num_scalar_prefetch=1,
            grid=(n // tm,),
            in_specs=[pl.BlockSpec(memory_space=pl.ANY),
                      pl.BlockSpec((nb, t, d), x_map),
                      pl.BlockSpec((nb, 6, d), mod_map),
                      pl.BlockSpec((tm, TOP_K), lambda i, dr: (i, 0)),
                      pl.BlockSpec((1, d), lambda i, dr: (0, 0))],
            out_specs=pl.BlockSpec((nb, t, d), x_map),
            scratch_shapes=[pltpu.VMEM((2, TOP_K, tm, d), F32),
                            pltpu.SemaphoreType.DMA((2,))]),
        compiler_params=_cparams(("arbitrary",)),
        name="combine_prompt" if is_prompt else "combine_sample",
    )(dest, yb, x1, mod, gates, g_post_ffn.reshape(1, d))


def _layer(xp, xs, cache_ckv, cache_kpe, layer, page_table, st_c, st_n, st_m, c_p, c_s, w):
    nbp, S, d = xp.shape
    nbs, T, _ = xs.shape
    past_len = page_table.shape[1] * PAGE_SIZE
    dt = xp.dtype

    mod = _modulation(jnp.concatenate([c_p, c_s], axis=0), w["w_ada"], w["b_ada"])
    mod = mod.reshape(nbp + nbs, 6, d)
    mod_p, mod_s = mod[:nbp], mod[nbp:]

    pw = _prep_proj_weights(w["w_in"], w["w_q_up"], w["w_uk"], w["w_uv"], w["b_igate"], w["b_fgate"])
    chunk_p = ML_CHUNK if S % ML_CHUNK == 0 else S
    chunk_s = ML_CHUNK if T % ML_CHUNK == 0 else T

    (q, k, v, ckv_p, kpe_p, mq, mk, mv, g, cum, og_p) = _project(
        xp, mod_p, _rope_tables(jnp.arange(S)), chunk_p, w["g_pre_mix"], w["g_qlat"], w["g_kvlat"], pw, True)
    o_mla_p = _prompt_attention(q, k, v, nbp, S)
    h_ml_p, c_new_p, n_new_p, m_new_p = _mlstm(
        mq, mk, mv, g, cum, jnp.zeros((nbp, ML_HEADS, ML_V, ML_QK), F32), jnp.zeros((nbp, ML_HEADS, ML_QK), F32),
        jnp.zeros((nbp, ML_HEADS), F32), nbp, S)

    (qa, qr, ckv_s, kpe_s, mq, mk, mv, g, cum, og_s) = _project(
        xs, mod_s, _rope_tables(past_len + jnp.arange(T)), chunk_s, w["g_pre_mix"], w["g_qlat"], w["g_kvlat"],
        pw, False)
    o_mla_s = _sample_attention(qa, qr, ckv_s, kpe_s, cache_ckv, cache_kpe, layer, page_table,
                                w["w_uv"].transpose(1, 0, 2))
    h_ml_s, c_new_s, n_new_s, m_new_s = _mlstm(mq, mk, mv, g, cum, st_c, st_n, st_m, nbs, T)

    w_out_b = w["w_out"].astype(BF16)
    wr_t = w["w_router"].T
    wr_hi = wr_t.astype(BF16)
    wr_lo = (wr_t - wr_hi.astype(F32)).astype(BF16)
    margs = (w["g_heads"], w_out_b, w["g_post_mix"], w["g_pre_ffn"], wr_hi, wr_lo, w["b_router"])
    x1_p, h2_p, idx_p, gate_p = _merge(xp, mod_p, o_mla_p, h_ml_p, og_p, *margs, True)
    x1_s, h2_s, idx_s, gate_s = _merge(xs, mod_s, o_mla_s, h_ml_s, og_s, *margs, False)

    n_p, n_s = nbp * S, nbs * T
    h2 = jnp.concatenate([h2_p, h2_s], axis=0)
    top_idx = jnp.concatenate([idx_p[:TOP_K], idx_s[:TOP_K]], axis=1).T
    gates = jnp.concatenate([gate_p[:TOP_K], gate_s[:TOP_K]], axis=1).T
    m_rows = (n_p + n_s) * TOP_K
    n_blocks = -(-m_rows // MOE_ROWS) + N_EXPERTS
    dest, tok_sorted, shift, block_e, n_used = _route(top_idx, n_blocks)
    yb = _moe_experts(h2, block_e, tok_sorted, shift, n_used, w["w_gate_up"], w["b_gate_up"], w["w_down"],
                      w["b_down"])
    y_p = _combine(yb, dest, 0, x1_p, mod_p, gates[:n_p], w["g_post_ffn"], True)
    y_s = _combine(yb, dest, n_p * TOP_K, x1_s, mod_s, gates[n_p:], w["g_post_ffn"], False)

    new_p = (ckv_p.reshape(nbp, S, KV_LORA), kpe_p.reshape(nbp, S, MLA_ROPE), c_new_p.astype(dt),
             n_new_p.astype(dt), m_new_p.astype(dt))
    new_s = (ckv_s.reshape(nbs, T, KV_LORA), kpe_s.reshape(nbs, T, MLA_ROPE), c_new_s.astype(dt),
             n_new_s.astype(dt), m_new_s.astype(dt))
    return y_p, y_s, new_p, new_s


def kernel(x_prompt, x_sample, cache_ckv, cache_kpe, page_table, state_C, state_n, state_m, c_prompt, c_sample,
           w_ada, b_ada, g_pre_mix, g_post_mix, g_pre_ffn, g_post_ffn, w_in, g_qlat, w_q_up, g_kvlat, w_uk, w_uv,
           b_igate, b_fgate, g_heads, w_out, w_router, b_router, w_gate_up, b_gate_up, w_down, b_down):
    weights = dict(w_ada=w_ada, b_ada=b_ada, g_pre_mix=g_pre_mix, g_post_mix=g_post_mix, g_pre_ffn=g_pre_ffn,
                   g_post_ffn=g_post_ffn, w_in=w_in, g_qlat=g_qlat, w_q_up=w_q_up, g_kvlat=g_kvlat, w_uk=w_uk,
                   w_uv=w_uv, b_igate=b_igate, b_fgate=b_fgate, g_heads=g_heads, w_out=w_out, w_router=w_router,
                   b_router=b_router, w_gate_up=w_gate_up, b_gate_up=b_gate_up, w_down=w_down, b_down=b_down)
    depth = w_ada.shape[0]
    xp, xs = x_prompt, x_sample
    news_p, news_s = [], []
    for l in range(depth):
        wl = {name: val[l] for name, val in weights.items()}
        xp, xs, new_p, new_s = _layer(xp, xs, cache_ckv, cache_kpe, l, page_table, state_C[l], state_n[l],
                                      state_m[l], c_prompt, c_sample, wl)
        news_p.append(new_p)
        news_s.append(new_s)
    stack = lambda items, j: jnp.stack([it[j] for it in items])
    return ((xp, xs) + tuple(stack(news_p, j) for j in range(5)) + tuple(stack(news_s, j) for j in range(5)))
```

```python
import functools
import math

import jax
import jax.numpy as jnp
from jax import lax
from jax.experimental import pallas as pl
from jax.experimental.pallas import tpu as pltpu

F32 = jnp.float32
BF16 = jnp.bfloat16

HEAD_OUT = 128
MLA_HEADS = 4
ML_HEADS = 4
MLA_NOPE = 128
MLA_ROPE = 64
Q_LORA = 384
KV_LORA = 256
ROPE_BASE = 10000.0
SM_SCALE = (MLA_NOPE + MLA_ROPE) ** -0.5
ML_QK = 64
ML_V = 128
ML_CHUNK = 64
IGATE_CAP = 15.0
N_EXPERTS = 32
TOP_K = 4
SWIGLU_LIMIT = 7.0
SWIGLU_ALPHA = 1.702
EPS = 1e-6
PAGE_SIZE = 128

LANES = 128
LOG2E = 1.4426950408889634
QK_SCALE = SM_SCALE * LOG2E
NEG_INF = float("-inf")

TOKEN_TILE = 512
ATTN_BLOCK = 512
PAGES_PER_STEP = 64
ATTN_SUB = 1024
MOE_ROWS = 256
COMBINE_TILE = 128
VMEM_LIMIT = 56 * 1024 * 1024

NT_DIMS = (((1,), (1,)), ((), ()))
TN_DIMS = (((0,), (0,)), ((), ()))


def _cparams(sem):
    return pltpu.CompilerParams(dimension_semantics=sem, vmem_limit_bytes=VMEM_LIMIT)


def _rms(x, g):
    return x * lax.rsqrt(jnp.mean(x * x, axis=-1, keepdims=True) + EPS) * g


def _dot(a, b):
    return jnp.dot(a, b, preferred_element_type=F32)


ROW_TILE = 8


def _store_row_tiles(ref, val):
    rows = val.shape[0]
    for c in range(ROW_TILE):
        ref[pl.ds(c, rows, stride=ROW_TILE), :] = val[:, c * LANES:(c + 1) * LANES]


def _load_row_tiles(ref, rows):
    return jnp.concatenate([ref[pl.ds(c, rows, stride=ROW_TILE), :] for c in range(ROW_TILE)], axis=1)


def _mod_kernel(c_ref, w_ref, b_ref, o_ref):
    c = c_ref[...]
    s = c * jax.nn.sigmoid(c)
    o_ref[...] = _dot(s.astype(BF16), w_ref[...].astype(BF16)) + b_ref[...]


def _modulation(c, w_ada, b_ada):
    nseq, d = c.shape
    n_out = w_ada.shape[1]
    return pl.pallas_call(
        _mod_kernel,
        out_shape=jax.ShapeDtypeStruct((nseq, n_out), F32),
        grid=(n_out // d,),
        in_specs=[pl.BlockSpec((nseq, d), lambda j: (0, 0)),
                  pl.BlockSpec((d, d), lambda j: (0, j)),
                  pl.BlockSpec((1, d), lambda j: (0, j))],
        out_specs=pl.BlockSpec((nseq, d), lambda j: (0, j)),
        compiler_params=_cparams(("arbitrary",)),
        name="modulation",
    )(c, w_ada, b_ada.reshape(1, n_out))


_C_QLAT = 0
_C_CKV = _C_QLAT + Q_LORA
_C_KPE = _C_CKV + KV_LORA
_C_KROT = _C_KPE + LANES
_C_MQ = _C_KROT + LANES
_C_MK = _C_MQ + ML_HEADS * ML_QK
_C_MV = _C_MK + ML_HEADS * ML_QK
_C_OG = _C_MV + ML_HEADS * ML_V
_C_GATE = _C_OG + ML_HEADS * ML_V
_C_END = _C_GATE + LANES


def _rot_half_cols(w):
    half = w.shape[-1] // 2
    return jnp.concatenate([-w[..., half:], w[..., :half]], axis=-1)


def _prep_proj_weights(w_in, w_q_up, w_uk, w_uv, b_igate, b_fgate):
    d = w_in.shape[0]
    o = 0
    parts = {}
    for name, size in (("qlat", Q_LORA), ("ckv", KV_LORA), ("kpe", MLA_ROPE), ("mq", ML_HEADS * ML_QK),
                       ("mk", ML_HEADS * ML_QK), ("mv", ML_HEADS * ML_V), ("ig", ML_HEADS), ("fg", ML_HEADS),
                       ("og", ML_HEADS * ML_V)):
        parts[name] = w_in[:, o:o + size]
        o += size
    z64 = jnp.zeros((d, LANES - MLA_ROPE), F32)
    zg = jnp.zeros((d, LANES - 2 * ML_HEADS), F32)
    w_in_r = jnp.concatenate(
        [parts["qlat"], parts["ckv"], parts["kpe"], z64, _rot_half_cols(parts["kpe"]), z64,
         parts["mq"], parts["mk"], parts["mv"], parts["og"], parts["ig"], parts["fg"], zg], axis=1).astype(BF16)
    b_gate = jnp.concatenate([b_igate, b_fgate, jnp.zeros((LANES - 2 * ML_HEADS,), F32)]).reshape(1, LANES)
    w_nope = w_q_up[:, :, :MLA_NOPE].reshape(Q_LORA, MLA_HEADS * MLA_NOPE)
    w_rope = w_q_up[:, :, MLA_NOPE:]
    zr = jnp.zeros((Q_LORA, MLA_HEADS, LANES - MLA_ROPE), F32)
    w_qr = jnp.concatenate([w_rope, zr], axis=-1).reshape(Q_LORA, MLA_HEADS * LANES)
    w_qt = jnp.concatenate([_rot_half_cols(w_rope), zr], axis=-1).reshape(Q_LORA, MLA_HEADS * LANES)
    w_uk2 = w_uk.reshape(KV_LORA, MLA_HEADS * MLA_NOPE).astype(BF16)
    w_uv2 = w_uv.reshape(KV_LORA, MLA_HEADS * HEAD_OUT).astype(BF16)
    w_ukT = jnp.transpose(w_uk, (1, 2, 0)).astype(BF16)
    return dict(w_in=w_in_r, b_gate=b_gate, w_qn=w_nope.astype(BF16), w_qr=w_qr.astype(BF16),
                w_qt=w_qt.astype(BF16), w_uk=w_uk2, w_uv=w_uv2, w_ukT=w_ukT)


def _rope_tables(pos):
    half = MLA_ROPE // 2
    inv_freq = ROPE_BASE ** (-jnp.arange(half, dtype=F32) / half)
    ang = pos.astype(F32)[:, None] * inv_freq[None, :]
    z = jnp.zeros((pos.shape[0], LANES - MLA_ROPE), F32)
    cos = jnp.concatenate([jnp.cos(ang), jnp.cos(ang), z], axis=1)
    sin = jnp.concatenate([jnp.sin(ang), jnp.sin(ang), z], axis=1)
    return cos, sin


def _chunk_tril(tm, chunk):
    r = jnp.arange(tm)
    same = (r[:, None] // chunk) == (r[None, :] // chunk)
    return (same & (r[None, :] <= r[:, None])).astype(BF16)


def _split3(x):
    hi = x.astype(BF16)
    r1 = x - hi.astype(F32)
    mid = r1.astype(BF16)
    lo = (r1 - mid.astype(F32)).astype(BF16)
    return hi, mid, lo


def _proj_kernel(*refs, is_prompt):
    (x_ref, mod_ref, gpre_ref, win_ref, bg_ref, gq_ref, wqn_ref, wqr_ref, wqt_ref, gkv_ref,
     cos_ref, sin_ref, tril_ref) = refs[:13]
    if is_prompt:
        wuk_ref, wuv_ref = refs[13:15]
        q_ref, k_ref, v_ref, ckv_ref, kpe_ref, mq_ref, mk_ref, mv_ref, g_ref, cum_ref, og_ref = refs[15:]
    else:
        wukT_ref = refs[13]
        qa_ref, qr_ref, ckv_ref, kpe_ref, mq_ref, mk_ref, mv_ref, g_ref, cum_ref, og_ref = refs[14:]

    x = x_ref[...]
    mod = mod_ref[...]
    nb, t, d = x.shape
    h = _rms(x, gpre_ref[...]) * (1.0 + mod[:, 1:2, :]) + mod[:, 0:1, :]
    hb = h.reshape(nb * t, d).astype(BF16)

    def zcols(a, b):
        return _dot(hb, win_ref[:, a:b])

    cos = cos_ref[...]
    sin = sin_ref[...]

    qn = _rms(zcols(_C_QLAT, _C_CKV), gq_ref[...]).astype(BF16)
    q_nope = _dot(qn, wqn_ref[...]) * QK_SCALE
    q_r = _dot(qn, wqr_ref[...])
    q_t = _dot(qn, wqt_ref[...])
    ckv = _rms(zcols(_C_CKV, _C_KPE), gkv_ref[...])
    ckv_ref[...] = ckv
    kpe = zcols(_C_KPE, _C_KROT) * cos + zcols(_C_KROT, _C_MQ) * sin
    kpe_ref[...] = kpe[:, :MLA_ROPE]
    for hd in range(MLA_HEADS):
        sl = slice(hd * LANES, (hd + 1) * LANES)
        roped = (q_r[:, sl] * cos + q_t[:, sl] * sin) * QK_SCALE
        if is_prompt:
            q_ref[:, 2 * hd * LANES:(2 * hd + 1) * LANES] = q_nope[:, sl].astype(BF16)
            q_ref[:, (2 * hd + 1) * LANES:(2 * hd + 2) * LANES] = roped.astype(BF16)
        else:
            qa_ref[:, hd * KV_LORA:(hd + 1) * KV_LORA] = _dot(q_nope[:, sl].astype(BF16), wukT_ref[hd])
            qr_ref[:, sl] = roped
    if is_prompt:
        cb = ckv.astype(BF16)
        k_nope = _dot(cb, wuk_ref[...])
        kpe_b = kpe.astype(BF16)
        for hd in range(MLA_HEADS):
            k_ref[:, 2 * hd * LANES:(2 * hd + 1) * LANES] = k_nope[:, hd * LANES:(hd + 1) * LANES].astype(BF16)
            k_ref[:, (2 * hd + 1) * LANES:(2 * hd + 2) * LANES] = kpe_b
        v_ref[...] = _dot(cb, wuv_ref[...]).astype(BF16)

    mq_ref[...] = zcols(_C_MQ, _C_MK) * (ML_QK ** -0.5)
    mk_ref[...] = zcols(_C_MK, _C_MV)
    mv_ref[...] = zcols(_C_MV, _C_OG)
    og_ref[...] = jax.nn.sigmoid(zcols(_C_OG, _C_GATE))
    zg = zcols(_C_GATE, _C_END) + bg_ref[...]
    ig = IGATE_CAP * jnp.tanh(zg / IGATE_CAP)
    lf = -(jnp.maximum(-zg, 0.0) + jnp.log1p(jnp.exp(-jnp.abs(zg))))
    lane = lax.broadcasted_iota(jnp.int32, zg.shape, 1)
    g = jnp.where(lane < ML_HEADS, ig, lf)
    g_ref[...] = g
    tril = tril_ref[...]
    hi, mid, lo = _split3(g)
    cum_ref[...] = _dot(tril, hi) + _dot(tril, mid) + _dot(tril, lo)


def _project(x, mod, pos_tab, chunk, g_pre, g_qlat, g_kvlat, pw, is_prompt):
    nseq, T, d = x.shape
    n = nseq * T
    if is_prompt:
        nb, t = 1, min(TOKEN_TILE, T)
    else:
        t = T
        nb = max(1, min(nseq, TOKEN_TILE // T))
    assert T % t == 0 and nseq % nb == 0
    tm = nb * t
    tiles_per_seq = T // t
    grid = (n // tm,)
    cos, sin = pos_tab
    if is_prompt:
        x_map = lambda i: (i // tiles_per_seq, i % tiles_per_seq, 0)
        mod_map = lambda i: (i // tiles_per_seq, 0, 0)
        tab_map = lambda i: (i % tiles_per_seq, 0)
    else:
        x_map = lambda i: (i, 0, 0)
        mod_map = lambda i: (i, 0, 0)
        tab_map = lambda i: (0, 0)
        cos = jnp.tile(cos, (nb, 1))
        sin = jnp.tile(sin, (nb, 1))
    const = lambda i: (0, 0)
    row = lambda i: (i, 0)
    tril = _chunk_tril(tm, chunk)
    in_specs = [
        pl.BlockSpec((nb, t, d), x_map),
        pl.BlockSpec((nb, 6, d), mod_map),
        pl.BlockSpec((1, d), const),
        pl.BlockSpec((d, _C_END), const),
        pl.BlockSpec((1, LANES), const),
        pl.BlockSpec((1, Q_LORA), const),
        pl.BlockSpec((Q_LORA, MLA_HEADS * MLA_NOPE), const),
        pl.BlockSpec((Q_LORA, MLA_HEADS * LANES), const),
        pl.BlockSpec((Q_LORA, MLA_HEADS * LANES), const),
        pl.BlockSpec((1, KV_LORA), const),
        pl.BlockSpec((tm, LANES), tab_map),
        pl.BlockSpec((tm, LANES), tab_map),
        pl.BlockSpec((tm, tm), const),
    ]
    args = [x, mod, g_pre.reshape(1, d), pw["w_in"], pw["b_gate"], g_qlat.reshape(1, Q_LORA), pw["w_qn"],
            pw["w_qr"], pw["w_qt"], g_kvlat.reshape(1, KV_LORA), cos, sin, tril]

    def out(cols, dtype=F32):
        return jax.ShapeDtypeStruct((n, cols), dtype), pl.BlockSpec((tm, cols), row)

    if is_prompt:
        in_specs += [pl.BlockSpec((KV_LORA, MLA_HEADS * MLA_NOPE), const),
                     pl.BlockSpec((KV_LORA, MLA_HEADS * HEAD_OUT), const)]
        args += [pw["w_uk"], pw["w_uv"]]
        outs = [out(MLA_HEADS * 2 * LANES, BF16), out(MLA_HEADS * 2 * LANES, BF16), out(MLA_HEADS * HEAD_OUT, BF16)]
    else:
        in_specs += [pl.BlockSpec((MLA_HEADS, MLA_NOPE, KV_LORA), lambda i: (0, 0, 0))]
        args += [pw["w_ukT"]]
        outs = [out(MLA_HEADS * KV_LORA), out(MLA_HEADS * LANES)]
    outs += [out(KV_LORA), out(MLA_ROPE), out(ML_HEADS * ML_QK), out(ML_HEADS * ML_QK), out(ML_HEADS * ML_V),
             out(LANES), out(LANES), out(ML_HEADS * ML_V)]
    return pl.pallas_call(
        functools.partial(_proj_kernel, is_prompt=is_prompt),
        out_shape=[o[0] for o in outs],
        grid=grid,
        in_specs=in_specs,
        out_specs=[o[1] for o in outs],
        compiler_params=_cparams(("arbitrary",)),
        name="proj_prompt" if is_prompt else "proj_sample",
    )(*args)


def _attn_kernel(qi_ref, ki_ref, q_ref, k_ref, v_ref, o_ref, m_ref, l_ref, acc_ref):
    p = pl.program_id(1)
    qi = qi_ref[p]
    ki = ki_ref[p]
    tq = q_ref.shape[0]
    tk = k_ref.shape[0]

    @pl.when(ki == 0)
    def _():
        m_ref[...] = jnp.full(m_ref.shape, NEG_INF, F32)
        l_ref[...] = jnp.zeros(l_ref.shape, F32)
        acc_ref[...] = jnp.zeros(acc_ref.shape, F32)

    def update(masked):
        for hd in range(MLA_HEADS):
            qh = q_ref[:, hd * 2 * LANES:(hd + 1) * 2 * LANES]
            kh = k_ref[:, hd * 2 * LANES:(hd + 1) * 2 * LANES]
            s = lax.dot_general(qh, kh, NT_DIMS, preferred_element_type=F32)
            if masked:
                row = lax.broadcasted_iota(jnp.int32, s.shape, 0)
                col = lax.broadcasted_iota(jnp.int32, s.shape, 1)
                s = jnp.where(col <= row, s, NEG_INF)
            m_prev = m_ref[hd]
            m_next = jnp.maximum(m_prev, jnp.max(s, axis=1, keepdims=True))
            alpha = jnp.exp2(m_prev - m_next)
            pr = jnp.exp2(s - jnp.concatenate([m_next] * (tk // LANES), axis=1))
            l_ref[hd] = alpha * l_ref[hd] + jnp.sum(pr, axis=1, keepdims=True)
            acc_ref[hd] = alpha * acc_ref[hd] + _dot(pr.astype(BF16), v_ref[:, hd * HEAD_OUT:(hd + 1) * HEAD_OUT])
            m_ref[hd] = m_next

    @pl.when(ki < qi)
    def _():
        update(False)

    @pl.when(ki == qi)
    def _():
        update(True)
        for hd in range(MLA_HEADS):
            o_ref[:, hd * HEAD_OUT:(hd + 1) * HEAD_OUT] = acc_ref[hd] / l_ref[hd]


def _prompt_attention(q, k, v, nseq, S):
    blk = min(ATTN_BLOCK, S)
    nq = S // blk
    pairs = [(i, j) for i in range(nq) for j in range(i + 1)]
    qi_tab = jnp.asarray([p[0] for p in pairs], jnp.int32)
    ki_tab = jnp.asarray([p[1] for p in pairs], jnp.int32)
    wq = MLA_HEADS * 2 * LANES
    wv = MLA_HEADS * HEAD_OUT
    return pl.pallas_call(
        _attn_kernel,
        out_shape=jax.ShapeDtypeStruct((nseq * S, wv), F32),
        grid_spec=pltpu.PrefetchScalarGridSpec(
            num_scalar_prefetch=2,
            grid=(nseq, len(pairs)),
            in_specs=[pl.BlockSpec((blk, wq), lambda b, p, qi, ki: (b * nq + qi[p], 0)),
                      pl.BlockSpec((blk, wq), lambda b, p, qi, ki: (b * nq + ki[p], 0)),
                      pl.BlockSpec((blk, wv), lambda b, p, qi, ki: (b * nq + ki[p], 0))],
            out_specs=pl.BlockSpec((blk, wv), lambda b, p, qi, ki: (b * nq + qi[p], 0)),
            scratch_shapes=[pltpu.VMEM((MLA_HEADS, blk, LANES), F32),
                            pltpu.VMEM((MLA_HEADS, blk, LANES), F32),
                            pltpu.VMEM((MLA_HEADS, blk, HEAD_OUT), F32)]),
        compiler_params=_cparams(("arbitrary", "arbitrary")),
        name="prompt_attention",
    )(qi_tab, ki_tab, q, k, v)


def _sattn_kernel(pt_ref, qa_ref, qr_ref, ckvn_ref, kpen_ref, wuv_ref, cckv_hbm, ckpe_hbm, o_ref,
                  ckv_buf, kpe_buf, sem, kcb, q_s, qr_s, kn_s, pn_s, m_ref, l_ref, acc_ref,
                  *, layer, npg, nchunks, sub):
    b = pl.program_id(0)
    c = pl.program_id(1)
    step = b * nchunks + c
    nsteps = pl.num_programs(0) * nchunks
    slot = step % 2
    T = qa_ref.shape[0]
    rows = MLA_HEADS * T
    npos = npg * PAGE_SIZE

    def page_copies(page_of, slt):
        cps = []
        for j in range(npg):
            pg = page_of(j)
            dst = pl.ds(j * PAGE_SIZE, PAGE_SIZE)
            cps.append(pltpu.make_async_copy(cckv_hbm.at[layer, pg], ckv_buf.at[slt, dst, :], sem.at[0, slt]))
            cps.append(pltpu.make_async_copy(ckpe_hbm.at[layer, pg], kpe_buf.at[slt, :, dst], sem.at[1, slt]))
        return cps

    def fetch(stp, slt):
        for cp in page_copies(lambda j: pt_ref[stp * npg + j], slt):
            cp.start()

    @pl.when(step == 0)
    def _():
        fetch(0, 0)

    @pl.when(step + 1 < nsteps)
    def _():
        fetch(step + 1, 1 - slot)

    @pl.when(c == 0)
    def _():
        m_ref[...] = jnp.full(m_ref.shape, NEG_INF, F32)
        l_ref[...] = jnp.zeros(l_ref.shape, F32)
        acc_ref[...] = jnp.zeros(acc_ref.shape, F32)
        qa = qa_ref[...]
        qr = qr_ref[...]
        q_s[...] = jnp.concatenate([qa[:, hd * KV_LORA:(hd + 1) * KV_LORA] for hd in range(MLA_HEADS)],
                                   axis=0).astype(BF16)
        qr_s[...] = jnp.concatenate([qr[:, hd * LANES:(hd + 1) * LANES] for hd in range(MLA_HEADS)],
                                    axis=0).astype(BF16)

    for cp in page_copies(lambda j: 0, slot):
        cp.wait()

    q = q_s[...]
    qr = qr_s[:, :MLA_ROPE]

    def softmax_update(s, pv):
        m_prev = m_ref[...]
        m_next = jnp.maximum(m_prev, jnp.max(s, axis=1, keepdims=True))
        alpha = jnp.exp2(m_prev - m_next)
        pr = jnp.exp2(s - jnp.concatenate([m_next] * (s.shape[1] // LANES), axis=1))
        l_ref[...] = alpha * l_ref[...] + jnp.sum(pr, axis=1, keepdims=True)
        acc_ref[...] = jnp.concatenate([alpha] * (KV_LORA // LANES), axis=1) * acc_ref[...] + pv(pr.astype(BF16))
        m_ref[...] = m_next

    s_parts = []
    for sb in range(npos // sub):
        sl = slice(sb * sub, (sb + 1) * sub)
        kc = ckv_buf[slot, sl, :].astype(BF16)
        kcb[sl, :] = kc
        kp = kpe_buf[slot, :, sl].astype(BF16)
        s_parts.append(lax.dot_general(q, kc, NT_DIMS, preferred_element_type=F32) + _dot(qr, kp))

    def pv_chunk(p):
        out = _dot(p[:, 0:sub], kcb[0:sub, :])
        for sb in range(1, npos // sub):
            out = out + _dot(p[:, sb * sub:(sb + 1) * sub], kcb[sb * sub:(sb + 1) * sub, :])
        return out

    softmax_update(jnp.concatenate(s_parts, axis=1), pv_chunk)

    @pl.when(c == nchunks - 1)
    def _():
        kn_s[...] = jnp.zeros(kn_s.shape, F32)
        pn_s[...] = jnp.zeros(pn_s.shape, F32)
        kn_s[0:T, :] = ckvn_ref[...]
        pn_s[0:T, :] = kpen_ref[...]
        row = lax.broadcasted_iota(jnp.int32, (rows, LANES), 0)
        col = lax.broadcasted_iota(jnp.int32, (rows, LANES), 1)
        kn = kn_s[...].astype(BF16)
        s_new = (lax.dot_general(q, kn, NT_DIMS, preferred_element_type=F32)
                 + lax.dot_general(qr, pn_s[...].astype(BF16), NT_DIMS, preferred_element_type=F32))
        softmax_update(jnp.where(col <= row % T, s_new, NEG_INF), lambda p: _dot(p, kn))
        o_lat = acc_ref[...] / jnp.concatenate([l_ref[...]] * (KV_LORA // LANES), axis=1)
        for hd in range(MLA_HEADS):
            o_ref[:, hd * HEAD_OUT:(hd + 1) * HEAD_OUT] = _dot(o_lat[hd * T:(hd + 1) * T, :], wuv_ref[hd])


def _sample_attention(qa, qr, ckv_new, kpe_new, cache_ckv, cache_kpe, layer, page_table, w_uv):
    nseq, n_pages = page_table.shape
    T = qa.shape[0] // nseq
    npg = min(PAGES_PER_STEP, n_pages)
    assert n_pages % npg == 0
    nchunks = n_pages // npg
    sub = min(ATTN_SUB, npg * PAGE_SIZE)
    rows = MLA_HEADS * T
    seq = lambda b, c, pt: (b, 0)
    kern = functools.partial(_sattn_kernel, layer=layer, npg=npg, nchunks=nchunks, sub=sub)
    return pl.pallas_call(
        kern,
        out_shape=jax.ShapeDtypeStruct((nseq * T, MLA_HEADS * HEAD_OUT), F32),
        grid_spec=pltpu.PrefetchScalarGridSpec(
            num_scalar_prefetch=1,
            grid=(nseq, nchunks),
            in_specs=[pl.BlockSpec((T, MLA_HEADS * KV_LORA), seq),
                      pl.BlockSpec((T, MLA_HEADS * LANES), seq),
                      pl.BlockSpec((T, KV_LORA), seq),
                      pl.BlockSpec((T, MLA_ROPE), seq),
                      pl.BlockSpec((MLA_HEADS, KV_LORA, HEAD_OUT), lambda b, c, pt: (0, 0, 0)),
                      pl.BlockSpec(memory_space=pl.ANY),
                      pl.BlockSpec(memory_space=pl.ANY)],
            out_specs=pl.BlockSpec((T, MLA_HEADS * HEAD_OUT), seq),
            scratch_shapes=[pltpu.VMEM((2, npg * PAGE_SIZE, KV_LORA), F32),
                            pltpu.VMEM((2, MLA_ROPE, npg * PAGE_SIZE), F32),
                            pltpu.SemaphoreType.DMA((2, 2)),
                            pltpu.VMEM((npg * PAGE_SIZE, KV_LORA), BF16),
                            pltpu.VMEM((rows, KV_LORA), BF16),
                            pltpu.VMEM((rows, LANES), BF16),
                            pltpu.VMEM((LANES, KV_LORA), F32),
                            pltpu.VMEM((LANES, MLA_ROPE), F32),
                            pltpu.VMEM((rows, LANES), F32),
                            pltpu.VMEM((rows, LANES), F32),
                            pltpu.VMEM((rows, KV_LORA), F32)]),
        compiler_params=_cparams(("arbitrary", "arbitrary")),
        name="sample_attention",
    )(page_table.reshape(-1), qa, qr, ckv_new, kpe_new, w_uv, cache_ckv, jnp.swapaxes(cache_kpe, 2, 3))


def _mlstm_kernel(mq_ref, mk_ref, mv_ref, g_ref, cum_ref, c0_ref, n0_ref, m0_ref,
                  h_ref, c_ref, n_ref, m_ref, *, unroll_seqs, dot_dtype):
    ci = pl.program_id(1)
    nbs, L = mq_ref.shape[0], mq_ref.shape[1]

    @pl.when(ci == 0)
    def _():
        c_ref[...] = c0_ref[...]
        n_ref[...] = n0_ref[...]
        m_ref[...] = m0_ref[...]

    trow = lax.broadcasted_iota(jnp.int32, (L, L), 0)
    tcol = lax.broadcasted_iota(jnp.int32, (L, L), 1)
    causal = tcol <= trow

    def one_seq(i):
        g = g_ref[i]
        cum = cum_ref[i]
        g_t = g.T
        cum_t = cum.T
        mq = mq_ref[i]
        mk = mk_ref[i]
        mv = mv_ref[i]
        for hd in range(ML_HEADS):
            q = mq[:, hd * ML_QK:(hd + 1) * ML_QK]
            k = mk[:, hd * ML_QK:(hd + 1) * ML_QK]
            v = mv[:, hd * ML_V:(hd + 1) * ML_V]
            ig_col = g[:, hd:hd + 1]
            ig_row = g_t[hd:hd + 1, :]
            b_col = cum[:, ML_HEADS + hd:ML_HEADS + hd + 1]
            b_row = cum_t[ML_HEADS + hd:ML_HEADS + hd + 1, :]
            m_prev = m_ref[i, hd:hd + 1, 0:1]
            c_prev = c_ref[i, hd]
            n_prev = n_ref[i, hd:hd + 1, :]
            log_d = jnp.where(causal, b_col - b_row + ig_row, NEG_INF)
            log_inter = b_col + m_prev
            m_t = jnp.maximum(log_inter, jnp.max(log_d, axis=1, keepdims=True))
            d = jnp.exp(log_d - m_t)
            inter = jnp.exp(log_inter - m_t)
            qd = q.astype(dot_dtype)
            kd = k.astype(dot_dtype)
            s = lax.dot_general(qd, kd, NT_DIMS, preferred_element_type=F32) * d
            num = (inter * lax.dot_general(qd, c_prev.astype(dot_dtype), NT_DIMS, preferred_element_type=F32)
                   + _dot(s.astype(dot_dtype), v.astype(dot_dtype)))
            den = inter * jnp.sum(q * n_prev, axis=1, keepdims=True) + jnp.sum(s, axis=1, keepdims=True)
            h_ref[i, :, hd * ML_V:(hd + 1) * ML_V] = num / jnp.maximum(jnp.abs(den), jnp.exp(-m_t))
            m_new = m_t[L - 1:L, :]
            b_last = b_col[L - 1:L, :]
            w_col = jnp.exp(b_last - b_col + ig_col - m_new)
            decay = jnp.exp(b_last + m_prev - m_new)
            c_ref[i, hd] = decay * c_prev + lax.dot_general((w_col * v).astype(dot_dtype), kd, TN_DIMS,
                                                            preferred_element_type=F32)
            n_ref[i, hd:hd + 1, :] = decay * n_prev + jnp.sum(w_col * k, axis=0, keepdims=True)
            m_ref[i, hd:hd + 1, :] = jnp.broadcast_to(m_new, (1, LANES))

    if unroll_seqs:
        for i in range(nbs):
            one_seq(i)
    else:
        def body(i, carry):
            one_seq(i)
            return carry
        lax.fori_loop(0, nbs, body, 0)


def _mlstm(mq, mk, mv, g, cum, c0, n0, m0, nseq, T):
    L = ML_CHUNK if T % ML_CHUNK == 0 else T
    nc = T // L
    nbs = nseq if nseq <= 2 else 16
    assert nseq % nbs == 0
    r3 = lambda a: a.reshape(nseq, T, a.shape[-1])
    tok = lambda gi, ci: (gi, ci, 0)
    st4 = lambda gi, ci: (gi, 0, 0, 0)
    st3 = lambda gi, ci: (gi, 0, 0)
    m0b = jnp.broadcast_to(m0.astype(F32)[:, :, None], (nseq, ML_HEADS, LANES))
    kern = functools.partial(_mlstm_kernel, unroll_seqs=nbs <= 2, dot_dtype=BF16 if L >= 16 else F32)
    h, c, n, m = pl.pallas_call(
        kern,
        out_shape=[jax.ShapeDtypeStruct((nseq, T, ML_HEADS * ML_V), F32),
                   jax.ShapeDtypeStruct((nseq, ML_HEADS, ML_V, ML_QK), F32),
                   jax.ShapeDtypeStruct((nseq, ML_HEADS, ML_QK), F32),
                   jax.ShapeDtypeStruct((nseq, ML_HEADS, LANES), F32)],
        grid=(nseq // nbs, nc),
        in_specs=[pl.BlockSpec((nbs, L, ML_HEADS * ML_QK), tok),
                  pl.BlockSpec((nbs, L, ML_HEADS * ML_QK), tok),
                  pl.BlockSpec((nbs, L, ML_HEADS * ML_V), tok),
                  pl.BlockSpec((nbs, L, LANES), tok),
                  pl.BlockSpec((nbs, L, LANES), tok),
                  pl.BlockSpec((nbs, ML_HEADS, ML_V, ML_QK), st4),
                  pl.BlockSpec((nbs, ML_HEADS, ML_QK), st3),
                  pl.BlockSpec((nbs, ML_HEADS, LANES), st3)],
        out_specs=[pl.BlockSpec((nbs, L, ML_HEADS * ML_V), tok),
                   pl.BlockSpec((nbs, ML_HEADS, ML_V, ML_QK), st4),
                   pl.BlockSpec((nbs, ML_HEADS, ML_QK), st3),
                   pl.BlockSpec((nbs, ML_HEADS, LANES), st3)],
        compiler_params=_cparams(("arbitrary", "arbitrary")),
        name="mlstm_prompt" if nseq <= 2 else "mlstm_sample",
    )(r3(mq), r3(mk), r3(mv), r3(g), r3(cum), c0.astype(F32), n0.astype(F32), m0b)
    return h.reshape(nseq * T, ML_HEADS * ML_V), c, n, m[:, :, 0]


def _merge_kernel(x_ref, mod_ref, omla_ref, hml_ref, og_ref, gh_ref, wout_ref, gpm_ref, gpf_ref,
                  wrh_ref, wrl_ref, br_ref, x1_ref, h2_ref, idx_ref, gate_ref):
    nb, t, d = x_ref.shape
    tm = nb * t
    gh = gh_ref[...]
    parts = []
    for hd in range(MLA_HEADS + ML_HEADS):
        if hd < MLA_HEADS:
            xh = omla_ref[:, hd * HEAD_OUT:(hd + 1) * HEAD_OUT]
        else:
            j = hd - MLA_HEADS
            xh = hml_ref[:, j * HEAD_OUT:(j + 1) * HEAD_OUT]
        yh = _rms(xh, gh[:, hd * HEAD_OUT:(hd + 1) * HEAD_OUT])
        if hd >= MLA_HEADS:
            yh = yh * og_ref[:, j * HEAD_OUT:(j + 1) * HEAD_OUT]
        parts.append(yh.astype(BF16))
    merged = jnp.concatenate(parts, axis=1)
    mix = _dot(merged, wout_ref[...])
    mod = mod_ref[...]
    x1 = x_ref[...] + mod[:, 2:3, :] * _rms(mix, gpm_ref[...]).reshape(nb, t, d)
    x1_ref[...] = x1
    h2 = (_rms(x1, gpf_ref[...]) * (1.0 + mod[:, 4:5, :]) + mod[:, 3:4, :]).reshape(tm, d)
    _store_row_tiles(h2_ref, h2)
    hh = h2.astype(BF16)
    hl = (h2 - hh.astype(F32)).astype(BF16)
    wrh = wrh_ref[...]
    logits = (lax.dot_general(wrh, hh, NT_DIMS, preferred_element_type=F32)
              + lax.dot_general(wrh, hl, NT_DIMS, preferred_element_type=F32)
              + lax.dot_general(wrl_ref[...], hh, NT_DIMS, preferred_element_type=F32)) + br_ref[...]
    eidx = lax.broadcasted_iota(jnp.int32, logits.shape, 0)
    work = logits
    vals, idxs = [], []
    for _ in range(TOP_K):
        mx = jnp.max(work, axis=0, keepdims=True)
        sel = jnp.min(jnp.where(work == mx, eidx, N_EXPERTS), axis=0, keepdims=True)
        vals.append(mx)
        idxs.append(sel)
        work = jnp.where(eidx == sel, NEG_INF, work)
    ex = [jnp.exp(v - vals[0]) for v in vals]
    tot = ex[0] + ex[1] + ex[2] + ex[3]
    zi = jnp.zeros((8 - TOP_K, tm), jnp.int32)
    zf = jnp.zeros((8 - TOP_K, tm), F32)
    idx_ref[...] = jnp.concatenate(idxs + [zi], axis=0)
    gate_ref[...] = jnp.concatenate([e / tot for e in ex] + [zf], axis=0)


def _merge(x, mod, o_mla, h_ml, og, g_heads, w_out, g_post_mix, g_pre_ffn, wr_hi, wr_lo, b_router, is_prompt):
    nseq, T, d = x.shape
    n = nseq * T
    if is_prompt:
        nb, t = 1, min(TOKEN_TILE, T)
    else:
        t = T
        nb = max(1, min(nseq, TOKEN_TILE // T))
    tm = nb * t
    tiles_per_seq = T // t
    if is_prompt:
        x_map = lambda i: (i // tiles_per_seq, i % tiles_per_seq, 0)
        mod_map = lambda i: (i // tiles_per_seq, 0, 0)
    else:
        x_map = lambda i: (i, 0, 0)
        mod_map = lambda i: (i, 0, 0)
    const = lambda i: (0, 0)
    row = lambda i: (i, 0)
    col = lambda i: (0, i)
    wh = MLA_HEADS * HEAD_OUT
    return pl.pallas_call(
        _merge_kernel,
        out_shape=[jax.ShapeDtypeStruct((nseq, T, d), F32), jax.ShapeDtypeStruct((n * ROW_TILE, LANES), F32),
                   jax.ShapeDtypeStruct((8, n), jnp.int32), jax.ShapeDtypeStruct((8, n), F32)],
        grid=(n // tm,),
        in_specs=[pl.BlockSpec((nb, t, d), x_map),
                  pl.BlockSpec((nb, 6, d), mod_map),
                  pl.BlockSpec((tm, wh), row),
                  pl.BlockSpec((tm, wh), row),
                  pl.BlockSpec((tm, wh), row),
                  pl.BlockSpec((1, 2 * wh), const),
                  pl.BlockSpec((2 * wh, d), const),
                  pl.BlockSpec((1, d), const),
                  pl.BlockSpec((1, d), const),
                  pl.BlockSpec((N_EXPERTS, d), const),
                  pl.BlockSpec((N_EXPERTS, d), const),
                  pl.BlockSpec((N_EXPERTS, 1), const)],
        out_specs=[pl.BlockSpec((nb, t, d), x_map), pl.BlockSpec((tm * ROW_TILE, LANES), row),
                   pl.BlockSpec((8, tm), col), pl.BlockSpec((8, tm), col)],
        compiler_params=_cparams(("arbitrary",)),
        name="merge_prompt" if is_prompt else "merge_sample",
    )(x, mod, o_mla, h_ml, og, g_heads.reshape(1, 2 * wh), w_out, g_post_mix.reshape(1, d),
      g_pre_ffn.reshape(1, d), wr_hi, wr_lo, b_router.reshape(N_EXPERTS, 1))


def _moe_kernel(be_ref, tok_ref, sh_ref, nu_ref, h2_hbm, wgu_ref, bgu_ref, wd_ref, bd_ref, y_ref,
                xbuf, sem, wgu_s, wd_s):
    blk = pl.program_id(0)
    nblk = pl.num_programs(0)
    slot = blk % 2
    tb = y_ref.shape[0] // ROW_TILE
    de = wd_ref.shape[0]
    n_used = nu_ref[0]

    def row_copy(tok8, r, slt):
        src = h2_hbm.at[pl.ds(pl.multiple_of(tok8, ROW_TILE), ROW_TILE), :]
        return pltpu.make_async_copy(src, xbuf.at[slt, pl.ds(r * ROW_TILE, ROW_TILE), :], sem.at[slt])

    def rows_base(bk):
        return bk * tb + sh_ref[be_ref[jnp.minimum(bk, nblk - 1)]]

    def wait_rows(slt):
        pltpu.make_async_copy(h2_hbm.at[pl.ds(0, tb * ROW_TILE), :], xbuf.at[slt], sem.at[slt]).wait()

    @pl.when(blk == 0)
    def _():
        base = rows_base(0)

        def body(r, carry):
            row_copy(tok_ref[base + r], r, 0).start()
            return carry
        lax.fori_loop(0, tb, body, 0, unroll=8)

    hot = blk < n_used
    e = be_ref[blk]
    e_prev = be_ref[jnp.maximum(blk - 1, 0)]

    @pl.when(jnp.logical_and(hot, jnp.logical_or(blk == 0, e != e_prev)))
    def _():
        wgu_s[...] = wgu_ref[...].astype(BF16)
        wd_s[...] = wd_ref[...].astype(BF16)

    @pl.when(hot)
    def _():
        wait_rows(slot)
        xb = _load_row_tiles(xbuf.at[slot], tb).astype(BF16)
        base = rows_base(blk + 1)
        for r in range(tb):
            row_copy(tok_ref[base + r], r, 1 - slot).start()
        gu = _dot(xb, wgu_s[...]) + bgu_ref[...]
        gt = jnp.minimum(gu[:, :de], SWIGLU_LIMIT)
        up = jnp.clip(gu[:, de:], -SWIGLU_LIMIT, SWIGLU_LIMIT)
        act = gt * jax.nn.sigmoid(SWIGLU_ALPHA * gt) * (up + 1.0)
        _store_row_tiles(y_ref, _dot(act.astype(BF16), wd_s[...]) + bd_ref[...])

    @pl.when(blk == n_used)
    def _():
        wait_rows(slot)

    @pl.when(jnp.logical_and(hot, blk == nblk - 1))
    def _():
        wait_rows(1 - slot)

    @pl.when(jnp.logical_not(hot))
    def _():
        y_ref[...] = jnp.zeros(y_ref.shape, F32)


def _moe_experts(h2, block_e, tok_sorted, shift, n_used, w_gate_up, b_gate_up, w_down, b_down):
    ne, d, de2 = w_gate_up.shape
    de = w_down.shape[1]
    n_blocks = block_e.shape[0]
    tb = MOE_ROWS
    wmap = lambda i, be, tk, sh, nu: (be[i], 0, 0)
    return pl.pallas_call(
        _moe_kernel,
        out_shape=jax.ShapeDtypeStruct((n_blocks * tb * ROW_TILE, LANES), F32),
        grid_spec=pltpu.PrefetchScalarGridSpec(
            num_scalar_prefetch=4,
            grid=(n_blocks,),
            in_specs=[pl.BlockSpec(memory_space=pl.ANY),
                      pl.BlockSpec((None, d, de2), wmap),
                      pl.BlockSpec((None, 1, de2), wmap),
                      pl.BlockSpec((None, de, d), wmap),
                      pl.BlockSpec((None, 1, d), wmap)],
            out_specs=pl.BlockSpec((tb * ROW_TILE, LANES), lambda i, be, tk, sh, nu: (i, 0)),
            scratch_shapes=[pltpu.VMEM((2, tb * ROW_TILE, LANES), F32),
                            pltpu.SemaphoreType.DMA((2,)),
                            pltpu.VMEM((d, de2), BF16),
                            pltpu.VMEM((de, d), BF16)]),
        compiler_params=_cparams(("arbitrary",)),
        name="moe_experts",
    )(block_e, tok_sorted, shift, n_used, h2, w_gate_up, b_gate_up.reshape(ne, 1, de2), w_down,
      b_down.reshape(ne, 1, d))


def _route(top_idx, n_blocks):
    n = top_idx.shape[0]
    m = n * TOP_K
    tb = MOE_ROWS
    flat_e = top_idx.reshape(m)
    pair = jnp.arange(m, dtype=jnp.int32)
    order = lax.sort(flat_e * m + pair) % m
    tok_sorted = jnp.concatenate([order // TOP_K * ROW_TILE, jnp.zeros((2 * tb,), jnp.int32)])
    onehot = (flat_e[:, None] == jnp.arange(N_EXPERTS, dtype=jnp.int32)[None, :]).astype(jnp.int32)
    csum = jnp.cumsum(onehot, axis=0)
    counts = csum[-1]
    padded = (counts + tb - 1) // tb * tb
    pends = jnp.cumsum(padded)
    pstarts = pends - padded
    starts = jnp.cumsum(counts) - counts
    dest = (jnp.sum(onehot * (csum - 1 + pstarts[None, :]), axis=1) * ROW_TILE).astype(jnp.int32)
    blk_row = jnp.arange(n_blocks, dtype=jnp.int32) * tb
    block_e = jnp.minimum(jnp.sum((pends[None, :] <= blk_row[:, None]).astype(jnp.int32), axis=1), N_EXPERTS - 1)
    n_used = (pends[-1] // tb).astype(jnp.int32).reshape(1)
    return dest, tok_sorted, (starts - pstarts).astype(jnp.int32), block_e.astype(jnp.int32), n_used


def _combine_kernel(dest_ref, yb_hbm, x1_ref, mod_ref, gate_ref, gpost_ref, o_ref, ybuf, sem, *, base):
    i = pl.program_id(0)
    nsteps = pl.num_programs(0)
    slot = i % 2
    nb, t, d = x1_ref.shape
    tm = nb * t

    def gather(step, slt):
        def body(r, carry):
            for kk in range(TOP_K):
                row8 = pl.multiple_of(dest_ref[base + (step * tm + r) * TOP_K + kk], ROW_TILE)
                dst = ybuf.at[slt, kk, pl.ds(pl.multiple_of(r * ROW_TILE, ROW_TILE), ROW_TILE), :]
                pltpu.make_async_copy(yb_hbm.at[pl.ds(row8, ROW_TILE), :], dst, sem.at[slt]).start()
            return carry
        lax.fori_loop(0, tm, body, 0, unroll=4)

    @pl.when(i == 0)
    def _():
        gather(0, 0)

    @pl.when(i + 1 < nsteps)
    def _():
        gather(i + 1, 1 - slot)

    for kk in range(TOP_K):
        pltpu.make_async_copy(yb_hbm.at[pl.ds(0, tm * ROW_TILE), :], ybuf.at[slot, kk], sem.at[slot]).wait()

    gate = gate_ref[...]
    y = gate[:, 0:1] * _load_row_tiles(ybuf.at[slot, 0], tm)
    for kk in range(1, TOP_K):
        y = y + gate[:, kk:kk + 1] * _load_row_tiles(ybuf.at[slot, kk], tm)
    mod = mod_ref[...]
    o_ref[...] = x1_ref[...] + mod[:, 5:6, :] * _rms(y, gpost_ref[...]).reshape(nb, t, d)


def _combine(yb, dest, base, x1, mod, gates, g_post_ffn, is_prompt):
    nseq, T, d = x1.shape
    n = nseq * T
    if is_prompt:
        nb, t = 1, min(COMBINE_TILE, T)
    else:
        t = T
        nb = max(1, min(nseq, COMBINE_TILE // T))
    tm = nb * t
    tiles_per_seq = T // t
    if is_prompt:
        x_map = lambda i, dr: (i // tiles_per_seq, i % tiles_per_seq, 0)
        mod_map = lambda i, dr: (i // tiles_per_seq, 0, 0)
    else:
        x_map = lambda i, dr: (i, 0, 0)
        mod_map = lambda i, dr: (i, 0, 0)
    return pl.pallas_call(
        functools.partial(_combine_kernel, base=base),
        out_shape=jax.ShapeDtypeStruct((nseq, T, d), F32),
        grid_spec=pltpu.PrefetchScalarGridSpec(
            num_scalar_prefetch=1,
            grid=(n // tm,),
            in_specs=[pl.BlockSpec(memory_space=pl.ANY),
                      pl.BlockSpec((nb, t, d), x_map),
                      pl.BlockSpec((nb, 6, d), mod_map),
                      pl.BlockSpec((tm, TOP_K), lambda i, dr: (i, 0)),
                      pl.BlockSpec((1, d), lambda i, dr: (0, 0))],
            out_specs=pl.BlockSpec((nb, t, d), x_map),
            scratch_shapes=[pltpu.VMEM((2, TOP_K, tm * ROW_TILE, LANES), F32),
                            pltpu.SemaphoreType.DMA((2,))]),
        compiler_params=_cparams(("arbitrary",)),
        name="combine_prompt" if is_prompt else "combine_sample",
    )(dest, yb, x1, mod, gates, g_post_ffn.reshape(1, d))


def _layer(xp, xs, cache_ckv, cache_kpe, layer, page_table, st_c, st_n, st_m, c_p, c_s, w):
    nbp, S, d = xp.shape
    nbs, T, _ = xs.shape
    past_len = page_table.shape[1] * PAGE_SIZE
    dt = xp.dtype

    mod = _modulation(jnp.concatenate([c_p, c_s], axis=0), w["w_ada"], w["b_ada"])
    mod = mod.reshape(nbp + nbs, 6, d)
    mod_p, mod_s = mod[:nbp], mod[nbp:]

    pw = _prep_proj_weights(w["w_in"], w["w_q_up"], w["w_uk"], w["w_uv"], w["b_igate"], w["b_fgate"])
    chunk_p = ML_CHUNK if S % ML_CHUNK == 0 else S
    chunk_s = ML_CHUNK if T % ML_CHUNK == 0 else T

    (q, k, v, ckv_p, kpe_p, mq, mk, mv, g, cum, og_p) = _project(
        xp, mod_p, _rope_tables(jnp.arange(S)), chunk_p, w["g_pre_mix"], w["g_qlat"], w["g_kvlat"], pw, True)
    o_mla_p = _prompt_attention(q, k, v, nbp, S)
    h_ml_p, c_new_p, n_new_p, m_new_p = _mlstm(
        mq, mk, mv, g, cum, jnp.zeros((nbp, ML_HEADS, ML_V, ML_QK), F32), jnp.zeros((nbp, ML_HEADS, ML_QK), F32),
        jnp.zeros((nbp, ML_HEADS), F32), nbp, S)

    (qa, qr, ckv_s, kpe_s, mq, mk, mv, g, cum, og_s) = _project(
        xs, mod_s, _rope_tables(past_len + jnp.arange(T)), chunk_s, w["g_pre_mix"], w["g_qlat"], w["g_kvlat"],
        pw, False)
    o_mla_s = _sample_attention(qa, qr, ckv_s, kpe_s, cache_ckv, cache_kpe, layer, page_table,
                                w["w_uv"].transpose(1, 0, 2))
    h_ml_s, c_new_s, n_new_s, m_new_s = _mlstm(mq, mk, mv, g, cum, st_c, st_n, st_m, nbs, T)

    w_out_b = w["w_out"].astype(BF16)
    wr_t = w["w_router"].T
    wr_hi = wr_t.astype(BF16)
    wr_lo = (wr_t - wr_hi.astype(F32)).astype(BF16)
    margs = (w["g_heads"], w_out_b, w["g_post_mix"], w["g_pre_ffn"], wr_hi, wr_lo, w["b_router"])
    x1_p, h2_p, idx_p, gate_p = _merge(xp, mod_p, o_mla_p, h_ml_p, og_p, *margs, True)
    x1_s, h2_s, idx_s, gate_s = _merge(xs, mod_s, o_mla_s, h_ml_s, og_s, *margs, False)

    n_p, n_s = nbp * S, nbs * T
    h2 = jnp.concatenate([h2_p, h2_s], axis=0)
    top_idx = jnp.concatenate([idx_p[:TOP_K], idx_s[:TOP_K]], axis=1).T
    gates = jnp.concatenate([gate_p[:TOP_K], gate_s[:TOP_K]], axis=1).T
    m_rows = (n_p + n_s) * TOP_K
    n_blocks = -(-m_rows // MOE_ROWS) + N_EXPERTS
    dest, tok_sorted, shift, block_e, n_used = _route(top_idx, n_blocks)
    yb = _moe_experts(h2, block_e, tok_sorted, shift, n_used, w["w_gate_up"], w["b_gate_up"], w["w_down"],
                      w["b_down"])
    y_p = _combine(yb, dest, 0, x1_p, mod_p, gates[:n_p], w["g_post_ffn"], True)
    y_s = _combine(yb, dest, n_p * TOP_K, x1_s, mod_s, gates[n_p:], w["g_post_ffn"], False)

    new_p = (ckv_p.reshape(nbp, S, KV_LORA), kpe_p.reshape(nbp, S, MLA_ROPE), c_new_p.astype(dt),
             n_new_p.astype(dt), m_new_p.astype(dt))
    new_s = (ckv_s.reshape(nbs, T, KV_LORA), kpe_s.reshape(nbs, T, MLA_ROPE), c_new_s.astype(dt),
             n_new_s.astype(dt), m_new_s.astype(dt))
    return y_p, y_s, new_p, new_s


def kernel(x_prompt, x_sample, cache_ckv, cache_kpe, page_table, state_C, state_n, state_m, c_prompt, c_sample,
           w_ada, b_ada, g_pre_mix, g_post_mix, g_pre_ffn, g_post_ffn, w_in, g_qlat, w_q_up, g_kvlat, w_uk, w_uv,
           b_igate, b_fgate, g_heads, w_out, w_router, b_router, w_gate_up, b_gate_up, w_down, b_down):
    weights = dict(w_ada=w_ada, b_ada=b_ada, g_pre_mix=g_pre_mix, g_post_mix=g_post_mix, g_pre_ffn=g_pre_ffn,
                   g_post_ffn=g_post_ffn, w_in=w_in, g_qlat=g_qlat, w_q_up=w_q_up, g_kvlat=g_kvlat, w_uk=w_uk,
                   w_uv=w_uv, b_igate=b_igate, b_fgate=b_fgate, g_heads=g_heads, w_out=w_out, w_router=w_router,
                   b_router=b_router, w_gate_up=w_gate_up, b_gate_up=b_gate_up, w_down=w_down, b_down=b_down)
    depth = w_ada.shape[0]
    xp, xs = x_prompt, x_sample
    news_p, news_s = [], []
    for l in range(depth):
        wl = {name: val[l] for name, val in weights.items()}
        xp, xs, new_p, new_s = _layer(xp, xs, cache_ckv, cache_kpe, l, page_table, state_C[l], state_n[l],
                                      state_m[l], c_prompt, c_sample, wl)
        news_p.append(new_p)
        news_s.append(new_s)
    stack = lambda items, j: jnp.stack([it[j] for it in items])
    return ((xp, xs) + tuple(stack(news_p, j) for j in range(5)) + tuple(stack(news_s, j) for j in range(5)))
```

```python
import functools
import math

import jax
import jax.numpy as jnp
from jax import lax
from jax.experimental import pallas as pl
from jax.experimental.pallas import tpu as pltpu

F32 = jnp.float32
BF16 = jnp.bfloat16

HEAD_OUT = 128
MLA_HEADS = 4
ML_HEADS = 4
MLA_NOPE = 128
MLA_ROPE = 64
Q_LORA = 384
KV_LORA = 256
ROPE_BASE = 10000.0
SM_SCALE = (MLA_NOPE + MLA_ROPE) ** -0.5
ML_QK = 64
ML_V = 128
ML_CHUNK = 64
IGATE_CAP = 15.0
N_EXPERTS = 32
TOP_K = 4
SWIGLU_LIMIT = 7.0
SWIGLU_ALPHA = 1.702
EPS = 1e-6
PAGE_SIZE = 128

LANES = 128
LOG2E = 1.4426950408889634
QK_SCALE = SM_SCALE * LOG2E
NEG_INF = float("-inf")

TOKEN_TILE = 512
ATTN_BLOCK = 512
PAGES_PER_STEP = 64
PAGE_SLOTS = 3
ATTN_SUB = 1024
MOE_ROWS = 256
MOE_LOOKAHEAD = 2
COMBINE_TILE = 128
VMEM_LIMIT = 56 * 1024 * 1024

NT_DIMS = (((1,), (1,)), ((), ()))
TN_DIMS = (((0,), (0,)), ((), ()))


def _cparams(sem):
    return pltpu.CompilerParams(dimension_semantics=sem, vmem_limit_bytes=VMEM_LIMIT)


def _rms(x, g):
    return x * lax.rsqrt(jnp.mean(x * x, axis=-1, keepdims=True) + EPS) * g


def _dot(a, b):
    return jnp.dot(a, b, preferred_element_type=F32)


ROW_TILE = 8


def _store_row_tiles(ref, val):
    rows = val.shape[0]
    for c in range(ROW_TILE):
        ref[pl.ds(c, rows, stride=ROW_TILE), :] = val[:, c * LANES:(c + 1) * LANES]


def _load_row_tiles(ref, rows):
    return jnp.concatenate([ref[pl.ds(c, rows, stride=ROW_TILE), :] for c in range(ROW_TILE)], axis=1)


def _mod_kernel(c_ref, w_ref, b_ref, o_ref):
    c = c_ref[...]
    s = c * jax.nn.sigmoid(c)
    o_ref[...] = _dot(s.astype(BF16), w_ref[...].astype(BF16)) + b_ref[...]


def _modulation(c, w_ada, b_ada):
    nseq, d = c.shape
    n_out = w_ada.shape[1]
    return pl.pallas_call(
        _mod_kernel,
        out_shape=jax.ShapeDtypeStruct((nseq, n_out), F32),
        grid=(n_out // d,),
        in_specs=[pl.BlockSpec((nseq, d), lambda j: (0, 0)),
                  pl.BlockSpec((d, d), lambda j: (0, j)),
                  pl.BlockSpec((1, d), lambda j: (0, j))],
        out_specs=pl.BlockSpec((nseq, d), lambda j: (0, j)),
        compiler_params=_cparams(("arbitrary",)),
        name="modulation",
    )(c, w_ada, b_ada.reshape(1, n_out))


_C_QLAT = 0
_C_CKV = _C_QLAT + Q_LORA
_C_KPE = _C_CKV + KV_LORA
_C_KROT = _C_KPE + LANES
_C_MQ = _C_KROT + LANES
_C_MK = _C_MQ + ML_HEADS * ML_QK
_C_MV = _C_MK + ML_HEADS * ML_QK
_C_OG = _C_MV + ML_HEADS * ML_V
_C_GATE = _C_OG + ML_HEADS * ML_V
_C_END = _C_GATE + LANES


def _rot_half_cols(w):
    half = w.shape[-1] // 2
    return jnp.concatenate([-w[..., half:], w[..., :half]], axis=-1)


def _prep_proj_weights(w_in, w_q_up, w_uk, w_uv, b_igate, b_fgate):
    d = w_in.shape[0]
    o = 0
    parts = {}
    for name, size in (("qlat", Q_LORA), ("ckv", KV_LORA), ("kpe", MLA_ROPE), ("mq", ML_HEADS * ML_QK),
                       ("mk", ML_HEADS * ML_QK), ("mv", ML_HEADS * ML_V), ("ig", ML_HEADS), ("fg", ML_HEADS),
                       ("og", ML_HEADS * ML_V)):
        parts[name] = w_in[:, o:o + size]
        o += size
    z64 = jnp.zeros((d, LANES - MLA_ROPE), F32)
    zg = jnp.zeros((d, LANES - 2 * ML_HEADS), F32)
    w_in_r = jnp.concatenate(
        [parts["qlat"], parts["ckv"], parts["kpe"], z64, _rot_half_cols(parts["kpe"]), z64,
         parts["mq"], parts["mk"], parts["mv"], parts["og"], parts["ig"], parts["fg"], zg], axis=1).astype(BF16)
    b_gate = jnp.concatenate([b_igate, b_fgate, jnp.zeros((LANES - 2 * ML_HEADS,), F32)]).reshape(1, LANES)
    w_nope = w_q_up[:, :, :MLA_NOPE].reshape(Q_LORA, MLA_HEADS * MLA_NOPE)
    w_rope = w_q_up[:, :, MLA_NOPE:]
    zr = jnp.zeros((Q_LORA, MLA_HEADS, LANES - MLA_ROPE), F32)
    w_qr = jnp.concatenate([w_rope, zr], axis=-1).reshape(Q_LORA, MLA_HEADS * LANES)
    w_qt = jnp.concatenate([_rot_half_cols(w_rope), zr], axis=-1).reshape(Q_LORA, MLA_HEADS * LANES)
    w_uk2 = w_uk.reshape(KV_LORA, MLA_HEADS * MLA_NOPE).astype(BF16)
    w_uv2 = w_uv.reshape(KV_LORA, MLA_HEADS * HEAD_OUT).astype(BF16)
    w_ukT = jnp.transpose(w_uk, (1, 2, 0)).astype(BF16)
    return dict(w_in=w_in_r, b_gate=b_gate, w_qn=w_nope.astype(BF16), w_qr=w_qr.astype(BF16),
                w_qt=w_qt.astype(BF16), w_uk=w_uk2, w_uv=w_uv2, w_ukT=w_ukT)


def _rope_tables(pos):
    half = MLA_ROPE // 2
    inv_freq = ROPE_BASE ** (-jnp.arange(half, dtype=F32) / half)
    ang = pos.astype(F32)[:, None] * inv_freq[None, :]
    z = jnp.zeros((pos.shape[0], LANES - MLA_ROPE), F32)
    cos = jnp.concatenate([jnp.cos(ang), jnp.cos(ang), z], axis=1)
    sin = jnp.concatenate([jnp.sin(ang), jnp.sin(ang), z], axis=1)
    return cos, sin


def _chunk_tril(tm, chunk):
    r = jnp.arange(tm)
    same = (r[:, None] // chunk) == (r[None, :] // chunk)
    return (same & (r[None, :] <= r[:, None])).astype(BF16)


def _split3(x):
    hi = x.astype(BF16)
    r1 = x - hi.astype(F32)
    mid = r1.astype(BF16)
    lo = (r1 - mid.astype(F32)).astype(BF16)
    return hi, mid, lo


def _proj_kernel(*refs, is_prompt):
    (x_ref, mod_ref, gpre_ref, win_ref, bg_ref, gq_ref, wqn_ref, wqr_ref, wqt_ref, gkv_ref,
     cos_ref, sin_ref, tril_ref) = refs[:13]
    if is_prompt:
        wuk_ref, wuv_ref = refs[13:15]
        q_ref, k_ref, v_ref, ckv_ref, kpe_ref, mq_ref, mk_ref, mv_ref, g_ref, cum_ref, og_ref = refs[15:]
    else:
        wukT_ref = refs[13]
        qa_ref, qr_ref, ckv_ref, kpe_ref, mq_ref, mk_ref, mv_ref, g_ref, cum_ref, og_ref = refs[14:]

    x = x_ref[...]
    mod = mod_ref[...]
    nb, t, d = x.shape
    h = _rms(x, gpre_ref[...]) * (1.0 + mod[:, 1:2, :]) + mod[:, 0:1, :]
    hb = h.reshape(nb * t, d).astype(BF16)

    def zcols(a, b):
        return _dot(hb, win_ref[:, a:b])

    cos = cos_ref[...]
    sin = sin_ref[...]

    qn = _rms(zcols(_C_QLAT, _C_CKV), gq_ref[...]).astype(BF16)
    q_nope = _dot(qn, wqn_ref[...]) * QK_SCALE
    q_r = _dot(qn, wqr_ref[...])
    q_t = _dot(qn, wqt_ref[...])
    ckv = _rms(zcols(_C_CKV, _C_KPE), gkv_ref[...])
    ckv_ref[...] = ckv
    kpe = zcols(_C_KPE, _C_KROT) * cos + zcols(_C_KROT, _C_MQ) * sin
    kpe_ref[...] = kpe[:, :MLA_ROPE]
    for hd in range(MLA_HEADS):
        sl = slice(hd * LANES, (hd + 1) * LANES)
        roped = (q_r[:, sl] * cos + q_t[:, sl] * sin) * QK_SCALE
        if is_prompt:
            q_ref[:, 2 * hd * LANES:(2 * hd + 1) * LANES] = q_nope[:, sl].astype(BF16)
            q_ref[:, (2 * hd + 1) * LANES:(2 * hd + 2) * LANES] = roped.astype(BF16)
        else:
            qa_ref[:, hd * KV_LORA:(hd + 1) * KV_LORA] = _dot(q_nope[:, sl].astype(BF16), wukT_ref[hd])
            qr_ref[:, sl] = roped
    if is_prompt:
        cb = ckv.astype(BF16)
        k_nope = _dot(cb, wuk_ref[...])
        kpe_b = kpe.astype(BF16)
        for hd in range(MLA_HEADS):
            k_ref[:, 2 * hd * LANES:(2 * hd + 1) * LANES] = k_nope[:, hd * LANES:(hd + 1) * LANES].astype(BF16)
            k_ref[:, (2 * hd + 1) * LANES:(2 * hd + 2) * LANES] = kpe_b
        v_ref[...] = _dot(cb, wuv_ref[...]).astype(BF16)

    mq_ref[...] = zcols(_C_MQ, _C_MK) * (ML_QK ** -0.5)
    mk_ref[...] = zcols(_C_MK, _C_MV)
    mv_ref[...] = zcols(_C_MV, _C_OG)
    og_ref[...] = jax.nn.sigmoid(zcols(_C_OG, _C_GATE))
    zg = zcols(_C_GATE, _C_END) + bg_ref[...]
    ig = IGATE_CAP * jnp.tanh(zg / IGATE_CAP)
    lf = -(jnp.maximum(-zg, 0.0) + jnp.log1p(jnp.exp(-jnp.abs(zg))))
    lane = lax.broadcasted_iota(jnp.int32, zg.shape, 1)
    g = jnp.where(lane < ML_HEADS, ig, lf)
    g_ref[...] = g
    tril = tril_ref[...]
    hi, mid, lo = _split3(g)
    cum_ref[...] = _dot(tril, hi) + _dot(tril, mid) + _dot(tril, lo)


def _project(x, mod, pos_tab, chunk, g_pre, g_qlat, g_kvlat, pw, is_prompt):
    nseq, T, d = x.shape
    n = nseq * T
    if is_prompt:
        nb, t = 1, min(TOKEN_TILE, T)
    else:
        t = T
        nb = max(1, min(nseq, TOKEN_TILE // T))
    assert T % t == 0 and nseq % nb == 0
    tm = nb * t
    tiles_per_seq = T // t
    grid = (n // tm,)
    cos, sin = pos_tab
    if is_prompt:
        x_map = lambda i: (i // tiles_per_seq, i % tiles_per_seq, 0)
        mod_map = lambda i: (i // tiles_per_seq, 0, 0)
        tab_map = lambda i: (i % tiles_per_seq, 0)
    else:
        x_map = lambda i: (i, 0, 0)
        mod_map = lambda i: (i, 0, 0)
        tab_map = lambda i: (0, 0)
        cos = jnp.tile(cos, (nb, 1))
        sin = jnp.tile(sin, (nb, 1))
    const = lambda i: (0, 0)
    row = lambda i: (i, 0)
    tril = _chunk_tril(tm, chunk)
    in_specs = [
        pl.BlockSpec((nb, t, d), x_map),
        pl.BlockSpec((nb, 6, d), mod_map),
        pl.BlockSpec((1, d), const),
        pl.BlockSpec((d, _C_END), const),
        pl.BlockSpec((1, LANES), const),
        pl.BlockSpec((1, Q_LORA), const),
        pl.BlockSpec((Q_LORA, MLA_HEADS * MLA_NOPE), const),
        pl.BlockSpec((Q_LORA, MLA_HEADS * LANES), const),
        pl.BlockSpec((Q_LORA, MLA_HEADS * LANES), const),
        pl.BlockSpec((1, KV_LORA), const),
        pl.BlockSpec((tm, LANES), tab_map),
        pl.BlockSpec((tm, LANES), tab_map),
        pl.BlockSpec((tm, tm), const),
    ]
    args = [x, mod, g_pre.reshape(1, d), pw["w_in"], pw["b_gate"], g_qlat.reshape(1, Q_LORA), pw["w_qn"],
            pw["w_qr"], pw["w_qt"], g_kvlat.reshape(1, KV_LORA), cos, sin, tril]

    def out(cols, dtype=F32):
        return jax.ShapeDtypeStruct((n, cols), dtype), pl.BlockSpec((tm, cols), row)

    if is_prompt:
        in_specs += [pl.BlockSpec((KV_LORA, MLA_HEADS * MLA_NOPE), const),
                     pl.BlockSpec((KV_LORA, MLA_HEADS * HEAD_OUT), const)]
        args += [pw["w_uk"], pw["w_uv"]]
        outs = [out(MLA_HEADS * 2 * LANES, BF16), out(MLA_HEADS * 2 * LANES, BF16), out(MLA_HEADS * HEAD_OUT, BF16)]
    else:
        in_specs += [pl.BlockSpec((MLA_HEADS, MLA_NOPE, KV_LORA), lambda i: (0, 0, 0))]
        args += [pw["w_ukT"]]
        outs = [out(MLA_HEADS * KV_LORA), out(MLA_HEADS * LANES)]
    outs += [out(KV_LORA), out(MLA_ROPE), out(ML_HEADS * ML_QK), out(ML_HEADS * ML_QK), out(ML_HEADS * ML_V),
             out(LANES), out(LANES), out(ML_HEADS * ML_V)]
    return pl.pallas_call(
        functools.partial(_proj_kernel, is_prompt=is_prompt),
        out_shape=[o[0] for o in outs],
        grid=grid,
        in_specs=in_specs,
        out_specs=[o[1] for o in outs],
        compiler_params=_cparams(("arbitrary",)),
        name="proj_prompt" if is_prompt else "proj_sample",
    )(*args)


def _attn_kernel(qi_ref, ki_ref, q_ref, k_ref, v_ref, o_ref, m_ref, l_ref, acc_ref):
    p = pl.program_id(1)
    qi = qi_ref[p]
    ki = ki_ref[p]
    tq = q_ref.shape[0]
    tk = k_ref.shape[0]

    @pl.when(ki == 0)
    def _():
        m_ref[...] = jnp.full(m_ref.shape, NEG_INF, F32)
        l_ref[...] = jnp.zeros(l_ref.shape, F32)
        acc_ref[...] = jnp.zeros(acc_ref.shape, F32)

    def update(masked):
        for hd in range(MLA_HEADS):
            qh = q_ref[:, hd * 2 * LANES:(hd + 1) * 2 * LANES]
            kh = k_ref[:, hd * 2 * LANES:(hd + 1) * 2 * LANES]
            s = lax.dot_general(qh, kh, NT_DIMS, preferred_element_type=F32)
            if masked:
                row = lax.broadcasted_iota(jnp.int32, s.shape, 0)
                col = lax.broadcasted_iota(jnp.int32, s.shape, 1)
                s = jnp.where(col <= row, s, NEG_INF)
            m_prev = m_ref[hd]
            m_next = jnp.maximum(m_prev, jnp.max(s, axis=1, keepdims=True))
            alpha = jnp.exp2(m_prev - m_next)
            pr = jnp.exp2(s - jnp.concatenate([m_next] * (tk // LANES), axis=1))
            l_ref[hd] = alpha * l_ref[hd] + jnp.sum(pr, axis=1, keepdims=True)
            acc_ref[hd] = alpha * acc_ref[hd] + _dot(pr.astype(BF16), v_ref[:, hd * HEAD_OUT:(hd + 1) * HEAD_OUT])
            m_ref[hd] = m_next

    @pl.when(ki < qi)
    def _():
        update(False)

    @pl.when(ki == qi)
    def _():
        update(True)
        for hd in range(MLA_HEADS):
            o_ref[:, hd * HEAD_OUT:(hd + 1) * HEAD_OUT] = acc_ref[hd] / l_ref[hd]


def _prompt_attention(q, k, v, nseq, S):
    blk = min(ATTN_BLOCK, S)
    nq = S // blk
    pairs = [(i, j) for i in range(nq) for j in range(i + 1)]
    qi_tab = jnp.asarray([p[0] for p in pairs], jnp.int32)
    ki_tab = jnp.asarray([p[1] for p in pairs], jnp.int32)
    wq = MLA_HEADS * 2 * LANES
    wv = MLA_HEADS * HEAD_OUT
    return pl.pallas_call(
        _attn_kernel,
        out_shape=jax.ShapeDtypeStruct((nseq * S, wv), F32),
        grid_spec=pltpu.PrefetchScalarGridSpec(
            num_scalar_prefetch=2,
            grid=(nseq, len(pairs)),
            in_specs=[pl.BlockSpec((blk, wq), lambda b, p, qi, ki: (b * nq + qi[p], 0)),
                      pl.BlockSpec((blk, wq), lambda b, p, qi, ki: (b * nq + ki[p], 0)),
                      pl.BlockSpec((blk, wv), lambda b, p, qi, ki: (b * nq + ki[p], 0))],
            out_specs=pl.BlockSpec((blk, wv), lambda b, p, qi, ki: (b * nq + qi[p], 0)),
            scratch_shapes=[pltpu.VMEM((MLA_HEADS, blk, LANES), F32),
                            pltpu.VMEM((MLA_HEADS, blk, LANES), F32),
                            pltpu.VMEM((MLA_HEADS, blk, HEAD_OUT), F32)]),
        compiler_params=_cparams(("arbitrary", "arbitrary")),
        name="prompt_attention",
    )(qi_tab, ki_tab, q, k, v)


def _sattn_kernel(pt_ref, qa_ref, qr_ref, ckvn_ref, kpen_ref, wuv_ref, cckv_hbm, ckpe_hbm, o_ref,
                  ckv_buf, kpe_buf, sem, kcb, q_s, qr_s, kn_s, pn_s, m_ref, l_ref, acc_ref,
                  *, layer, npg, nchunks, sub):
    b = pl.program_id(0)
    c = pl.program_id(1)
    step = b * nchunks + c
    nsteps = pl.num_programs(0) * nchunks
    nslots = ckv_buf.shape[0]
    slot = step % nslots
    T = qa_ref.shape[0]
    rows = MLA_HEADS * T
    npos = npg * PAGE_SIZE

    def page_copies(page_of, slt):
        cps = []
        for j in range(npg):
            pg = page_of(j)
            dst = pl.ds(j * PAGE_SIZE, PAGE_SIZE)
            cps.append(pltpu.make_async_copy(cckv_hbm.at[layer, pg], ckv_buf.at[slt, dst, :], sem.at[0, slt]))
            cps.append(pltpu.make_async_copy(ckpe_hbm.at[layer, pg], kpe_buf.at[slt, :, dst], sem.at[1, slt]))
        return cps

    def fetch(stp, slt):
        base = jnp.minimum(stp, nsteps - 1) * npg
        for cp in page_copies(lambda j: pt_ref[base + j], slt):
            cp.start()

    def wait_pages(slt):
        for cp in page_copies(lambda j: 0, slt):
            cp.wait()

    @pl.when(step == 0)
    def _():
        for k in range(nslots - 1):
            fetch(k, k)

    @pl.when(c == 0)
    def _():
        m_ref[...] = jnp.full(m_ref.shape, NEG_INF, F32)
        l_ref[...] = jnp.zeros(l_ref.shape, F32)
        acc_ref[...] = jnp.zeros(acc_ref.shape, F32)
        qa = qa_ref[...]
        qr = qr_ref[...]
        q_s[...] = jnp.concatenate([qa[:, hd * KV_LORA:(hd + 1) * KV_LORA] for hd in range(MLA_HEADS)],
                                   axis=0).astype(BF16)
        qr_s[...] = jnp.concatenate([qr[:, hd * LANES:(hd + 1) * LANES] for hd in range(MLA_HEADS)],
                                    axis=0).astype(BF16)

    wait_pages(slot)
    fetch(step + nslots - 1, (step + nslots - 1) % nslots)

    q = q_s[...]
    qr = qr_s[:, :MLA_ROPE]

    def softmax_update(s, pv):
        m_prev = m_ref[...]
        m_next = jnp.maximum(m_prev, jnp.max(s, axis=1, keepdims=True))
        alpha = jnp.exp2(m_prev - m_next)
        pr = jnp.exp2(s - jnp.concatenate([m_next] * (s.shape[1] // LANES), axis=1))
        l_ref[...] = alpha * l_ref[...] + jnp.sum(pr, axis=1, keepdims=True)
        acc_ref[...] = jnp.concatenate([alpha] * (KV_LORA // LANES), axis=1) * acc_ref[...] + pv(pr.astype(BF16))
        m_ref[...] = m_next

    s_parts = []
    for sb in range(npos // sub):
        sl = slice(sb * sub, (sb + 1) * sub)
        kc = ckv_buf[slot, sl, :].astype(BF16)
        kcb[sl, :] = kc
        kp = kpe_buf[slot, :, sl].astype(BF16)
        s_parts.append(lax.dot_general(q, kc, NT_DIMS, preferred_element_type=F32) + _dot(qr, kp))

    def pv_chunk(p):
        out = _dot(p[:, 0:sub], kcb[0:sub, :])
        for sb in range(1, npos // sub):
            out = out + _dot(p[:, sb * sub:(sb + 1) * sub], kcb[sb * sub:(sb + 1) * sub, :])
        return out

    softmax_update(jnp.concatenate(s_parts, axis=1), pv_chunk)

    @pl.when(c == nchunks - 1)
    def _():
        kn_s[...] = jnp.zeros(kn_s.shape, F32)
        pn_s[...] = jnp.zeros(pn_s.shape, F32)
        kn_s[0:T, :] = ckvn_ref[...]
        pn_s[0:T, :] = kpen_ref[...]
        row = lax.broadcasted_iota(jnp.int32, (rows, LANES), 0)
        col = lax.broadcasted_iota(jnp.int32, (rows, LANES), 1)
        kn = kn_s[...].astype(BF16)
        s_new = (lax.dot_general(q, kn, NT_DIMS, preferred_element_type=F32)
                 + lax.dot_general(qr, pn_s[...].astype(BF16), NT_DIMS, preferred_element_type=F32))
        softmax_update(jnp.where(col <= row % T, s_new, NEG_INF), lambda p: _dot(p, kn))
        o_lat = acc_ref[...] / jnp.concatenate([l_ref[...]] * (KV_LORA // LANES), axis=1)
        for hd in range(MLA_HEADS):
            o_ref[:, hd * HEAD_OUT:(hd + 1) * HEAD_OUT] = _dot(o_lat[hd * T:(hd + 1) * T, :], wuv_ref[hd])

    @pl.when(step == nsteps - 1)
    def _():
        for k in range(1, nslots):
            wait_pages((step + k) % nslots)


def _sample_attention(qa, qr, ckv_new, kpe_new, cache_ckv, cache_kpe, layer, page_table, w_uv):
    nseq, n_pages = page_table.shape
    T = qa.shape[0] // nseq
    npg = min(PAGES_PER_STEP, n_pages)
    assert n_pages % npg == 0
    nchunks = n_pages // npg
    sub = min(ATTN_SUB, npg * PAGE_SIZE)
    rows = MLA_HEADS * T
    seq = lambda b, c, pt: (b, 0)
    kern = functools.partial(_sattn_kernel, layer=layer, npg=npg, nchunks=nchunks, sub=sub)
    return pl.pallas_call(
        kern,
        out_shape=jax.ShapeDtypeStruct((nseq * T, MLA_HEADS * HEAD_OUT), F32),
        grid_spec=pltpu.PrefetchScalarGridSpec(
            num_scalar_prefetch=1,
            grid=(nseq, nchunks),
            in_specs=[pl.BlockSpec((T, MLA_HEADS * KV_LORA), seq),
                      pl.BlockSpec((T, MLA_HEADS * LANES), seq),
                      pl.BlockSpec((T, KV_LORA), seq),
                      pl.BlockSpec((T, MLA_ROPE), seq),
                      pl.BlockSpec((MLA_HEADS, KV_LORA, HEAD_OUT), lambda b, c, pt: (0, 0, 0)),
                      pl.BlockSpec(memory_space=pl.ANY),
                      pl.BlockSpec(memory_space=pl.ANY)],
            out_specs=pl.BlockSpec((T, MLA_HEADS * HEAD_OUT), seq),
            scratch_shapes=[pltpu.VMEM((PAGE_SLOTS, npg * PAGE_SIZE, KV_LORA), F32),
                            pltpu.VMEM((PAGE_SLOTS, MLA_ROPE, npg * PAGE_SIZE), F32),
                            pltpu.SemaphoreType.DMA((2, PAGE_SLOTS)),
                            pltpu.VMEM((npg * PAGE_SIZE, KV_LORA), BF16),
                            pltpu.VMEM((rows, KV_LORA), BF16),
                            pltpu.VMEM((rows, LANES), BF16),
                            pltpu.VMEM((LANES, KV_LORA), F32),
                            pltpu.VMEM((LANES, MLA_ROPE), F32),
                            pltpu.VMEM((rows, LANES), F32),
                            pltpu.VMEM((rows, LANES), F32),
                            pltpu.VMEM((rows, KV_LORA), F32)]),
        compiler_params=_cparams(("arbitrary", "arbitrary")),
        name="sample_attention",
    )(page_table.reshape(-1), qa, qr, ckv_new, kpe_new, w_uv, cache_ckv, jnp.swapaxes(cache_kpe, 2, 3))


ML_HK = ML_HEADS * ML_QK


def _rows_per_head(vals, rows):
    return jnp.concatenate([jnp.broadcast_to(v, (rows, v.shape[1])) for v in vals], axis=0)


def _head_block_mask(nrows, row_blk, ncols, col_blk):
    r = lax.broadcasted_iota(jnp.int32, (nrows, ncols), 0) // row_blk
    c = lax.broadcasted_iota(jnp.int32, (nrows, ncols), 1) // col_blk
    return r == c


def _mlstm_kernel(mq_ref, mk_ref, mv_ref, g_ref, cum_ref, s0_ref, n0_ref, m0_ref, h_ref, st_ref, n_ref, m_ref,
                  *, unroll_seqs, dot_dtype):
    ci = pl.program_id(1)
    nbs, L = mq_ref.shape[0], mq_ref.shape[1]
    H = ML_HEADS
    R = H * L

    @pl.when(ci == 0)
    def _():
        st_ref[...] = s0_ref[...]
        n_ref[...] = n0_ref[...]
        m_ref[...] = m0_ref[...]

    trow = lax.broadcasted_iota(jnp.int32, (R, L), 0) % L
    tcol = lax.broadcasted_iota(jnp.int32, (R, L), 1)
    causal = tcol <= trow
    q_mask = _head_block_mask(R, L, ML_HK, ML_QK)
    s_mask = _head_block_mask(R, L, R, L)
    k_mask = _head_block_mask(ML_HK, ML_QK, R, L)

    def one_seq(i):
        g = g_ref[i]
        cum = cum_ref[i]
        g_t = g.T
        cum_t = cum.T
        mq = mq_ref[i]
        mk = mk_ref[i]
        mv = mv_ref[i]
        mprev = m_ref[i]
        ig_col = jnp.concatenate([g[:, h:h + 1] for h in range(H)], axis=0)
        b_col = jnp.concatenate([cum[:, H + h:H + h + 1] for h in range(H)], axis=0)
        ig_row = _rows_per_head([g_t[h:h + 1, :] for h in range(H)], L)
        b_row = _rows_per_head([cum_t[H + h:H + h + 1, :] for h in range(H)], L)
        m_prev = _rows_per_head([mprev[h:h + 1, 0:1] for h in range(H)], L)
        b_last = _rows_per_head([cum[L - 1:L, H + h:H + h + 1] for h in range(H)], L)

        log_d = jnp.where(causal, b_col - b_row + ig_row, NEG_INF)
        log_inter = b_col + m_prev
        m_t = jnp.maximum(log_inter, jnp.max(log_d, axis=1, keepdims=True))
        d = jnp.exp(log_d - m_t)
        inter = jnp.exp(log_inter - m_t)

        q_blk = jnp.where(q_mask, jnp.concatenate([mq] * H, axis=0), 0.0)
        k_t = mk.T
        s = _dot(q_blk.astype(dot_dtype), k_t.astype(dot_dtype)) * d
        s_bd = jnp.where(s_mask, jnp.concatenate([s] * H, axis=1), 0.0)
        v_stack = jnp.concatenate([mv[:, h * ML_V:(h + 1) * ML_V] for h in range(H)], axis=0)
        st = st_ref[i]
        n_row = n_ref[i]
        num = (_dot((inter * q_blk).astype(dot_dtype), st.astype(dot_dtype))
               + _dot(s_bd.astype(dot_dtype), v_stack.astype(dot_dtype)))
        den = inter * jnp.sum(q_blk * n_row, axis=1, keepdims=True) + jnp.sum(s, axis=1, keepdims=True)
        h_all = num / jnp.maximum(jnp.abs(den), jnp.exp(-m_t))
        for h in range(H):
            h_ref[i, :, h * ML_V:(h + 1) * ML_V] = h_all[h * L:(h + 1) * L, :]

        m_last = [m_t[(h + 1) * L - 1:(h + 1) * L, :] for h in range(H)]
        m_new = _rows_per_head(m_last, L)
        w_col = jnp.exp(b_last - b_col + ig_col - m_new)
        decay = jnp.exp(b_last + m_prev - m_new)
        kt_blk = jnp.where(k_mask, jnp.concatenate([k_t] * H, axis=1), 0.0)
        decay_k = _rows_per_head([decay[h * L:h * L + 1, :] for h in range(H)], ML_QK)
        st_ref[i] = decay_k * st + _dot(kt_blk.astype(dot_dtype), (w_col * v_stack).astype(dot_dtype))

        def widen(col):
            return jnp.concatenate([jnp.broadcast_to(col[h * L:(h + 1) * L, :], (L, ML_QK)) for h in range(H)],
                                   axis=1)
        n_ref[i] = widen(decay)[0:1, :] * n_row + jnp.sum(widen(w_col) * mk, axis=0, keepdims=True)
        m_ref[i] = jnp.concatenate([jnp.broadcast_to(ml, (1, LANES)) for ml in m_last], axis=0)

    if unroll_seqs:
        for i in range(nbs):
            one_seq(i)
    else:
        def body(i, carry):
            one_seq(i)
            return carry
        lax.fori_loop(0, nbs, body, 0)


def _mlstm(mq, mk, mv, g, cum, c0, n0, m0, nseq, T):
    L = ML_CHUNK if T % ML_CHUNK == 0 else T
    nc = T // L
    nbs = nseq if nseq <= 2 else 16
    assert nseq % nbs == 0
    r3 = lambda a: a.reshape(nseq, T, a.shape[-1])
    tok = lambda gi, ci: (gi, ci, 0)
    st3 = lambda gi, ci: (gi, 0, 0)
    ct0 = jnp.swapaxes(c0.astype(F32), 2, 3).reshape(nseq, ML_HK, ML_V)
    n0r = n0.astype(F32).reshape(nseq, 1, ML_HK)
    m0b = jnp.broadcast_to(m0.astype(F32)[:, :, None], (nseq, ML_HEADS, LANES))
    kern = functools.partial(_mlstm_kernel, unroll_seqs=nbs <= 2, dot_dtype=BF16 if L >= 16 else F32)
    h, st, n, m = pl.pallas_call(
        kern,
        out_shape=[jax.ShapeDtypeStruct((nseq, T, ML_HEADS * ML_V), F32),
                   jax.ShapeDtypeStruct((nseq, ML_HK, ML_V), F32),
                   jax.ShapeDtypeStruct((nseq, 1, ML_HK), F32),
                   jax.ShapeDtypeStruct((nseq, ML_HEADS, LANES), F32)],
        grid=(nseq // nbs, nc),
        in_specs=[pl.BlockSpec((nbs, L, ML_HK), tok),
                  pl.BlockSpec((nbs, L, ML_HK), tok),
                  pl.BlockSpec((nbs, L, ML_HEADS * ML_V), tok),
                  pl.BlockSpec((nbs, L, LANES), tok),
                  pl.BlockSpec((nbs, L, LANES), tok),
                  pl.BlockSpec((nbs, ML_HK, ML_V), st3),
                  pl.BlockSpec((nbs, 1, ML_HK), st3),
                  pl.BlockSpec((nbs, ML_HEADS, LANES), st3)],
        out_specs=[pl.BlockSpec((nbs, L, ML_HEADS * ML_V), tok),
                   pl.BlockSpec((nbs, ML_HK, ML_V), st3),
                   pl.BlockSpec((nbs, 1, ML_HK), st3),
                   pl.BlockSpec((nbs, ML_HEADS, LANES), st3)],
        compiler_params=_cparams(("arbitrary", "arbitrary")),
        name="mlstm_prompt" if nseq <= 2 else "mlstm_sample",
    )(r3(mq), r3(mk), r3(mv), r3(g), r3(cum), ct0, n0r, m0b)
    c_new = jnp.swapaxes(st.reshape(nseq, ML_HEADS, ML_QK, ML_V), 2, 3)
    return h.reshape(nseq * T, ML_HEADS * ML_V), c_new, n.reshape(nseq, ML_HEADS, ML_QK), m[:, :, 0]


def _merge_kernel(x_ref, mod_ref, omla_ref, hml_ref, og_ref, gh_ref, wout_ref, gpm_ref, gpf_ref,
                  wrh_ref, wrl_ref, br_ref, x1_ref, h2_ref, idx_ref, gate_ref):
    nb, t, d = x_ref.shape
    tm = nb * t
    gh = gh_ref[...]
    parts = []
    for hd in range(MLA_HEADS + ML_HEADS):
        if hd < MLA_HEADS:
            xh = omla_ref[:, hd * HEAD_OUT:(hd + 1) * HEAD_OUT]
        else:
            j = hd - MLA_HEADS
            xh = hml_ref[:, j * HEAD_OUT:(j + 1) * HEAD_OUT]
        yh = _rms(xh, gh[:, hd * HEAD_OUT:(hd + 1) * HEAD_OUT])
        if hd >= MLA_HEADS:
            yh = yh * og_ref[:, j * HEAD_OUT:(j + 1) * HEAD_OUT]
        parts.append(yh.astype(BF16))
    merged = jnp.concatenate(parts, axis=1)
    mix = _dot(merged, wout_ref[...])
    mod = mod_ref[...]
    x1 = x_ref[...] + mod[:, 2:3, :] * _rms(mix, gpm_ref[...]).reshape(nb, t, d)
    x1_ref[...] = x1
    h2 = (_rms(x1, gpf_ref[...]) * (1.0 + mod[:, 4:5, :]) + mod[:, 3:4, :]).reshape(tm, d)
    _store_row_tiles(h2_ref, h2)
    hh = h2.astype(BF16)
    hl = (h2 - hh.astype(F32)).astype(BF16)
    wrh = wrh_ref[...]
    logits = (lax.dot_general(wrh, hh, NT_DIMS, preferred_element_type=F32)
              + lax.dot_general(wrh, hl, NT_DIMS, preferred_element_type=F32)
              + lax.dot_general(wrl_ref[...], hh, NT_DIMS, preferred_element_type=F32)) + br_ref[...]
    eidx = lax.broadcasted_iota(jnp.int32, logits.shape, 0)
    work = logits
    vals, idxs = [], []
    for _ in range(TOP_K):
        mx = jnp.max(work, axis=0, keepdims=True)
        sel = jnp.min(jnp.where(work == mx, eidx, N_EXPERTS), axis=0, keepdims=True)
        vals.append(mx)
        idxs.append(sel)
        work = jnp.where(eidx == sel, NEG_INF, work)
    ex = [jnp.exp(v - vals[0]) for v in vals]
    tot = ex[0] + ex[1] + ex[2] + ex[3]
    zi = jnp.zeros((8 - TOP_K, tm), jnp.int32)
    zf = jnp.zeros((8 - TOP_K, tm), F32)
    idx_ref[...] = jnp.concatenate(idxs + [zi], axis=0)
    gate_ref[...] = jnp.concatenate([e / tot for e in ex] + [zf], axis=0)


def _merge(x, mod, o_mla, h_ml, og, g_heads, w_out, g_post_mix, g_pre_ffn, wr_hi, wr_lo, b_router, is_prompt):
    nseq, T, d = x.shape
    n = nseq * T
    if is_prompt:
        nb, t = 1, min(TOKEN_TILE, T)
    else:
        t = T
        nb = max(1, min(nseq, TOKEN_TILE // T))
    tm = nb * t
    tiles_per_seq = T // t
    if is_prompt:
        x_map = lambda i: (i // tiles_per_seq, i % tiles_per_seq, 0)
        mod_map = lambda i: (i // tiles_per_seq, 0, 0)
    else:
        x_map = lambda i: (i, 0, 0)
        mod_map = lambda i: (i, 0, 0)
    const = lambda i: (0, 0)
    row = lambda i: (i, 0)
    col = lambda i: (0, i)
    wh = MLA_HEADS * HEAD_OUT
    return pl.pallas_call(
        _merge_kernel,
        out_shape=[jax.ShapeDtypeStruct((nseq, T, d), F32), jax.ShapeDtypeStruct((n * ROW_TILE, LANES), F32),
                   jax.ShapeDtypeStruct((8, n), jnp.int32), jax.ShapeDtypeStruct((8, n), F32)],
        grid=(n // tm,),
        in_specs=[pl.BlockSpec((nb, t, d), x_map),
                  pl.BlockSpec((nb, 6, d), mod_map),
                  pl.BlockSpec((tm, wh), row),
                  pl.BlockSpec((tm, wh), row),
                  pl.BlockSpec((tm, wh), row),
                  pl.BlockSpec((1, 2 * wh), const),
                  pl.BlockSpec((2 * wh, d), const),
                  pl.BlockSpec((1, d), const),
                  pl.BlockSpec((1, d), const),
                  pl.BlockSpec((N_EXPERTS, d), const),
                  pl.BlockSpec((N_EXPERTS, d), const),
                  pl.BlockSpec((N_EXPERTS, 1), const)],
        out_specs=[pl.BlockSpec((nb, t, d), x_map), pl.BlockSpec((tm * ROW_TILE, LANES), row),
                   pl.BlockSpec((8, tm), col), pl.BlockSpec((8, tm), col)],
        compiler_params=_cparams(("arbitrary",)),
        name="merge_prompt" if is_prompt else "merge_sample",
    )(x, mod, o_mla, h_ml, og, g_heads.reshape(1, 2 * wh), w_out, g_post_mix.reshape(1, d),
      g_pre_ffn.reshape(1, d), wr_hi, wr_lo, b_router.reshape(N_EXPERTS, 1))


def _moe_kernel(be_ref, tok_ref, sh_ref, nu_ref, h2_hbm, wgu_ref, bgu_ref, wd_ref, bd_ref, y_ref,
                xbuf, sem, wgu_s, wd_s):
    blk = pl.program_id(0)
    nblk = pl.num_programs(0)
    nslots = xbuf.shape[0]
    slot = blk % nslots
    tb = y_ref.shape[0] // ROW_TILE
    de = wd_ref.shape[0]
    n_used = nu_ref[0]

    def row_copy(tok8, r, slt):
        src = h2_hbm.at[pl.ds(pl.multiple_of(tok8, ROW_TILE), ROW_TILE), :]
        return pltpu.make_async_copy(src, xbuf.at[slt, pl.ds(r * ROW_TILE, ROW_TILE), :], sem.at[slt])

    def rows_base(bk):
        return bk * tb + sh_ref[be_ref[jnp.minimum(bk, nblk - 1)]]

    def wait_rows(slt):
        pltpu.make_async_copy(h2_hbm.at[pl.ds(0, tb * ROW_TILE), :], xbuf.at[slt], sem.at[slt]).wait()

    @pl.when(blk == 0)
    def _():
        for bk in range(nslots - 1):
            base = rows_base(bk)

            def body(r, carry):
                row_copy(tok_ref[base + r], r, bk).start()
                return carry
            lax.fori_loop(0, tb, body, 0, unroll=8)

    hot = blk < n_used
    e = be_ref[blk]
    e_prev = be_ref[jnp.maximum(blk - 1, 0)]

    @pl.when(jnp.logical_and(hot, jnp.logical_or(blk == 0, e != e_prev)))
    def _():
        wgu_s[...] = wgu_ref[...].astype(BF16)
        wd_s[...] = wd_ref[...].astype(BF16)

    @pl.when(hot)
    def _():
        wait_rows(slot)
        xb = _load_row_tiles(xbuf.at[slot], tb).astype(BF16)
        ahead = blk + nslots - 1
        base = rows_base(ahead)
        for r in range(tb):
            row_copy(tok_ref[base + r], r, ahead % nslots).start()
        gu = _dot(xb, wgu_s[...]) + bgu_ref[...]
        gt = jnp.minimum(gu[:, :de], SWIGLU_LIMIT)
        up = jnp.clip(gu[:, de:], -SWIGLU_LIMIT, SWIGLU_LIMIT)
        act = gt * jax.nn.sigmoid(SWIGLU_ALPHA * gt) * (up + 1.0)
        _store_row_tiles(y_ref, _dot(act.astype(BF16), wd_s[...]) + bd_ref[...])

    @pl.when(blk == n_used)
    def _():
        for k in range(nslots - 1):
            wait_rows((blk + k) % nslots)

    @pl.when(jnp.logical_and(hot, jnp.logical_and(blk == nblk - 1, n_used == nblk)))
    def _():
        for k in range(1, nslots):
            wait_rows((blk + k) % nslots)

    @pl.when(jnp.logical_not(hot))
    def _():
        y_ref[...] = jnp.zeros(y_ref.shape, F32)


def _moe_experts(h2, block_e, tok_sorted, shift, n_used, w_gate_up, b_gate_up, w_down, b_down):
    ne, d, de2 = w_gate_up.shape
    de = w_down.shape[1]
    n_blocks = block_e.shape[0]
    tb = MOE_ROWS
    wmap = lambda i, be, tk, sh, nu: (be[i], 0, 0)
    return pl.pallas_call(
        _moe_kernel,
        out_shape=jax.ShapeDtypeStruct((n_blocks * tb * ROW_TILE, LANES), F32),
        grid_spec=pltpu.PrefetchScalarGridSpec(
            num_scalar_prefetch=4,
            grid=(n_blocks,),
            in_specs=[pl.BlockSpec(memory_space=pl.ANY),
                      pl.BlockSpec((None, d, de2), wmap),
                      pl.BlockSpec((None, 1, de2), wmap),
                      pl.BlockSpec((None, de, d), wmap),
                      pl.BlockSpec((None, 1, d), wmap)],
            out_specs=pl.BlockSpec((tb * ROW_TILE, LANES), lambda i, be, tk, sh, nu: (i, 0)),
            scratch_shapes=[pltpu.VMEM((MOE_LOOKAHEAD + 1, tb * ROW_TILE, LANES), F32),
                            pltpu.SemaphoreType.DMA((MOE_LOOKAHEAD + 1,)),
                            pltpu.VMEM((d, de2), BF16),
                            pltpu.VMEM((de, d), BF16)]),
        compiler_params=_cparams(("arbitrary",)),
        name="moe_experts",
    )(block_e, tok_sorted, shift, n_used, h2, w_gate_up, b_gate_up.reshape(ne, 1, de2), w_down,
      b_down.reshape(ne, 1, d))


def _route(top_idx, n_blocks):
    n = top_idx.shape[0]
    m = n * TOP_K
    tb = MOE_ROWS
    flat_e = top_idx.reshape(m)
    pair = jnp.arange(m, dtype=jnp.int32)
    order = lax.sort(flat_e * m + pair) % m
    tok_sorted = jnp.concatenate([order // TOP_K * ROW_TILE, jnp.zeros(((MOE_LOOKAHEAD + 1) * tb,), jnp.int32)])
    onehot = (flat_e[:, None] == jnp.arange(N_EXPERTS, dtype=jnp.int32)[None, :]).astype(jnp.int32)
    csum = jnp.cumsum(onehot, axis=0)
    counts = csum[-1]
    padded = (counts + tb - 1) // tb * tb
    pends = jnp.cumsum(padded)
    pstarts = pends - padded
    starts = jnp.cumsum(counts) - counts
    dest = (jnp.sum(onehot * (csum - 1 + pstarts[None, :]), axis=1) * ROW_TILE).astype(jnp.int32)
    blk_row = jnp.arange(n_blocks, dtype=jnp.int32) * tb
    block_e = jnp.minimum(jnp.sum((pends[None, :] <= blk_row[:, None]).astype(jnp.int32), axis=1), N_EXPERTS - 1)
    n_used = (pends[-1] // tb).astype(jnp.int32).reshape(1)
    return dest, tok_sorted, (starts - pstarts).astype(jnp.int32), block_e.astype(jnp.int32), n_used


def _combine_kernel(dest_ref, yb_hbm, x1_ref, mod_ref, gate_ref, gpost_ref, o_ref, ybuf, sem, *, base):
    i = pl.program_id(0)
    nsteps = pl.num_programs(0)
    slot = i % 2
    nb, t, d = x1_ref.shape
    tm = nb * t

    def gather(step, slt):
        def body(r, carry):
            for kk in range(TOP_K):
                row8 = pl.multiple_of(dest_ref[base + (step * tm + r) * TOP_K + kk], ROW_TILE)
                dst = ybuf.at[slt, kk, pl.ds(pl.multiple_of(r * ROW_TILE, ROW_TILE), ROW_TILE), :]
                pltpu.make_async_copy(yb_hbm.at[pl.ds(row8, ROW_TILE), :], dst, sem.at[slt]).start()
            return carry
        lax.fori_loop(0, tm, body, 0, unroll=4)

    @pl.when(i == 0)
    def _():
        gather(0, 0)

    @pl.when(i + 1 < nsteps)
    def _():
        gather(i + 1, 1 - slot)

    for kk in range(TOP_K):
        pltpu.make_async_copy(yb_hbm.at[pl.ds(0, tm * ROW_TILE), :], ybuf.at[slot, kk], sem.at[slot]).wait()

    gate = gate_ref[...]
    y = gate[:, 0:1] * _load_row_tiles(ybuf.at[slot, 0], tm)
    for kk in range(1, TOP_K):
        y = y + gate[:, kk:kk + 1] * _load_row_tiles(ybuf.at[slot, kk], tm)
    mod = mod_ref[...]
    o_ref[...] = x1_ref[...] + mod[:, 5:6, :] * _rms(y, gpost_ref[...]).reshape(nb, t, d)


def _combine(yb, dest, base, x1, mod, gates, g_post_ffn, is_prompt):
    nseq, T, d = x1.shape
    n = nseq * T
    if is_prompt:
        nb, t = 1, min(COMBINE_TILE, T)
    else:
        t = T
        nb = max(1, min(nseq, COMBINE_TILE // T))
    tm = nb * t
    tiles_per_seq = T // t
    if is_prompt:
        x_map = lambda i, dr: (i // tiles_per_seq, i % tiles_per_seq, 0)
        mod_map = lambda i, dr: (i // tiles_per_seq, 0, 0)
    else:
        x_map = lambda i, dr: (i, 0, 0)
        mod_map = lambda i, dr: (i, 0, 0)
    return pl.pallas_call(
        functools.partial(_combine_kernel, base=base),
        out_shape=jax.ShapeDtypeStruct((nseq, T, d), F32),
        grid_spec=pltpu.PrefetchScalarGridSpec(
            num_scalar_prefetch=1,
            grid=(n // tm,),
            in_specs=[pl.BlockSpec(memory_space=pl.ANY),
                      pl.BlockSpec((nb, t, d), x_map),
                      pl.BlockSpec((nb, 6, d), mod_map),
                      pl.BlockSpec((tm, TOP_K), lambda i, dr: (i, 0)),
                      pl.BlockSpec((1, d), lambda i, dr: (0, 0))],
            out_specs=pl.BlockSpec((nb, t, d), x_map),
            scratch_shapes=[pltpu.VMEM((2, TOP_K, tm * ROW_TILE, LANES), F32),
                            pltpu.SemaphoreType.DMA((2,))]),
        compiler_params=_cparams(("arbitrary",)),
        name="combine_prompt" if is_prompt else "combine_sample",
    )(dest, yb, x1, mod, gates, g_post_ffn.reshape(1, d))


def _layer(xp, xs, cache_ckv, cache_kpe, layer, page_table, st_c, st_n, st_m, c_p, c_s, w):
    nbp, S, d = xp.shape
    nbs, T, _ = xs.shape
    past_len = page_table.shape[1] * PAGE_SIZE
    dt = xp.dtype

    mod = _modulation(jnp.concatenate([c_p, c_s], axis=0), w["w_ada"], w["b_ada"])
    mod = mod.reshape(nbp + nbs, 6, d)
    mod_p, mod_s = mod[:nbp], mod[nbp:]

    pw = _prep_proj_weights(w["w_in"], w["w_q_up"], w["w_uk"], w["w_uv"], w["b_igate"], w["b_fgate"])
    chunk_p = ML_CHUNK if S % ML_CHUNK == 0 else S
    chunk_s = ML_CHUNK if T % ML_CHUNK == 0 else T

    (q, k, v, ckv_p, kpe_p, mq, mk, mv, g, cum, og_p) = _project(
        xp, mod_p, _rope_tables(jnp.arange(S)), chunk_p, w["g_pre_mix"], w["g_qlat"], w["g_kvlat"], pw, True)
    o_mla_p = _prompt_attention(q, k, v, nbp, S)
    h_ml_p, c_new_p, n_new_p, m_new_p = _mlstm(
        mq, mk, mv, g, cum, jnp.zeros((nbp, ML_HEADS, ML_V, ML_QK), F32), jnp.zeros((nbp, ML_HEADS, ML_QK), F32),
        jnp.zeros((nbp, ML_HEADS), F32), nbp, S)

    (qa, qr, ckv_s, kpe_s, mq, mk, mv, g, cum, og_s) = _project(
        xs, mod_s, _rope_tables(past_len + jnp.arange(T)), chunk_s, w["g_pre_mix"], w["g_qlat"], w["g_kvlat"],
        pw, False)
    o_mla_s = _sample_attention(qa, qr, ckv_s, kpe_s, cache_ckv, cache_kpe, layer, page_table,
                                w["w_uv"].transpose(1, 0, 2))
    h_ml_s, c_new_s, n_new_s, m_new_s = _mlstm(mq, mk, mv, g, cum, st_c, st_n, st_m, nbs, T)

    w_out_b = w["w_out"].astype(BF16)
    wr_t = w["w_router"].T
    wr_hi = wr_t.astype(BF16)
    wr_lo = (wr_t - wr_hi.astype(F32)).astype(BF16)
    margs = (w["g_heads"], w_out_b, w["g_post_mix"], w["g_pre_ffn"], wr_hi, wr_lo, w["b_router"])
    x1_p, h2_p, idx_p, gate_p = _merge(xp, mod_p, o_mla_p, h_ml_p, og_p, *margs, True)
    x1_s, h2_s, idx_s, gate_s = _merge(xs, mod_s, o_mla_s, h_ml_s, og_s, *margs, False)

    n_p, n_s = nbp * S, nbs * T
    h2 = jnp.concatenate([h2_p, h2_s], axis=0)
    top_idx = jnp.concatenate([idx_p[:TOP_K], idx_s[:TOP_K]], axis=1).T
    gates = jnp.concatenate([gate_p[:TOP_K], gate_s[:TOP_K]], axis=1).T
    m_rows = (n_p + n_s) * TOP_K
    n_blocks = -(-m_rows // MOE_ROWS) + N_EXPERTS
    dest, tok_sorted, shift, block_e, n_used = _route(top_idx, n_blocks)
    yb = _moe_experts(h2, block_e, tok_sorted, shift, n_used, w["w_gate_up"], w["b_gate_up"], w["w_down"],
                      w["b_down"])
    y_p = _combine(yb, dest, 0, x1_p, mod_p, gates[:n_p], w["g_post_ffn"], True)
    y_s = _combine(yb, dest, n_p * TOP_K, x1_s, mod_s, gates[n_p:], w["g_post_ffn"], False)

    new_p = (ckv_p.reshape(nbp, S, KV_LORA), kpe_p.reshape(nbp, S, MLA_ROPE), c_new_p.astype(dt),
             n_new_p.astype(dt), m_new_p.astype(dt))
    new_s = (ckv_s.reshape(nbs, T, KV_LORA), kpe_s.reshape(nbs, T, MLA_ROPE), c_new_s.astype(dt),
             n_new_s.astype(dt), m_new_s.astype(dt))
    return y_p, y_s, new_p, new_s


def kernel(x_prompt, x_sample, cache_ckv, cache_kpe, page_table, state_C, state_n, state_m, c_prompt, c_sample,
           w_ada, b_ada, g_pre_mix, g_post_mix, g_pre_ffn, g_post_ffn, w_in, g_qlat, w_q_up, g_kvlat, w_uk, w_uv,
           b_igate, b_fgate, g_heads, w_out, w_router, b_router, w_gate_up, b_gate_up, w_down, b_down):
    weights = dict(w_ada=w_ada, b_ada=b_ada, g_pre_mix=g_pre_mix, g_post_mix=g_post_mix, g_pre_ffn=g_pre_ffn,
                   g_post_ffn=g_post_ffn, w_in=w_in, g_qlat=g_qlat, w_q_up=w_q_up, g_kvlat=g_kvlat, w_uk=w_uk,
                   w_uv=w_uv, b_igate=b_igate, b_fgate=b_fgate, g_heads=g_heads, w_out=w_out, w_router=w_router,
                   b_router=b_router, w_gate_up=w_gate_up, b_gate_up=b_gate_up, w_down=w_down, b_down=b_down)
    depth = w_ada.shape[0]
    xp, xs = x_prompt, x_sample
    news_p, news_s = [], []
    for l in range(depth):
        wl = {name: val[l] for name, val in weights.items()}
        xp, xs, new_p, new_s = _layer(xp, xs, cache_ckv, cache_kpe, l, page_table, state_C[l], state_n[l],
                                      state_m[l], c_prompt, c_sample, wl)
        news_p.append(new_p)
        news_s.append(new_s)
    stack = lambda items, j: jnp.stack([it[j] for it in items])
    return ((xp, xs) + tuple(stack(news_p, j) for j in range(5)) + tuple(stack(news_s, j) for j in range(5)))
```

```python
import functools
import math

import jax
import jax.numpy as jnp
from jax import lax
from jax.experimental import pallas as pl
from jax.experimental.pallas import tpu as pltpu

F32 = jnp.float32
BF16 = jnp.bfloat16

HEAD_OUT = 128
MLA_HEADS = 4
ML_HEADS = 4
MLA_NOPE = 128
MLA_ROPE = 64
Q_LORA = 384
KV_LORA = 256
ROPE_BASE = 10000.0
SM_SCALE = (MLA_NOPE + MLA_ROPE) ** -0.5
ML_QK = 64
ML_V = 128
ML_CHUNK = 64
IGATE_CAP = 15.0
N_EXPERTS = 32
TOP_K = 4
SWIGLU_LIMIT = 7.0
SWIGLU_ALPHA = 1.702
EPS = 1e-6
PAGE_SIZE = 128

LANES = 128
LOG2E = 1.4426950408889634
QK_SCALE = SM_SCALE * LOG2E
NEG_INF = float("-inf")

TOKEN_TILE = 512
ATTN_BLOCK = 512
PAGES_PER_STEP = 64
PAGE_SLOTS = 3
ATTN_SUB = 1024
MOE_ROWS = 512
MOE_LOOKAHEAD = 2
COMBINE_TILE = 128
VMEM_LIMIT = 56 * 1024 * 1024

NT_DIMS = (((1,), (1,)), ((), ()))
TN_DIMS = (((0,), (0,)), ((), ()))


def _cparams(sem):
    return pltpu.CompilerParams(dimension_semantics=sem, vmem_limit_bytes=VMEM_LIMIT)


def _rms(x, g):
    return x * lax.rsqrt(jnp.mean(x * x, axis=-1, keepdims=True) + EPS) * g


def _dot(a, b):
    return jnp.dot(a, b, preferred_element_type=F32)


ROW_TILE = 8


def _store_row_tiles(ref, val):
    rows = val.shape[0]
    for c in range(ROW_TILE):
        ref[pl.ds(c, rows, stride=ROW_TILE), :] = val[:, c * LANES:(c + 1) * LANES]


def _load_row_tiles(ref, rows):
    return jnp.concatenate([ref[pl.ds(c, rows, stride=ROW_TILE), :] for c in range(ROW_TILE)], axis=1)


def _mod_kernel(c_ref, w_ref, b_ref, o_ref):
    c = c_ref[...]
    s = c * jax.nn.sigmoid(c)
    o_ref[...] = _dot(s.astype(BF16), w_ref[...].astype(BF16)) + b_ref[...]


def _modulation(c, w_ada, b_ada):
    nseq, d = c.shape
    n_out = w_ada.shape[1]
    return pl.pallas_call(
        _mod_kernel,
        out_shape=jax.ShapeDtypeStruct((nseq, n_out), F32),
        grid=(n_out // d,),
        in_specs=[pl.BlockSpec((nseq, d), lambda j: (0, 0)),
                  pl.BlockSpec((d, d), lambda j: (0, j)),
                  pl.BlockSpec((1, d), lambda j: (0, j))],
        out_specs=pl.BlockSpec((nseq, d), lambda j: (0, j)),
        compiler_params=_cparams(("arbitrary",)),
        name="modulation",
    )(c, w_ada, b_ada.reshape(1, n_out))


_C_QLAT = 0
_C_CKV = _C_QLAT + Q_LORA
_C_KPE = _C_CKV + KV_LORA
_C_KROT = _C_KPE + LANES
_C_MQ = _C_KROT + LANES
_C_MK = _C_MQ + ML_HEADS * ML_QK
_C_MV = _C_MK + ML_HEADS * ML_QK
_C_OG = _C_MV + ML_HEADS * ML_V
_C_GATE = _C_OG + ML_HEADS * ML_V
_C_END = _C_GATE + LANES


def _rot_half_cols(w):
    half = w.shape[-1] // 2
    return jnp.concatenate([-w[..., half:], w[..., :half]], axis=-1)


def _prep_proj_weights(w_in, w_q_up, w_uk, w_uv, b_igate, b_fgate):
    d = w_in.shape[0]
    o = 0
    parts = {}
    for name, size in (("qlat", Q_LORA), ("ckv", KV_LORA), ("kpe", MLA_ROPE), ("mq", ML_HEADS * ML_QK),
                       ("mk", ML_HEADS * ML_QK), ("mv", ML_HEADS * ML_V), ("ig", ML_HEADS), ("fg", ML_HEADS),
                       ("og", ML_HEADS * ML_V)):
        parts[name] = w_in[:, o:o + size]
        o += size
    z64 = jnp.zeros((d, LANES - MLA_ROPE), F32)
    zg = jnp.zeros((d, LANES - 2 * ML_HEADS), F32)
    w_in_r = jnp.concatenate(
        [parts["qlat"], parts["ckv"], parts["kpe"], z64, _rot_half_cols(parts["kpe"]), z64,
         parts["mq"], parts["mk"], parts["mv"], parts["og"], parts["ig"], parts["fg"], zg], axis=1).astype(BF16)
    b_gate = jnp.concatenate([b_igate, b_fgate, jnp.zeros((LANES - 2 * ML_HEADS,), F32)]).reshape(1, LANES)
    w_nope = w_q_up[:, :, :MLA_NOPE].reshape(Q_LORA, MLA_HEADS * MLA_NOPE)
    w_rope = w_q_up[:, :, MLA_NOPE:]
    zr = jnp.zeros((Q_LORA, MLA_HEADS, LANES - MLA_ROPE), F32)
    w_qr = jnp.concatenate([w_rope, zr], axis=-1).reshape(Q_LORA, MLA_HEADS * LANES)
    w_qt = jnp.concatenate([_rot_half_cols(w_rope), zr], axis=-1).reshape(Q_LORA, MLA_HEADS * LANES)
    w_uk2 = w_uk.reshape(KV_LORA, MLA_HEADS * MLA_NOPE).astype(BF16)
    w_uv2 = w_uv.reshape(KV_LORA, MLA_HEADS * HEAD_OUT).astype(BF16)
    w_ukT = jnp.transpose(w_uk, (1, 2, 0)).astype(BF16)
    return dict(w_in=w_in_r, b_gate=b_gate, w_qn=w_nope.astype(BF16), w_qr=w_qr.astype(BF16),
                w_qt=w_qt.astype(BF16), w_uk=w_uk2, w_uv=w_uv2, w_ukT=w_ukT)


def _rope_tables(pos):
    half = MLA_ROPE // 2
    inv_freq = ROPE_BASE ** (-jnp.arange(half, dtype=F32) / half)
    ang = pos.astype(F32)[:, None] * inv_freq[None, :]
    z = jnp.zeros((pos.shape[0], LANES - MLA_ROPE), F32)
    cos = jnp.concatenate([jnp.cos(ang), jnp.cos(ang), z], axis=1)
    sin = jnp.concatenate([jnp.sin(ang), jnp.sin(ang), z], axis=1)
    return cos, sin


def _chunk_tril(tm, chunk):
    r = jnp.arange(tm)
    same = (r[:, None] // chunk) == (r[None, :] // chunk)
    return (same & (r[None, :] <= r[:, None])).astype(BF16)


def _split3(x):
    hi = x.astype(BF16)
    r1 = x - hi.astype(F32)
    mid = r1.astype(BF16)
    lo = (r1 - mid.astype(F32)).astype(BF16)
    return hi, mid, lo


def _proj_kernel(*refs, is_prompt):
    (x_ref, mod_ref, gpre_ref, win_ref, bg_ref, gq_ref, wqn_ref, wqr_ref, wqt_ref, gkv_ref,
     cos_ref, sin_ref, tril_ref) = refs[:13]
    if is_prompt:
        wuk_ref, wuv_ref = refs[13:15]
        q_ref, k_ref, v_ref, ckv_ref, kpe_ref, mq_ref, mk_ref, mv_ref, g_ref, cum_ref, og_ref = refs[15:]
    else:
        wukT_ref = refs[13]
        qa_ref, qr_ref, ckv_ref, kpe_ref, mq_ref, mk_ref, mv_ref, g_ref, cum_ref, og_ref = refs[14:]

    x = x_ref[...]
    mod = mod_ref[...]
    nb, t, d = x.shape
    h = _rms(x, gpre_ref[...]) * (1.0 + mod[:, 1:2, :]) + mod[:, 0:1, :]
    hb = h.reshape(nb * t, d).astype(BF16)

    def zcols(a, b):
        return _dot(hb, win_ref[:, a:b])

    cos = cos_ref[...]
    sin = sin_ref[...]

    qn = _rms(zcols(_C_QLAT, _C_CKV), gq_ref[...]).astype(BF16)
    q_nope = _dot(qn, wqn_ref[...]) * QK_SCALE
    q_r = _dot(qn, wqr_ref[...])
    q_t = _dot(qn, wqt_ref[...])
    ckv = _rms(zcols(_C_CKV, _C_KPE), gkv_ref[...])
    ckv_ref[...] = ckv
    kpe = zcols(_C_KPE, _C_KROT) * cos + zcols(_C_KROT, _C_MQ) * sin
    kpe_ref[...] = kpe[:, :MLA_ROPE]
    for hd in range(MLA_HEADS):
        sl = slice(hd * LANES, (hd + 1) * LANES)
        roped = (q_r[:, sl] * cos + q_t[:, sl] * sin) * QK_SCALE
        if is_prompt:
            q_ref[:, 2 * hd * LANES:(2 * hd + 1) * LANES] = q_nope[:, sl].astype(BF16)
            q_ref[:, (2 * hd + 1) * LANES:(2 * hd + 2) * LANES] = roped.astype(BF16)
        else:
            qa_ref[:, hd * KV_LORA:(hd + 1) * KV_LORA] = _dot(q_nope[:, sl].astype(BF16), wukT_ref[hd])
            qr_ref[:, sl] = roped
    if is_prompt:
        cb = ckv.astype(BF16)
        k_nope = _dot(cb, wuk_ref[...])
        kpe_b = kpe.astype(BF16)
        for hd in range(MLA_HEADS):
            k_ref[:, 2 * hd * LANES:(2 * hd + 1) * LANES] = k_nope[:, hd * LANES:(hd + 1) * LANES].astype(BF16)
            k_ref[:, (2 * hd + 1) * LANES:(2 * hd + 2) * LANES] = kpe_b
        v_ref[...] = _dot(cb, wuv_ref[...]).astype(BF16)

    mq_ref[...] = zcols(_C_MQ, _C_MK) * (ML_QK ** -0.5)
    mk_ref[...] = zcols(_C_MK, _C_MV)
    mv_ref[...] = zcols(_C_MV, _C_OG)
    og_ref[...] = jax.nn.sigmoid(zcols(_C_OG, _C_GATE))
    zg = zcols(_C_GATE, _C_END) + bg_ref[...]
    ig = IGATE_CAP * jnp.tanh(zg / IGATE_CAP)
    lf = -(jnp.maximum(-zg, 0.0) + jnp.log1p(jnp.exp(-jnp.abs(zg))))
    lane = lax.broadcasted_iota(jnp.int32, zg.shape, 1)
    g = jnp.where(lane < ML_HEADS, ig, lf)
    g_ref[...] = g
    tril = tril_ref[...]
    hi, mid, lo = _split3(g)
    cum_ref[...] = _dot(tril, hi) + _dot(tril, mid) + _dot(tril, lo)


def _project(x, mod, pos_tab, chunk, g_pre, g_qlat, g_kvlat, pw, is_prompt):
    nseq, T, d = x.shape
    n = nseq * T
    if is_prompt:
        nb, t = 1, min(TOKEN_TILE, T)
    else:
        t = T
        nb = max(1, min(nseq, TOKEN_TILE // T))
    assert T % t == 0 and nseq % nb == 0
    tm = nb * t
    tiles_per_seq = T // t
    grid = (n // tm,)
    cos, sin = pos_tab
    if is_prompt:
        x_map = lambda i: (i // tiles_per_seq, i % tiles_per_seq, 0)
        mod_map = lambda i: (i // tiles_per_seq, 0, 0)
        tab_map = lambda i: (i % tiles_per_seq, 0)
    else:
        x_map = lambda i: (i, 0, 0)
        mod_map = lambda i: (i, 0, 0)
        tab_map = lambda i: (0, 0)
        cos = jnp.tile(cos, (nb, 1))
        sin = jnp.tile(sin, (nb, 1))
    const = lambda i: (0, 0)
    row = lambda i: (i, 0)
    tril = _chunk_tril(tm, chunk)
    in_specs = [
        pl.BlockSpec((nb, t, d), x_map),
        pl.BlockSpec((nb, 6, d), mod_map),
        pl.BlockSpec((1, d), const),
        pl.BlockSpec((d, _C_END), const),
        pl.BlockSpec((1, LANES), const),
        pl.BlockSpec((1, Q_LORA), const),
        pl.BlockSpec((Q_LORA, MLA_HEADS * MLA_NOPE), const),
        pl.BlockSpec((Q_LORA, MLA_HEADS * LANES), const),
        pl.BlockSpec((Q_LORA, MLA_HEADS * LANES), const),
        pl.BlockSpec((1, KV_LORA), const),
        pl.BlockSpec((tm, LANES), tab_map),
        pl.BlockSpec((tm, LANES), tab_map),
        pl.BlockSpec((tm, tm), const),
    ]
    args = [x, mod, g_pre.reshape(1, d), pw["w_in"], pw["b_gate"], g_qlat.reshape(1, Q_LORA), pw["w_qn"],
            pw["w_qr"], pw["w_qt"], g_kvlat.reshape(1, KV_LORA), cos, sin, tril]

    def out(cols, dtype=F32):
        return jax.ShapeDtypeStruct((n, cols), dtype), pl.BlockSpec((tm, cols), row)

    if is_prompt:
        in_specs += [pl.BlockSpec((KV_LORA, MLA_HEADS * MLA_NOPE), const),
                     pl.BlockSpec((KV_LORA, MLA_HEADS * HEAD_OUT), const)]
        args += [pw["w_uk"], pw["w_uv"]]
        outs = [out(MLA_HEADS * 2 * LANES, BF16), out(MLA_HEADS * 2 * LANES, BF16), out(MLA_HEADS * HEAD_OUT, BF16)]
    else:
        in_specs += [pl.BlockSpec((MLA_HEADS, MLA_NOPE, KV_LORA), lambda i: (0, 0, 0))]
        args += [pw["w_ukT"]]
        outs = [out(MLA_HEADS * KV_LORA), out(MLA_HEADS * LANES)]
    outs += [out(KV_LORA), out(MLA_ROPE), out(ML_HEADS * ML_QK), out(ML_HEADS * ML_QK), out(ML_HEADS * ML_V),
             out(LANES), out(LANES), out(ML_HEADS * ML_V)]
    return pl.pallas_call(
        functools.partial(_proj_kernel, is_prompt=is_prompt),
        out_shape=[o[0] for o in outs],
        grid=grid,
        in_specs=in_specs,
        out_specs=[o[1] for o in outs],
        compiler_params=_cparams(("arbitrary",)),
        name="proj_prompt" if is_prompt else "proj_sample",
    )(*args)


def _attn_kernel(qi_ref, ki_ref, q_ref, k_ref, v_ref, o_ref, m_ref, l_ref, acc_ref):
    p = pl.program_id(1)
    qi = qi_ref[p]
    ki = ki_ref[p]
    tq = q_ref.shape[0]
    tk = k_ref.shape[0]

    @pl.when(ki == 0)
    def _():
        m_ref[...] = jnp.full(m_ref.shape, NEG_INF, F32)
        l_ref[...] = jnp.zeros(l_ref.shape, F32)
        acc_ref[...] = jnp.zeros(acc_ref.shape, F32)

    def update(masked):
        for hd in range(MLA_HEADS):
            qh = q_ref[:, hd * 2 * LANES:(hd + 1) * 2 * LANES]
            kh = k_ref[:, hd * 2 * LANES:(hd + 1) * 2 * LANES]
            s = lax.dot_general(qh, kh, NT_DIMS, preferred_element_type=F32)
            if masked:
                row = lax.broadcasted_iota(jnp.int32, s.shape, 0)
                col = lax.broadcasted_iota(jnp.int32, s.shape, 1)
                s = jnp.where(col <= row, s, NEG_INF)
            m_prev = m_ref[hd]
            m_next = jnp.maximum(m_prev, jnp.max(s, axis=1, keepdims=True))
            alpha = jnp.exp2(m_prev - m_next)
            pr = jnp.exp2(s - jnp.concatenate([m_next] * (tk // LANES), axis=1))
            l_ref[hd] = alpha * l_ref[hd] + jnp.sum(pr, axis=1, keepdims=True)
            acc_ref[hd] = alpha * acc_ref[hd] + _dot(pr.astype(BF16), v_ref[:, hd * HEAD_OUT:(hd + 1) * HEAD_OUT])
            m_ref[hd] = m_next

    @pl.when(ki < qi)
    def _():
        update(False)

    @pl.when(ki == qi)
    def _():
        update(True)
        for hd in range(MLA_HEADS):
            o_ref[:, hd * HEAD_OUT:(hd + 1) * HEAD_OUT] = acc_ref[hd] / l_ref[hd]


def _prompt_attention(q, k, v, nseq, S):
    blk = min(ATTN_BLOCK, S)
    nq = S // blk
    pairs = [(i, j) for i in range(nq) for j in range(i + 1)]
    qi_tab = jnp.asarray([p[0] for p in pairs], jnp.int32)
    ki_tab = jnp.asarray([p[1] for p in pairs], jnp.int32)
    wq = MLA_HEADS * 2 * LANES
    wv = MLA_HEADS * HEAD_OUT
    return pl.pallas_call(
        _attn_kernel,
        out_shape=jax.ShapeDtypeStruct((nseq * S, wv), F32),
        grid_spec=pltpu.PrefetchScalarGridSpec(
            num_scalar_prefetch=2,
            grid=(nseq, len(pairs)),
            in_specs=[pl.BlockSpec((blk, wq), lambda b, p, qi, ki: (b * nq + qi[p], 0)),
                      pl.BlockSpec((blk, wq), lambda b, p, qi, ki: (b * nq + ki[p], 0)),
                      pl.BlockSpec((blk, wv), lambda b, p, qi, ki: (b * nq + ki[p], 0))],
            out_specs=pl.BlockSpec((blk, wv), lambda b, p, qi, ki: (b * nq + qi[p], 0)),
            scratch_shapes=[pltpu.VMEM((MLA_HEADS, blk, LANES), F32),
                            pltpu.VMEM((MLA_HEADS, blk, LANES), F32),
                            pltpu.VMEM((MLA_HEADS, blk, HEAD_OUT), F32)]),
        compiler_params=_cparams(("arbitrary", "arbitrary")),
        name="prompt_attention",
    )(qi_tab, ki_tab, q, k, v)


def _sattn_kernel(pt_ref, qa_ref, qr_ref, ckvn_ref, kpen_ref, wuv_ref, cckv_hbm, ckpe_hbm, o_ref,
                  ckv_buf, kpe_buf, sem, kcb, wq_s, q_s, qr_s, kn_s, pn_s, m_ref, l_ref, acc_ref,
                  *, layer, npg, nchunks, sub):
    b = pl.program_id(0)
    c = pl.program_id(1)
    step = b * nchunks + c
    nsteps = pl.num_programs(0) * nchunks
    nslots = ckv_buf.shape[0]
    slot = step % nslots
    T = qa_ref.shape[0]
    rows = MLA_HEADS * T
    npos = npg * PAGE_SIZE
    pack = wq_s.shape[0]

    def page_copies(page_of, slt):
        cps = []
        for j in range(npg):
            pg = page_of(j)
            dst = pl.ds(j * PAGE_SIZE, PAGE_SIZE)
            cps.append(pltpu.make_async_copy(cckv_hbm.at[layer, pg], ckv_buf.at[slt, dst, :], sem.at[0, slt]))
            cps.append(pltpu.make_async_copy(ckpe_hbm.at[layer, pg], kpe_buf.at[slt, :, dst], sem.at[1, slt]))
        return cps

    def fetch(stp, slt):
        base = jnp.minimum(stp, nsteps - 1) * npg
        for cp in page_copies(lambda j: pt_ref[base + j], slt):
            cp.start()

    def wait_pages(slt):
        for cp in page_copies(lambda j: 0, slt):
            cp.wait()

    @pl.when(step == 0)
    def _():
        for k in range(nslots - 1):
            fetch(k, k)

    @pl.when(c == 0)
    def _():
        m_ref[...] = jnp.full(m_ref.shape, NEG_INF, F32)
        l_ref[...] = jnp.zeros(l_ref.shape, F32)
        acc_ref[...] = jnp.zeros(acc_ref.shape, F32)
        qa = qa_ref[...]
        qr = qr_ref[...]
        q32 = jnp.concatenate([qa[:, hd * KV_LORA:(hd + 1) * KV_LORA] for hd in range(MLA_HEADS)], axis=0)
        q_s[...] = q32.astype(BF16)
        qr_s[...] = jnp.concatenate([qr[:, hd * LANES:(hd + 1) * LANES] for hd in range(MLA_HEADS)],
                                    axis=0).astype(BF16)
        qt = jnp.concatenate([q32] * pack, axis=0).T
        lane_blk = lax.broadcasted_iota(jnp.int32, qt.shape, 1) // rows
        for j in range(pack):
            wq_s[j] = jnp.where(lane_blk == j, qt, 0.0).astype(BF16)

    wait_pages(slot)

    q = q_s[...]
    qr = qr_s[:, :MLA_ROPE]

    def softmax_update(s, pv):
        m_prev = m_ref[...]
        m_next = jnp.maximum(m_prev, jnp.max(s, axis=1, keepdims=True))
        alpha = jnp.exp2(m_prev - m_next)
        pr = jnp.exp2(s - jnp.concatenate([m_next] * (s.shape[1] // LANES), axis=1))
        l_ref[...] = alpha * l_ref[...] + jnp.sum(pr, axis=1, keepdims=True)
        acc_ref[...] = jnp.concatenate([alpha] * (KV_LORA // LANES), axis=1) * acc_ref[...] + pv(pr.astype(BF16))
        m_ref[...] = m_next

    s_parts = []
    for grp in range(npos // (sub * pack)):
        st = None
        for j in range(pack):
            sb = grp * pack + j
            sl = slice(sb * sub, (sb + 1) * sub)
            kc = ckv_buf[slot, sl, :].astype(BF16)
            kcb[sl, :] = kc
            t = _dot(kc, wq_s[j])
            st = t if st is None else st + t
        s_t = st.T
        for j in range(pack):
            sb = grp * pack + j
            kp = kpe_buf[slot, :, sb * sub:(sb + 1) * sub].astype(BF16)
            s_parts.append(s_t[j * rows:(j + 1) * rows, :] + _dot(qr, kp))

    fetch(step + nslots - 1, (step + nslots - 1) % nslots)

    def pv_chunk(p):
        parts = [_dot(p[:, sb * sub:(sb + 1) * sub], kcb[sb * sub:(sb + 1) * sub, :]) for sb in range(npos // sub)]
        while len(parts) > 1:
            parts = [parts[i] + parts[i + 1] for i in range(0, len(parts) - 1, 2)] + parts[len(parts) & ~1:]
        return parts[0]

    softmax_update(jnp.concatenate(s_parts, axis=1), pv_chunk)

    @pl.when(c == nchunks - 1)
    def _():
        kn_s[...] = jnp.zeros(kn_s.shape, F32)
        pn_s[...] = jnp.zeros(pn_s.shape, F32)
        kn_s[0:T, :] = ckvn_ref[...]
        pn_s[0:T, :] = kpen_ref[...]
        row = lax.broadcasted_iota(jnp.int32, (rows, LANES), 0)
        col = lax.broadcasted_iota(jnp.int32, (rows, LANES), 1)
        kn = kn_s[...].astype(BF16)
        s_new = (lax.dot_general(q, kn, NT_DIMS, preferred_element_type=F32)
                 + lax.dot_general(qr, pn_s[...].astype(BF16), NT_DIMS, preferred_element_type=F32))
        softmax_update(jnp.where(col <= row % T, s_new, NEG_INF), lambda p: _dot(p, kn))
        o_lat = acc_ref[...] / jnp.concatenate([l_ref[...]] * (KV_LORA // LANES), axis=1)
        for hd in range(MLA_HEADS):
            o_ref[:, hd * HEAD_OUT:(hd + 1) * HEAD_OUT] = _dot(o_lat[hd * T:(hd + 1) * T, :], wuv_ref[hd])

    @pl.when(step == nsteps - 1)
    def _():
        for k in range(1, nslots):
            wait_pages((step + k) % nslots)


def _sample_attention(qa, qr, ckv_new, kpe_new, cache_ckv, cache_kpe, layer, page_table, w_uv):
    nseq, n_pages = page_table.shape
    T = qa.shape[0] // nseq
    npg = min(PAGES_PER_STEP, n_pages)
    assert n_pages % npg == 0
    nchunks = n_pages // npg
    rows = MLA_HEADS * T
    assert LANES % rows == 0
    pack = LANES // rows
    sub = min(ATTN_SUB, npg * PAGE_SIZE // pack)
    assert (npg * PAGE_SIZE) % (sub * pack) == 0
    seq = lambda b, c, pt: (b, 0)
    kern = functools.partial(_sattn_kernel, layer=layer, npg=npg, nchunks=nchunks, sub=sub)
    return pl.pallas_call(
        kern,
        out_shape=jax.ShapeDtypeStruct((nseq * T, MLA_HEADS * HEAD_OUT), F32),
        grid_spec=pltpu.PrefetchScalarGridSpec(
            num_scalar_prefetch=1,
            grid=(nseq, nchunks),
            in_specs=[pl.BlockSpec((T, MLA_HEADS * KV_LORA), seq),
                      pl.BlockSpec((T, MLA_HEADS * LANES), seq),
                      pl.BlockSpec((T, KV_LORA), seq),
                      pl.BlockSpec((T, MLA_ROPE), seq),
                      pl.BlockSpec((MLA_HEADS, KV_LORA, HEAD_OUT), lambda b, c, pt: (0, 0, 0)),
                      pl.BlockSpec(memory_space=pl.ANY),
                      pl.BlockSpec(memory_space=pl.ANY)],
            out_specs=pl.BlockSpec((T, MLA_HEADS * HEAD_OUT), seq),
            scratch_shapes=[pltpu.VMEM((PAGE_SLOTS, npg * PAGE_SIZE, KV_LORA), F32),
                            pltpu.VMEM((PAGE_SLOTS, MLA_ROPE, npg * PAGE_SIZE), F32),
                            pltpu.SemaphoreType.DMA((2, PAGE_SLOTS)),
                            pltpu.VMEM((npg * PAGE_SIZE, KV_LORA), BF16),
                            pltpu.VMEM((pack, KV_LORA, LANES), BF16),
                            pltpu.VMEM((rows, KV_LORA), BF16),
                            pltpu.VMEM((rows, LANES), BF16),
                            pltpu.VMEM((LANES, KV_LORA), F32),
                            pltpu.VMEM((LANES, MLA_ROPE), F32),
                            pltpu.VMEM((rows, LANES), F32),
                            pltpu.VMEM((rows, LANES), F32),
                            pltpu.VMEM((rows, KV_LORA), F32)]),
        compiler_params=_cparams(("arbitrary", "arbitrary")),
        name="sample_attention",
    )(page_table.reshape(-1), qa, qr, ckv_new, kpe_new, w_uv, cache_ckv, jnp.swapaxes(cache_kpe, 2, 3))


ML_HK = ML_HEADS * ML_QK


def _rows_per_head(vals, rows):
    return jnp.concatenate([jnp.broadcast_to(v, (rows, v.shape[1])) for v in vals], axis=0)


def _head_block_mask(nrows, row_blk, ncols, col_blk):
    r = lax.broadcasted_iota(jnp.int32, (nrows, ncols), 0) // row_blk
    c = lax.broadcasted_iota(jnp.int32, (nrows, ncols), 1) // col_blk
    return r == c


def _mlstm_kernel(mq_ref, mk_ref, mv_ref, g_ref, cum_ref, s0_ref, n0_ref, m0_ref, h_ref, st_ref, n_ref, m_ref,
                  *, unroll_seqs, dot_dtype):
    ci = pl.program_id(1)
    nbs, L = mq_ref.shape[0], mq_ref.shape[1]
    H = ML_HEADS
    R = H * L

    @pl.when(ci == 0)
    def _():
        st_ref[...] = s0_ref[...]
        n_ref[...] = n0_ref[...]
        m_ref[...] = m0_ref[...]

    trow = lax.broadcasted_iota(jnp.int32, (R, L), 0) % L
    tcol = lax.broadcasted_iota(jnp.int32, (R, L), 1)
    causal = tcol <= trow
    q_mask = _head_block_mask(R, L, ML_HK, ML_QK)
    s_mask = _head_block_mask(R, L, R, L)
    k_mask = _head_block_mask(ML_HK, ML_QK, R, L)

    def one_seq(i):
        g = g_ref[i]
        cum = cum_ref[i]
        g_t = g.T
        cum_t = cum.T
        mq = mq_ref[i]
        mk = mk_ref[i]
        mv = mv_ref[i]
        mprev = m_ref[i]
        ig_col = jnp.concatenate([g[:, h:h + 1] for h in range(H)], axis=0)
        b_col = jnp.concatenate([cum[:, H + h:H + h + 1] for h in range(H)], axis=0)
        ig_row = _rows_per_head([g_t[h:h + 1, :] for h in range(H)], L)
        b_row = _rows_per_head([cum_t[H + h:H + h + 1, :] for h in range(H)], L)
        m_prev = _rows_per_head([mprev[h:h + 1, 0:1] for h in range(H)], L)
        b_last = _rows_per_head([cum[L - 1:L, H + h:H + h + 1] for h in range(H)], L)

        log_d = jnp.where(causal, b_col - b_row + ig_row, NEG_INF)
        log_inter = b_col + m_prev
        m_t = jnp.maximum(log_inter, jnp.max(log_d, axis=1, keepdims=True))
        d = jnp.exp(log_d - m_t)
        inter = jnp.exp(log_inter - m_t)

        q_blk = jnp.where(q_mask, jnp.concatenate([mq] * H, axis=0), 0.0)
        k_t = mk.T
        s = _dot(q_blk.astype(dot_dtype), k_t.astype(dot_dtype)) * d
        s_bd = jnp.where(s_mask, jnp.concatenate([s] * H, axis=1), 0.0)
        v_stack = jnp.concatenate([mv[:, h * ML_V:(h + 1) * ML_V] for h in range(H)], axis=0)
        st = st_ref[i]
        n_row = n_ref[i]
        num = (_dot((inter * q_blk).astype(dot_dtype), st.astype(dot_dtype))
               + _dot(s_bd.astype(dot_dtype), v_stack.astype(dot_dtype)))
        den = inter * jnp.sum(q_blk * n_row, axis=1, keepdims=True) + jnp.sum(s, axis=1, keepdims=True)
        h_all = num / jnp.maximum(jnp.abs(den), jnp.exp(-m_t))
        for h in range(H):
            h_ref[i, :, h * ML_V:(h + 1) * ML_V] = h_all[h * L:(h + 1) * L, :]

        m_last = [m_t[(h + 1) * L - 1:(h + 1) * L, :] for h in range(H)]
        m_new = _rows_per_head(m_last, L)
        w_col = jnp.exp(b_last - b_col + ig_col - m_new)
        decay = jnp.exp(b_last + m_prev - m_new)
        kt_blk = jnp.where(k_mask, jnp.concatenate([k_t] * H, axis=1), 0.0)
        decay_k = _rows_per_head([decay[h * L:h * L + 1, :] for h in range(H)], ML_QK)
        st_ref[i] = decay_k * st + _dot(kt_blk.astype(dot_dtype), (w_col * v_stack).astype(dot_dtype))

        def widen(col):
            return jnp.concatenate([jnp.broadcast_to(col[h * L:(h + 1) * L, :], (L, ML_QK)) for h in range(H)],
                                   axis=1)
        n_ref[i] = widen(decay)[0:1, :] * n_row + jnp.sum(widen(w_col) * mk, axis=0, keepdims=True)
        m_ref[i] = jnp.concatenate([jnp.broadcast_to(ml, (1, LANES)) for ml in m_last], axis=0)

    if unroll_seqs:
        for i in range(nbs):
            one_seq(i)
    else:
        def body(i, carry):
            one_seq(i)
            return carry
        lax.fori_loop(0, nbs, body, 0)


def _mlstm(mq, mk, mv, g, cum, c0, n0, m0, nseq, T):
    L = ML_CHUNK if T % ML_CHUNK == 0 else T
    nc = T // L
    nbs = nseq if nseq <= 2 else 16
    assert nseq % nbs == 0
    r3 = lambda a: a.reshape(nseq, T, a.shape[-1])
    tok = lambda gi, ci: (gi, ci, 0)
    st3 = lambda gi, ci: (gi, 0, 0)
    ct0 = jnp.swapaxes(c0.astype(F32), 2, 3).reshape(nseq, ML_HK, ML_V)
    n0r = n0.astype(F32).reshape(nseq, 1, ML_HK)
    m0b = jnp.broadcast_to(m0.astype(F32)[:, :, None], (nseq, ML_HEADS, LANES))
    kern = functools.partial(_mlstm_kernel, unroll_seqs=nbs <= 2, dot_dtype=BF16 if L >= 16 else F32)
    h, st, n, m = pl.pallas_call(
        kern,
        out_shape=[jax.ShapeDtypeStruct((nseq, T, ML_HEADS * ML_V), F32),
                   jax.ShapeDtypeStruct((nseq, ML_HK, ML_V), F32),
                   jax.ShapeDtypeStruct((nseq, 1, ML_HK), F32),
                   jax.ShapeDtypeStruct((nseq, ML_HEADS, LANES), F32)],
        grid=(nseq // nbs, nc),
        in_specs=[pl.BlockSpec((nbs, L, ML_HK), tok),
                  pl.BlockSpec((nbs, L, ML_HK), tok),
                  pl.BlockSpec((nbs, L, ML_HEADS * ML_V), tok),
                  pl.BlockSpec((nbs, L, LANES), tok),
                  pl.BlockSpec((nbs, L, LANES), tok),
                  pl.BlockSpec((nbs, ML_HK, ML_V), st3),
                  pl.BlockSpec((nbs, 1, ML_HK), st3),
                  pl.BlockSpec((nbs, ML_HEADS, LANES), st3)],
        out_specs=[pl.BlockSpec((nbs, L, ML_HEADS * ML_V), tok),
                   pl.BlockSpec((nbs, ML_HK, ML_V), st3),
                   pl.BlockSpec((nbs, 1, ML_HK), st3),
                   pl.BlockSpec((nbs, ML_HEADS, LANES), st3)],
        compiler_params=_cparams(("arbitrary", "arbitrary")),
        name="mlstm_prompt" if nseq <= 2 else "mlstm_sample",
    )(r3(mq), r3(mk), r3(mv), r3(g), r3(cum), ct0, n0r, m0b)
    c_new = jnp.swapaxes(st.reshape(nseq, ML_HEADS, ML_QK, ML_V), 2, 3)
    return h.reshape(nseq * T, ML_HEADS * ML_V), c_new, n.reshape(nseq, ML_HEADS, ML_QK), m[:, :, 0]


def _merge_kernel(x_ref, mod_ref, omla_ref, hml_ref, og_ref, gh_ref, wout_ref, gpm_ref, gpf_ref,
                  wrh_ref, wrl_ref, br_ref, x1_ref, h2_ref, idx_ref, gate_ref):
    nb, t, d = x_ref.shape
    tm = nb * t
    gh = gh_ref[...]
    parts = []
    for hd in range(MLA_HEADS + ML_HEADS):
        if hd < MLA_HEADS:
            xh = omla_ref[:, hd * HEAD_OUT:(hd + 1) * HEAD_OUT]
        else:
            j = hd - MLA_HEADS
            xh = hml_ref[:, j * HEAD_OUT:(j + 1) * HEAD_OUT]
        yh = _rms(xh, gh[:, hd * HEAD_OUT:(hd + 1) * HEAD_OUT])
        if hd >= MLA_HEADS:
            yh = yh * og_ref[:, j * HEAD_OUT:(j + 1) * HEAD_OUT]
        parts.append(yh.astype(BF16))
    merged = jnp.concatenate(parts, axis=1)
    mix = _dot(merged, wout_ref[...])
    mod = mod_ref[...]
    x1 = x_ref[...] + mod[:, 2:3, :] * _rms(mix, gpm_ref[...]).reshape(nb, t, d)
    x1_ref[...] = x1
    h2 = (_rms(x1, gpf_ref[...]) * (1.0 + mod[:, 4:5, :]) + mod[:, 3:4, :]).reshape(tm, d)
    _store_row_tiles(h2_ref, h2)
    hh = h2.astype(BF16)
    hl = (h2 - hh.astype(F32)).astype(BF16)
    wrh = wrh_ref[...]
    logits = (lax.dot_general(wrh, hh, NT_DIMS, preferred_element_type=F32)
              + lax.dot_general(wrh, hl, NT_DIMS, preferred_element_type=F32)
              + lax.dot_general(wrl_ref[...], hh, NT_DIMS, preferred_element_type=F32)) + br_ref[...]
    eidx = lax.broadcasted_iota(jnp.int32, logits.shape, 0)
    work = logits
    vals, idxs = [], []
    for _ in range(TOP_K):
        mx = jnp.max(work, axis=0, keepdims=True)
        sel = jnp.min(jnp.where(work == mx, eidx, N_EXPERTS), axis=0, keepdims=True)
        vals.append(mx)
        idxs.append(sel)
        work = jnp.where(eidx == sel, NEG_INF, work)
    ex = [jnp.exp(v - vals[0]) for v in vals]
    tot = ex[0] + ex[1] + ex[2] + ex[3]
    zi = jnp.zeros((8 - TOP_K, tm), jnp.int32)
    zf = jnp.zeros((8 - TOP_K, tm), F32)
    idx_ref[...] = jnp.concatenate(idxs + [zi], axis=0)
    gate_ref[...] = jnp.concatenate([e / tot for e in ex] + [zf], axis=0)


def _merge(x, mod, o_mla, h_ml, og, g_heads, w_out, g_post_mix, g_pre_ffn, wr_hi, wr_lo, b_router, is_prompt):
    nseq, T, d = x.shape
    n = nseq * T
    if is_prompt:
        nb, t = 1, min(TOKEN_TILE, T)
    else:
        t = T
        nb = max(1, min(nseq, TOKEN_TILE // T))
    tm = nb * t
    tiles_per_seq = T // t
    if is_prompt:
        x_map = lambda i: (i // tiles_per_seq, i % tiles_per_seq, 0)
        mod_map = lambda i: (i // tiles_per_seq, 0, 0)
    else:
        x_map = lambda i: (i, 0, 0)
        mod_map = lambda i: (i, 0, 0)
    const = lambda i: (0, 0)
    row = lambda i: (i, 0)
    col = lambda i: (0, i)
    wh = MLA_HEADS * HEAD_OUT
    return pl.pallas_call(
        _merge_kernel,
        out_shape=[jax.ShapeDtypeStruct((nseq, T, d), F32), jax.ShapeDtypeStruct((n * ROW_TILE, LANES), F32),
                   jax.ShapeDtypeStruct((8, n), jnp.int32), jax.ShapeDtypeStruct((8, n), F32)],
        grid=(n // tm,),
        in_specs=[pl.BlockSpec((nb, t, d), x_map),
                  pl.BlockSpec((nb, 6, d), mod_map),
                  pl.BlockSpec((tm, wh), row),
                  pl.BlockSpec((tm, wh), row),
                  pl.BlockSpec((tm, wh), row),
                  pl.BlockSpec((1, 2 * wh), const),
                  pl.BlockSpec((2 * wh, d), const),
                  pl.BlockSpec((1, d), const),
                  pl.BlockSpec((1, d), const),
                  pl.BlockSpec((N_EXPERTS, d), const),
                  pl.BlockSpec((N_EXPERTS, d), const),
                  pl.BlockSpec((N_EXPERTS, 1), const)],
        out_specs=[pl.BlockSpec((nb, t, d), x_map), pl.BlockSpec((tm * ROW_TILE, LANES), row),
                   pl.BlockSpec((8, tm), col), pl.BlockSpec((8, tm), col)],
        compiler_params=_cparams(("arbitrary",)),
        name="merge_prompt" if is_prompt else "merge_sample",
    )(x, mod, o_mla, h_ml, og, g_heads.reshape(1, 2 * wh), w_out, g_post_mix.reshape(1, d),
      g_pre_ffn.reshape(1, d), wr_hi, wr_lo, b_router.reshape(N_EXPERTS, 1))


def _moe_kernel(be_ref, tok_ref, sh_ref, nu_ref, h2_hbm, wgu_ref, bgu_ref, wd_ref, bd_ref, y_ref,
                xbuf, sem, wgu_s, wd_s):
    blk = pl.program_id(0)
    nblk = pl.num_programs(0)
    nslots = xbuf.shape[0]
    slot = blk % nslots
    tb = y_ref.shape[0] // ROW_TILE
    de = wd_ref.shape[0]
    n_used = nu_ref[0]

    def row_copy(tok8, r, slt):
        src = h2_hbm.at[pl.ds(pl.multiple_of(tok8, ROW_TILE), ROW_TILE), :]
        return pltpu.make_async_copy(src, xbuf.at[slt, pl.ds(r * ROW_TILE, ROW_TILE), :], sem.at[slt])

    def rows_base(bk):
        return bk * tb + sh_ref[be_ref[jnp.minimum(bk, nblk - 1)]]

    def wait_rows(slt):
        pltpu.make_async_copy(h2_hbm.at[pl.ds(0, tb * ROW_TILE), :], xbuf.at[slt], sem.at[slt]).wait()

    @pl.when(blk == 0)
    def _():
        for bk in range(nslots - 1):
            base = rows_base(bk)

            def body(r, carry):
                row_copy(tok_ref[base + r], r, bk).start()
                return carry
            lax.fori_loop(0, tb, body, 0, unroll=8)

    hot = blk < n_used
    e = be_ref[blk]
    e_prev = be_ref[jnp.maximum(blk - 1, 0)]

    @pl.when(jnp.logical_and(hot, jnp.logical_or(blk == 0, e != e_prev)))
    def _():
        wgu_s[...] = wgu_ref[...].astype(BF16)
        wd_s[...] = wd_ref[...].astype(BF16)

    @pl.when(hot)
    def _():
        wait_rows(slot)
        xb = _load_row_tiles(xbuf.at[slot], tb).astype(BF16)
        ahead = blk + nslots - 1
        base = rows_base(ahead)
        for r in range(tb):
            row_copy(tok_ref[base + r], r, ahead % nslots).start()
        gu = _dot(xb, wgu_s[...]) + bgu_ref[...]
        gt = jnp.minimum(gu[:, :de], SWIGLU_LIMIT)
        up = jnp.clip(gu[:, de:], -SWIGLU_LIMIT, SWIGLU_LIMIT)
        act = gt * jax.nn.sigmoid(SWIGLU_ALPHA * gt) * (up + 1.0)
        _store_row_tiles(y_ref, _dot(act.astype(BF16), wd_s[...]) + bd_ref[...])

    @pl.when(blk == n_used)
    def _():
        for k in range(nslots - 1):
            wait_rows((blk + k) % nslots)

    @pl.when(jnp.logical_and(hot, jnp.logical_and(blk == nblk - 1, n_used == nblk)))
    def _():
        for k in range(1, nslots):
            wait_rows((blk + k) % nslots)

    @pl.when(jnp.logical_not(hot))
    def _():
        y_ref[...] = jnp.zeros(y_ref.shape, F32)


def _moe_experts(h2, block_e, tok_sorted, shift, n_used, w_gate_up, b_gate_up, w_down, b_down):
    ne, d, de2 = w_gate_up.shape
    de = w_down.shape[1]
    n_blocks = block_e.shape[0]
    tb = MOE_ROWS
    wmap = lambda i, be, tk, sh, nu: (be[i], 0, 0)
    return pl.pallas_call(
        _moe_kernel,
        out_shape=jax.ShapeDtypeStruct((n_blocks * tb * ROW_TILE, LANES), F32),
        grid_spec=pltpu.PrefetchScalarGridSpec(
            num_scalar_prefetch=4,
            grid=(n_blocks,),
            in_specs=[pl.BlockSpec(memory_space=pl.ANY),
                      pl.BlockSpec((None, d, de2), wmap),
                      pl.BlockSpec((None, 1, de2), wmap),
                      pl.BlockSpec((None, de, d), wmap),
                      pl.BlockSpec((None, 1, d), wmap)],
            out_specs=pl.BlockSpec((tb * ROW_TILE, LANES), lambda i, be, tk, sh, nu: (i, 0)),
            scratch_shapes=[pltpu.VMEM((MOE_LOOKAHEAD + 1, tb * ROW_TILE, LANES), F32),
                            pltpu.SemaphoreType.DMA((MOE_LOOKAHEAD + 1,)),
                            pltpu.VMEM((d, de2), BF16),
                            pltpu.VMEM((de, d), BF16)]),
        compiler_params=_cparams(("arbitrary",)),
        name="moe_experts",
    )(block_e, tok_sorted, shift, n_used, h2, w_gate_up, b_gate_up.reshape(ne, 1, de2), w_down,
      b_down.reshape(ne, 1, d))


def _route(top_idx, n_blocks):
    n = top_idx.shape[0]
    m = n * TOP_K
    tb = MOE_ROWS
    flat_e = top_idx.reshape(m)
    pair = jnp.arange(m, dtype=jnp.int32)
    order = lax.sort(flat_e * m + pair) % m
    tok_sorted = jnp.concatenate([order // TOP_K * ROW_TILE, jnp.zeros(((MOE_LOOKAHEAD + 1) * tb,), jnp.int32)])
    onehot = (flat_e[:, None] == jnp.arange(N_EXPERTS, dtype=jnp.int32)[None, :]).astype(jnp.int32)
    csum = jnp.cumsum(onehot, axis=0)
    counts = csum[-1]
    padded = (counts + tb - 1) // tb * tb
    pends = jnp.cumsum(padded)
    pstarts = pends - padded
    starts = jnp.cumsum(counts) - counts
    dest = (jnp.sum(onehot * (csum - 1 + pstarts[None, :]), axis=1) * ROW_TILE).astype(jnp.int32)
    blk_row = jnp.arange(n_blocks, dtype=jnp.int32) * tb
    block_e = jnp.minimum(jnp.sum((pends[None, :] <= blk_row[:, None]).astype(jnp.int32), axis=1), N_EXPERTS - 1)
    n_used = (pends[-1] // tb).astype(jnp.int32).reshape(1)
    return dest, tok_sorted, (starts - pstarts).astype(jnp.int32), block_e.astype(jnp.int32), n_used


def _combine_kernel(dest_ref, yb_hbm, x1_ref, mod_ref, gate_ref, gpost_ref, o_ref, ybuf, sem, *, base):
    i = pl.program_id(0)
    nsteps = pl.num_programs(0)
    slot = i % 2
    nb, t, d = x1_ref.shape
    tm = nb * t

    def gather(step, slt):
        def body(r, carry):
            for kk in range(TOP_K):
                row8 = pl.multiple_of(dest_ref[base + (step * tm + r) * TOP_K + kk], ROW_TILE)
                dst = ybuf.at[slt, kk, pl.ds(pl.multiple_of(r * ROW_TILE, ROW_TILE), ROW_TILE), :]
                pltpu.make_async_copy(yb_hbm.at[pl.ds(row8, ROW_TILE), :], dst, sem.at[slt]).start()
            return carry
        lax.fori_loop(0, tm, body, 0, unroll=4)

    @pl.when(i == 0)
    def _():
        gather(0, 0)

    @pl.when(i + 1 < nsteps)
    def _():
        gather(i + 1, 1 - slot)

    for kk in range(TOP_K):
        pltpu.make_async_copy(yb_hbm.at[pl.ds(0, tm * ROW_TILE), :], ybuf.at[slot, kk], sem.at[slot]).wait()

    gate = gate_ref[...]
    y = gate[:, 0:1] * _load_row_tiles(ybuf.at[slot, 0], tm)
    for kk in range(1, TOP_K):
        y = y + gate[:, kk:kk + 1] * _load_row_tiles(ybuf.at[slot, kk], tm)
    mod = mod_ref[...]
    o_ref[...] = x1_ref[...] + mod[:, 5:6, :] * _rms(y, gpost_ref[...]).reshape(nb, t, d)


def _combine(yb, dest, base, x1, mod, gates, g_post_ffn, is_prompt):
    nseq, T, d = x1.shape
    n = nseq * T
    if is_prompt:
        nb, t = 1, min(COMBINE_TILE, T)
    else:
        t = T
        nb = max(1, min(nseq, COMBINE_TILE // T))
    tm = nb * t
    tiles_per_seq = T // t
    if is_prompt:
        x_map = lambda i, dr: (i // tiles_per_seq, i % tiles_per_seq, 0)
        mod_map = lambda i, dr: (i // tiles_per_seq, 0, 0)
    else:
        x_map = lambda i, dr: (i, 0, 0)
        mod_map = lambda i, dr: (i, 0, 0)
    return pl.pallas_call(
        functools.partial(_combine_kernel, base=base),
        out_shape=jax.ShapeDtypeStruct((nseq, T, d), F32),
        grid_spec=pltpu.PrefetchScalarGridSpec(
            num_scalar_prefetch=1,
            grid=(n // tm,),
            in_specs=[pl.BlockSpec(memory_space=pl.ANY),
                      pl.BlockSpec((nb, t, d), x_map),
                      pl.BlockSpec((nb, 6, d), mod_map),
                      pl.BlockSpec((tm, TOP_K), lambda i, dr: (i, 0)),
                      pl.BlockSpec((1, d), lambda i, dr: (0, 0))],
            out_specs=pl.BlockSpec((nb, t, d), x_map),
            scratch_shapes=[pltpu.VMEM((2, TOP_K, tm * ROW_TILE, LANES), F32),
                            pltpu.SemaphoreType.DMA((2,))]),
        compiler_params=_cparams(("arbitrary",)),
        name="combine_prompt" if is_prompt else "combine_sample",
    )(dest, yb, x1, mod, gates, g_post_ffn.reshape(1, d))


def _layer(xp, xs, cache_ckv, cache_kpe, layer, page_table, st_c, st_n, st_m, c_p, c_s, w):
    nbp, S, d = xp.shape
    nbs, T, _ = xs.shape
    past_len = page_table.shape[1] * PAGE_SIZE
    dt = xp.dtype

    mod = _modulation(jnp.concatenate([c_p, c_s], axis=0), w["w_ada"], w["b_ada"])
    mod = mod.reshape(nbp + nbs, 6, d)
    mod_p, mod_s = mod[:nbp], mod[nbp:]

    pw = _prep_proj_weights(w["w_in"], w["w_q_up"], w["w_uk"], w["w_uv"], w["b_igate"], w["b_fgate"])
    chunk_p = ML_CHUNK if S % ML_CHUNK == 0 else S
    chunk_s = ML_CHUNK if T % ML_CHUNK == 0 else T

    (q, k, v, ckv_p, kpe_p, mq, mk, mv, g, cum, og_p) = _project(
        xp, mod_p, _rope_tables(jnp.arange(S)), chunk_p, w["g_pre_mix"], w["g_qlat"], w["g_kvlat"], pw, True)
    o_mla_p = _prompt_attention(q, k, v, nbp, S)
    h_ml_p, c_new_p, n_new_p, m_new_p = _mlstm(
        mq, mk, mv, g, cum, jnp.zeros((nbp, ML_HEADS, ML_V, ML_QK), F32), jnp.zeros((nbp, ML_HEADS, ML_QK), F32),
        jnp.zeros((nbp, ML_HEADS), F32), nbp, S)

    (qa, qr, ckv_s, kpe_s, mq, mk, mv, g, cum, og_s) = _project(
        xs, mod_s, _rope_tables(past_len + jnp.arange(T)), chunk_s, w["g_pre_mix"], w["g_qlat"], w["g_kvlat"],
        pw, False)
    o_mla_s = _sample_attention(qa, qr, ckv_s, kpe_s, cache_ckv, cache_kpe, layer, page_table,
                                w["w_uv"].transpose(1, 0, 2))
    h_ml_s, c_new_s, n_new_s, m_new_s = _mlstm(mq, mk, mv, g, cum, st_c, st_n, st_m, nbs, T)

    w_out_b = w["w_out"].astype(BF16)
    wr_t = w["w_router"].T
    wr_hi = wr_t.astype(BF16)
    wr_lo = (wr_t - wr_hi.astype(F32)).astype(BF16)
    margs = (w["g_heads"], w_out_b, w["g_post_mix"], w["g_pre_ffn"], wr_hi, wr_lo, w["b_router"])
    x1_p, h2_p, idx_p, gate_p = _merge(xp, mod_p, o_mla_p, h_ml_p, og_p, *margs, True)
    x1_s, h2_s, idx_s, gate_s = _merge(xs, mod_s, o_mla_s, h_ml_s, og_s, *margs, False)

    n_p, n_s = nbp * S, nbs * T
    h2 = jnp.concatenate([h2_p, h2_s], axis=0)
    top_idx = jnp.concatenate([idx_p[:TOP_K], idx_s[:TOP_K]], axis=1).T
    gates = jnp.concatenate([gate_p[:TOP_K], gate_s[:TOP_K]], axis=1).T
    m_rows = (n_p + n_s) * TOP_K
    n_blocks = -(-m_rows // MOE_ROWS) + N_EXPERTS
    dest, tok_sorted, shift, block_e, n_used = _route(top_idx, n_blocks)
    yb = _moe_experts(h2, block_e, tok_sorted, shift, n_used, w["w_gate_up"], w["b_gate_up"], w["w_down"],
                      w["b_down"])
    y_p = _combine(yb, dest, 0, x1_p, mod_p, gates[:n_p], w["g_post_ffn"], True)
    y_s = _combine(yb, dest, n_p * TOP_K, x1_s, mod_s, gates[n_p:], w["g_post_ffn"], False)

    new_p = (ckv_p.reshape(nbp, S, KV_LORA), kpe_p.reshape(nbp, S, MLA_ROPE), c_new_p.astype(dt),
             n_new_p.astype(dt), m_new_p.astype(dt))
    new_s = (ckv_s.reshape(nbs, T, KV_LORA), kpe_s.reshape(nbs, T, MLA_ROPE), c_new_s.astype(dt),
             n_new_s.astype(dt), m_new_s.astype(dt))
    return y_p, y_s, new_p, new_s


def kernel(x_prompt, x_sample, cache_ckv, cache_kpe, page_table, state_C, state_n, state_m, c_prompt, c_sample,
           w_ada, b_ada, g_pre_mix, g_post_mix, g_pre_ffn, g_post_ffn, w_in, g_qlat, w_q_up, g_kvlat, w_uk, w_uv,
           b_igate, b_fgate, g_heads, w_out, w_router, b_router, w_gate_up, b_gate_up, w_down, b_down):
    weights = dict(w_ada=w_ada, b_ada=b_ada, g_pre_mix=g_pre_mix, g_post_mix=g_post_mix, g_pre_ffn=g_pre_ffn,
                   g_post_ffn=g_post_ffn, w_in=w_in, g_qlat=g_qlat, w_q_up=w_q_up, g_kvlat=g_kvlat, w_uk=w_uk,
                   w_uv=w_uv, b_igate=b_igate, b_fgate=b_fgate, g_heads=g_heads, w_out=w_out, w_router=w_router,
                   b_router=b_router, w_gate_up=w_gate_up, b_gate_up=b_gate_up, w_down=w_down, b_down=b_down)
    depth = w_ada.shape[0]
    xp, xs = x_prompt, x_sample
    news_p, news_s = [], []
    for l in range(depth):
        wl = {name: val[l] for name, val in weights.items()}
        xp, xs, new_p, new_s = _layer(xp, xs, cache_ckv, cache_kpe, l, page_table, state_C[l], state_n[l],
                                      state_m[l], c_prompt, c_sample, wl)
        news_p.append(new_p)
        news_s.append(new_s)
    stack = lambda items, j: jnp.stack([it[j] for it in items])
    return ((xp, xs) + tuple(stack(news_p, j) for j in range(5)) + tuple(stack(news_s, j) for j in range(5)))
```

```python
import functools
import math

import jax
import jax.numpy as jnp
from jax import lax
from jax.experimental import pallas as pl
from jax.experimental.pallas import tpu as pltpu

F32 = jnp.float32
BF16 = jnp.bfloat16

HEAD_OUT = 128
MLA_HEADS = 4
ML_HEADS = 4
MLA_NOPE = 128
MLA_ROPE = 64
Q_LORA = 384
KV_LORA = 256
ROPE_BASE = 10000.0
SM_SCALE = (MLA_NOPE + MLA_ROPE) ** -0.5
ML_QK = 64
ML_V = 128
ML_CHUNK = 64
IGATE_CAP = 15.0
N_EXPERTS = 32
TOP_K = 4
SWIGLU_LIMIT = 7.0
SWIGLU_ALPHA = 1.702
EPS = 1e-6
PAGE_SIZE = 128

LANES = 128
LOG2E = 1.4426950408889634
QK_SCALE = SM_SCALE * LOG2E
NEG_INF = float("-inf")

TOKEN_TILE = 512
ATTN_BLOCK = 512
PAGES_PER_STEP = 64
PAGE_SLOTS = 3
ATTN_SUB = 1024
MOE_ROWS = 512
MOE_LOOKAHEAD = 2
COMBINE_TILE = 128
VMEM_LIMIT = 56 * 1024 * 1024

NT_DIMS = (((1,), (1,)), ((), ()))
TN_DIMS = (((0,), (0,)), ((), ()))


def _cparams(sem):
    return pltpu.CompilerParams(dimension_semantics=sem, vmem_limit_bytes=VMEM_LIMIT)


def _rms(x, g):
    return x * lax.rsqrt(jnp.mean(x * x, axis=-1, keepdims=True) + EPS) * g


def _dot(a, b):
    return jnp.dot(a, b, preferred_element_type=F32)


ROW_TILE = 8


def _store_row_tiles(ref, val):
    rows = val.shape[0]
    for c in range(ROW_TILE):
        ref[pl.ds(c, rows, stride=ROW_TILE), :] = val[:, c * LANES:(c + 1) * LANES]


def _load_row_tiles(ref, rows):
    return jnp.concatenate([ref[pl.ds(c, rows, stride=ROW_TILE), :] for c in range(ROW_TILE)], axis=1)


def _mod_kernel(c_ref, w_ref, b_ref, o_ref):
    c = c_ref[...]
    s = c * jax.nn.sigmoid(c)
    o_ref[...] = _dot(s.astype(BF16), w_ref[...].astype(BF16)) + b_ref[...]


def _modulation(c, w_ada, b_ada):
    nseq, d = c.shape
    n_out = w_ada.shape[1]
    return pl.pallas_call(
        _mod_kernel,
        out_shape=jax.ShapeDtypeStruct((nseq, n_out), F32),
        grid=(n_out // d,),
        in_specs=[pl.BlockSpec((nseq, d), lambda j: (0, 0)),
                  pl.BlockSpec((d, d), lambda j: (0, j)),
                  pl.BlockSpec((1, d), lambda j: (0, j))],
        out_specs=pl.BlockSpec((nseq, d), lambda j: (0, j)),
        compiler_params=_cparams(("arbitrary",)),
        name="modulation",
    )(c, w_ada, b_ada.reshape(1, n_out))


_C_QLAT = 0
_C_CKV = _C_QLAT + Q_LORA
_C_KPE = _C_CKV + KV_LORA
_C_KROT = _C_KPE + LANES
_C_MQ = _C_KROT + LANES
_C_MK = _C_MQ + ML_HEADS * ML_QK
_C_MV = _C_MK + ML_HEADS * ML_QK
_C_OG = _C_MV + ML_HEADS * ML_V
_C_GATE = _C_OG + ML_HEADS * ML_V
_C_END = _C_GATE + LANES


def _rot_half_cols(w):
    half = w.shape[-1] // 2
    return jnp.concatenate([-w[..., half:], w[..., :half]], axis=-1)


def _prep_proj_weights(w_in, w_q_up, w_uk, w_uv, b_igate, b_fgate):
    d = w_in.shape[0]
    o = 0
    parts = {}
    for name, size in (("qlat", Q_LORA), ("ckv", KV_LORA), ("kpe", MLA_ROPE), ("mq", ML_HEADS * ML_QK),
                       ("mk", ML_HEADS * ML_QK), ("mv", ML_HEADS * ML_V), ("ig", ML_HEADS), ("fg", ML_HEADS),
                       ("og", ML_HEADS * ML_V)):
        parts[name] = w_in[:, o:o + size]
        o += size
    z64 = jnp.zeros((d, LANES - MLA_ROPE), F32)
    zg = jnp.zeros((d, LANES - 2 * ML_HEADS), F32)
    w_in_r = jnp.concatenate(
        [parts["qlat"], parts["ckv"], parts["kpe"], z64, _rot_half_cols(parts["kpe"]), z64,
         parts["mq"], parts["mk"], parts["mv"], parts["og"], parts["ig"], parts["fg"], zg], axis=1).astype(BF16)
    b_gate = jnp.concatenate([b_igate, b_fgate, jnp.zeros((LANES - 2 * ML_HEADS,), F32)]).reshape(1, LANES)
    w_nope = w_q_up[:, :, :MLA_NOPE].reshape(Q_LORA, MLA_HEADS * MLA_NOPE)
    w_rope = w_q_up[:, :, MLA_NOPE:]
    zr = jnp.zeros((Q_LORA, MLA_HEADS, LANES - MLA_ROPE), F32)
    w_qr = jnp.concatenate([w_rope, zr], axis=-1).reshape(Q_LORA, MLA_HEADS * LANES)
    w_qt = jnp.concatenate([_rot_half_cols(w_rope), zr], axis=-1).reshape(Q_LORA, MLA_HEADS * LANES)
    w_uk2 = w_uk.reshape(KV_LORA, MLA_HEADS * MLA_NOPE).astype(BF16)
    w_uv2 = w_uv.reshape(KV_LORA, MLA_HEADS * HEAD_OUT).astype(BF16)
    w_ukT = jnp.transpose(w_uk, (1, 2, 0)).astype(BF16)
    return dict(w_in=w_in_r, b_gate=b_gate, w_qn=w_nope.astype(BF16), w_qr=w_qr.astype(BF16),
                w_qt=w_qt.astype(BF16), w_uk=w_uk2, w_uv=w_uv2, w_ukT=w_ukT)


def _rope_tables(pos):
    half = MLA_ROPE // 2
    inv_freq = ROPE_BASE ** (-jnp.arange(half, dtype=F32) / half)
    ang = pos.astype(F32)[:, None] * inv_freq[None, :]
    z = jnp.zeros((pos.shape[0], LANES - MLA_ROPE), F32)
    cos = jnp.concatenate([jnp.cos(ang), jnp.cos(ang), z], axis=1)
    sin = jnp.concatenate([jnp.sin(ang), jnp.sin(ang), z], axis=1)
    return cos, sin


def _chunk_tril(tm, chunk):
    r = jnp.arange(tm)
    same = (r[:, None] // chunk) == (r[None, :] // chunk)
    return (same & (r[None, :] <= r[:, None])).astype(BF16)


def _split3(x):
    hi = x.astype(BF16)
    r1 = x - hi.astype(F32)
    mid = r1.astype(BF16)
    lo = (r1 - mid.astype(F32)).astype(BF16)
    return hi, mid, lo


def _proj_kernel(*refs, is_prompt):
    (x_ref, mod_ref, gpre_ref, win_ref, bg_ref, gq_ref, wqn_ref, wqr_ref, wqt_ref, gkv_ref,
     cos_ref, sin_ref, tril_ref) = refs[:13]
    if is_prompt:
        wuk_ref, wuv_ref = refs[13:15]
        q_ref, k_ref, v_ref, ckv_ref, kpe_ref, mq_ref, mk_ref, mv_ref, g_ref, cum_ref, og_ref = refs[15:]
    else:
        wukT_ref = refs[13]
        qa_ref, qr_ref, ckv_ref, kpe_ref, mq_ref, mk_ref, mv_ref, g_ref, cum_ref, og_ref = refs[14:]

    x = x_ref[...]
    mod = mod_ref[...]
    nb, t, d = x.shape
    h = _rms(x, gpre_ref[...]) * (1.0 + mod[:, 1:2, :]) + mod[:, 0:1, :]
    hb = h.reshape(nb * t, d).astype(BF16)

    def zcols(a, b):
        return _dot(hb, win_ref[:, a:b])

    cos = cos_ref[...]
    sin = sin_ref[...]

    qn = _rms(zcols(_C_QLAT, _C_CKV), gq_ref[...]).astype(BF16)
    q_nope = _dot(qn, wqn_ref[...]) * QK_SCALE
    q_r = _dot(qn, wqr_ref[...])
    q_t = _dot(qn, wqt_ref[...])
    ckv = _rms(zcols(_C_CKV, _C_KPE), gkv_ref[...])
    ckv_ref[...] = ckv
    kpe = zcols(_C_KPE, _C_KROT) * cos + zcols(_C_KROT, _C_MQ) * sin
    kpe_ref[...] = kpe[:, :MLA_ROPE]
    for hd in range(MLA_HEADS):
        sl = slice(hd * LANES, (hd + 1) * LANES)
        roped = (q_r[:, sl] * cos + q_t[:, sl] * sin) * QK_SCALE
        if is_prompt:
            q_ref[:, 2 * hd * LANES:(2 * hd + 1) * LANES] = q_nope[:, sl].astype(BF16)
            q_ref[:, (2 * hd + 1) * LANES:(2 * hd + 2) * LANES] = roped.astype(BF16)
        else:
            qa_ref[:, hd * KV_LORA:(hd + 1) * KV_LORA] = _dot(q_nope[:, sl].astype(BF16), wukT_ref[hd])
            qr_ref[:, sl] = roped
    if is_prompt:
        cb = ckv.astype(BF16)
        k_nope = _dot(cb, wuk_ref[...])
        kpe_b = kpe.astype(BF16)
        for hd in range(MLA_HEADS):
            k_ref[:, 2 * hd * LANES:(2 * hd + 1) * LANES] = k_nope[:, hd * LANES:(hd + 1) * LANES].astype(BF16)
            k_ref[:, (2 * hd + 1) * LANES:(2 * hd + 2) * LANES] = kpe_b
        v_ref[...] = _dot(cb, wuv_ref[...]).astype(BF16)

    mq_ref[...] = zcols(_C_MQ, _C_MK) * (ML_QK ** -0.5)
    mk_ref[...] = zcols(_C_MK, _C_MV)
    mv_ref[...] = zcols(_C_MV, _C_OG)
    og_ref[...] = jax.nn.sigmoid(zcols(_C_OG, _C_GATE))
    zg = zcols(_C_GATE, _C_END) + bg_ref[...]
    ig = IGATE_CAP * jnp.tanh(zg / IGATE_CAP)
    lf = -(jnp.maximum(-zg, 0.0) + jnp.log1p(jnp.exp(-jnp.abs(zg))))
    lane = lax.broadcasted_iota(jnp.int32, zg.shape, 1)
    g = jnp.where(lane < ML_HEADS, ig, lf)
    g_ref[...] = g
    tril = tril_ref[...]
    hi, mid, lo = _split3(g)
    cum_ref[...] = _dot(tril, hi) + _dot(tril, mid) + _dot(tril, lo)


def _project(x, mod, pos_tab, chunk, g_pre, g_qlat, g_kvlat, pw, is_prompt):
    nseq, T, d = x.shape
    n = nseq * T
    if is_prompt:
        nb, t = 1, min(TOKEN_TILE, T)
    else:
        t = T
        nb = max(1, min(nseq, TOKEN_TILE // T))
    assert T % t == 0 and nseq % nb == 0
    tm = nb * t
    tiles_per_seq = T // t
    grid = (n // tm,)
    cos, sin = pos_tab
    if is_prompt:
        x_map = lambda i: (i // tiles_per_seq, i % tiles_per_seq, 0)
        mod_map = lambda i: (i // tiles_per_seq, 0, 0)
        tab_map = lambda i: (i % tiles_per_seq, 0)
    else:
        x_map = lambda i: (i, 0, 0)
        mod_map = lambda i: (i, 0, 0)
        tab_map = lambda i: (0, 0)
        cos = jnp.tile(cos, (nb, 1))
        sin = jnp.tile(sin, (nb, 1))
    const = lambda i: (0, 0)
    row = lambda i: (i, 0)
    tril = _chunk_tril(tm, chunk)
    in_specs = [
        pl.BlockSpec((nb, t, d), x_map),
        pl.BlockSpec((nb, 6, d), mod_map),
        pl.BlockSpec((1, d), const),
        pl.BlockSpec((d, _C_END), const),
        pl.BlockSpec((1, LANES), const),
        pl.BlockSpec((1, Q_LORA), const),
        pl.BlockSpec((Q_LORA, MLA_HEADS * MLA_NOPE), const),
        pl.BlockSpec((Q_LORA, MLA_HEADS * LANES), const),
        pl.BlockSpec((Q_LORA, MLA_HEADS * LANES), const),
        pl.BlockSpec((1, KV_LORA), const),
        pl.BlockSpec((tm, LANES), tab_map),
        pl.BlockSpec((tm, LANES), tab_map),
        pl.BlockSpec((tm, tm), const),
    ]
    args = [x, mod, g_pre.reshape(1, d), pw["w_in"], pw["b_gate"], g_qlat.reshape(1, Q_LORA), pw["w_qn"],
            pw["w_qr"], pw["w_qt"], g_kvlat.reshape(1, KV_LORA), cos, sin, tril]

    def out(cols, dtype=F32):
        return jax.ShapeDtypeStruct((n, cols), dtype), pl.BlockSpec((tm, cols), row)

    if is_prompt:
        in_specs += [pl.BlockSpec((KV_LORA, MLA_HEADS * MLA_NOPE), const),
                     pl.BlockSpec((KV_LORA, MLA_HEADS * HEAD_OUT), const)]
        args += [pw["w_uk"], pw["w_uv"]]
        outs = [out(MLA_HEADS * 2 * LANES, BF16), out(MLA_HEADS * 2 * LANES, BF16), out(MLA_HEADS * HEAD_OUT, BF16)]
    else:
        in_specs += [pl.BlockSpec((MLA_HEADS, MLA_NOPE, KV_LORA), lambda i: (0, 0, 0))]
        args += [pw["w_ukT"]]
        outs = [out(MLA_HEADS * KV_LORA), out(MLA_HEADS * LANES)]
    outs += [out(KV_LORA), out(MLA_ROPE), out(ML_HEADS * ML_QK), out(ML_HEADS * ML_QK), out(ML_HEADS * ML_V),
             out(LANES), out(LANES), out(ML_HEADS * ML_V)]
    return pl.pallas_call(
        functools.partial(_proj_kernel, is_prompt=is_prompt),
        out_shape=[o[0] for o in outs],
        grid=grid,
        in_specs=in_specs,
        out_specs=[o[1] for o in outs],
        compiler_params=_cparams(("arbitrary",)),
        name="proj_prompt" if is_prompt else "proj_sample",
    )(*args)


def _attn_kernel(qi_ref, ki_ref, q_ref, k_ref, v_ref, o_ref, m_ref, l_ref, acc_ref):
    p = pl.program_id(1)
    qi = qi_ref[p]
    ki = ki_ref[p]
    tq = q_ref.shape[0]
    tk = k_ref.shape[0]

    @pl.when(ki == 0)
    def _():
        m_ref[...] = jnp.full(m_ref.shape, NEG_INF, F32)
        l_ref[...] = jnp.zeros(l_ref.shape, F32)
        acc_ref[...] = jnp.zeros(acc_ref.shape, F32)

    def update(masked):
        for hd in range(MLA_HEADS):
            qh = q_ref[:, hd * 2 * LANES:(hd + 1) * 2 * LANES]
            kh = k_ref[:, hd * 2 * LANES:(hd + 1) * 2 * LANES]
            s = lax.dot_general(qh, kh, NT_DIMS, preferred_element_type=F32)
            if masked:
                row = lax.broadcasted_iota(jnp.int32, s.shape, 0)
                col = lax.broadcasted_iota(jnp.int32, s.shape, 1)
                s = jnp.where(col <= row, s, NEG_INF)
            m_prev = m_ref[hd]
            m_next = jnp.maximum(m_prev, jnp.max(s, axis=1, keepdims=True))
            alpha = jnp.exp2(m_prev - m_next)
            pr = jnp.exp2(s - jnp.concatenate([m_next] * (tk // LANES), axis=1))
            l_ref[hd] = alpha * l_ref[hd] + jnp.sum(pr, axis=1, keepdims=True)
            acc_ref[hd] = alpha * acc_ref[hd] + _dot(pr.astype(BF16), v_ref[:, hd * HEAD_OUT:(hd + 1) * HEAD_OUT])
            m_ref[hd] = m_next

    @pl.when(ki < qi)
    def _():
        update(False)

    @pl.when(ki == qi)
    def _():
        update(True)
        for hd in range(MLA_HEADS):
            o_ref[:, hd * HEAD_OUT:(hd + 1) * HEAD_OUT] = acc_ref[hd] / l_ref[hd]


def _prompt_attention(q, k, v, nseq, S):
    blk = min(ATTN_BLOCK, S)
    nq = S // blk
    pairs = [(i, j) for i in range(nq) for j in range(i + 1)]
    qi_tab = jnp.asarray([p[0] for p in pairs], jnp.int32)
    ki_tab = jnp.asarray([p[1] for p in pairs], jnp.int32)
    wq = MLA_HEADS * 2 * LANES
    wv = MLA_HEADS * HEAD_OUT
    return pl.pallas_call(
        _attn_kernel,
        out_shape=jax.ShapeDtypeStruct((nseq * S, wv), F32),
        grid_spec=pltpu.PrefetchScalarGridSpec(
            num_scalar_prefetch=2,
            grid=(nseq, len(pairs)),
            in_specs=[pl.BlockSpec((blk, wq), lambda b, p, qi, ki: (b * nq + qi[p], 0)),
                      pl.BlockSpec((blk, wq), lambda b, p, qi, ki: (b * nq + ki[p], 0)),
                      pl.BlockSpec((blk, wv), lambda b, p, qi, ki: (b * nq + ki[p], 0))],
            out_specs=pl.BlockSpec((blk, wv), lambda b, p, qi, ki: (b * nq + qi[p], 0)),
            scratch_shapes=[pltpu.VMEM((MLA_HEADS, blk, LANES), F32),
                            pltpu.VMEM((MLA_HEADS, blk, LANES), F32),
                            pltpu.VMEM((MLA_HEADS, blk, HEAD_OUT), F32)]),
        compiler_params=_cparams(("arbitrary", "arbitrary")),
        name="prompt_attention",
    )(qi_tab, ki_tab, q, k, v)


def _sattn_kernel(pt_ref, qa_ref, qr_ref, ckvn_ref, kpen_ref, wuv_ref, cckv_hbm, ckpe_hbm, o_ref,
                  ckv_buf, kpe_buf, sem, kcb, wq_s, q_s, qr_s, kn_s, pn_s, m_ref, l_ref, acc_ref,
                  *, layer, npg, nchunks, sub):
    b = pl.program_id(0)
    c = pl.program_id(1)
    step = b * nchunks + c
    nsteps = pl.num_programs(0) * nchunks
    nslots = ckv_buf.shape[0]
    slot = step % nslots
    T = qa_ref.shape[0]
    rows = MLA_HEADS * T
    npos = npg * PAGE_SIZE
    pack = wq_s.shape[0]

    def page_copies(page_of, slt):
        cps = []
        for j in range(npg):
            pg = page_of(j)
            dst = pl.ds(j * PAGE_SIZE, PAGE_SIZE)
            cps.append(pltpu.make_async_copy(cckv_hbm.at[layer, pg], ckv_buf.at[slt, dst, :], sem.at[0, slt]))
            cps.append(pltpu.make_async_copy(ckpe_hbm.at[layer, pg], kpe_buf.at[slt, :, dst], sem.at[1, slt]))
        return cps

    def fetch(stp, slt):
        base = jnp.minimum(stp, nsteps - 1) * npg
        for cp in page_copies(lambda j: pt_ref[base + j], slt):
            cp.start()

    def wait_pages(slt):
        for cp in page_copies(lambda j: 0, slt):
            cp.wait()

    @pl.when(step == 0)
    def _():
        for k in range(nslots - 1):
            fetch(k, k)

    @pl.when(c == 0)
    def _():
        m_ref[...] = jnp.full(m_ref.shape, NEG_INF, F32)
        l_ref[...] = jnp.zeros(l_ref.shape, F32)
        acc_ref[...] = jnp.zeros(acc_ref.shape, F32)
        qa = qa_ref[...]
        qr = qr_ref[...]
        q32 = jnp.concatenate([qa[:, hd * KV_LORA:(hd + 1) * KV_LORA] for hd in range(MLA_HEADS)], axis=0)
        q_s[...] = q32.astype(BF16)
        qr_s[...] = jnp.concatenate([qr[:, hd * LANES:(hd + 1) * LANES] for hd in range(MLA_HEADS)],
                                    axis=0).astype(BF16)
        qt = jnp.concatenate([q32] * pack, axis=0).T
        lane_blk = lax.broadcasted_iota(jnp.int32, qt.shape, 1) // rows
        for j in range(pack):
            wq_s[j] = jnp.where(lane_blk == j, qt, 0.0).astype(BF16)

    wait_pages(slot)

    q = q_s[...]
    qr = qr_s[:, :MLA_ROPE]

    def softmax_update(s, pv):
        m_prev = m_ref[...]
        m_next = jnp.maximum(m_prev, jnp.max(s, axis=1, keepdims=True))
        alpha = jnp.exp2(m_prev - m_next)
        pr = jnp.exp2(s - jnp.concatenate([m_next] * (s.shape[1] // LANES), axis=1))
        l_ref[...] = alpha * l_ref[...] + jnp.sum(pr, axis=1, keepdims=True)
        acc_ref[...] = jnp.concatenate([alpha] * (KV_LORA // LANES), axis=1) * acc_ref[...] + pv(pr.astype(BF16))
        m_ref[...] = m_next

    s_parts = []
    for grp in range(npos // (sub * pack)):
        st = None
        for j in range(pack):
            sb = grp * pack + j
            sl = slice(sb * sub, (sb + 1) * sub)
            kc = ckv_buf[slot, sl, :].astype(BF16)
            kcb[sl, :] = kc
            t = _dot(kc, wq_s[j])
            st = t if st is None else st + t
        s_t = st.T
        for j in range(pack):
            sb = grp * pack + j
            kp = kpe_buf[slot, :, sb * sub:(sb + 1) * sub].astype(BF16)
            s_parts.append(s_t[j * rows:(j + 1) * rows, :] + _dot(qr, kp))

    fetch(step + nslots - 1, (step + nslots - 1) % nslots)

    def pv_chunk(p):
        parts = []
        for sb in range(npos // sub):
            psb = p[:, sb * sub:(sb + 1) * sub]
            parts.append(jnp.concatenate(
                [_dot(psb, kcb[sb * sub:(sb + 1) * sub, hh * LANES:(hh + 1) * LANES])
                 for hh in range(KV_LORA // LANES)], axis=1))
        while len(parts) > 1:
            parts = [parts[i] + parts[i + 1] for i in range(0, len(parts) - 1, 2)] + parts[len(parts) & ~1:]
        return parts[0]

    softmax_update(jnp.concatenate(s_parts, axis=1), pv_chunk)

    @pl.when(c == nchunks - 1)
    def _():
        kn_s[...] = jnp.zeros(kn_s.shape, F32)
        pn_s[...] = jnp.zeros(pn_s.shape, F32)
        kn_s[0:T, :] = ckvn_ref[...]
        pn_s[0:T, :] = kpen_ref[...]
        row = lax.broadcasted_iota(jnp.int32, (rows, LANES), 0)
        col = lax.broadcasted_iota(jnp.int32, (rows, LANES), 1)
        kn = kn_s[...].astype(BF16)
        s_new = (lax.dot_general(q, kn, NT_DIMS, preferred_element_type=F32)
                 + lax.dot_general(qr, pn_s[...].astype(BF16), NT_DIMS, preferred_element_type=F32))
        softmax_update(jnp.where(col <= row % T, s_new, NEG_INF), lambda p: _dot(p, kn))
        o_lat = acc_ref[...] / jnp.concatenate([l_ref[...]] * (KV_LORA // LANES), axis=1)
        for hd in range(MLA_HEADS):
            o_ref[:, hd * HEAD_OUT:(hd + 1) * HEAD_OUT] = _dot(o_lat[hd * T:(hd + 1) * T, :], wuv_ref[hd])

    @pl.when(step == nsteps - 1)
    def _():
        for k in range(1, nslots):
            wait_pages((step + k) % nslots)


def _sample_attention(qa, qr, ckv_new, kpe_new, cache_ckv, cache_kpe, layer, page_table, w_uv):
    nseq, n_pages = page_table.shape
    T = qa.shape[0] // nseq
    npg = min(PAGES_PER_STEP, n_pages)
    assert n_pages % npg == 0
    nchunks = n_pages // npg
    rows = MLA_HEADS * T
    assert LANES % rows == 0
    pack = LANES // rows
    sub = min(ATTN_SUB, npg * PAGE_SIZE // pack)
    assert (npg * PAGE_SIZE) % (sub * pack) == 0
    seq = lambda b, c, pt: (b, 0)
    kern = functools.partial(_sattn_kernel, layer=layer, npg=npg, nchunks=nchunks, sub=sub)
    return pl.pallas_call(
        kern,
        out_shape=jax.ShapeDtypeStruct((nseq * T, MLA_HEADS * HEAD_OUT), F32),
        grid_spec=pltpu.PrefetchScalarGridSpec(
            num_scalar_prefetch=1,
            grid=(nseq, nchunks),
            in_specs=[pl.BlockSpec((T, MLA_HEADS * KV_LORA), seq),
                      pl.BlockSpec((T, MLA_HEADS * LANES), seq),
                      pl.BlockSpec((T, KV_LORA), seq),
                      pl.BlockSpec((T, MLA_ROPE), seq),
                      pl.BlockSpec((MLA_HEADS, KV_LORA, HEAD_OUT), lambda b, c, pt: (0, 0, 0)),
                      pl.BlockSpec(memory_space=pl.ANY),
                      pl.BlockSpec(memory_space=pl.ANY)],
            out_specs=pl.BlockSpec((T, MLA_HEADS * HEAD_OUT), seq),
            scratch_shapes=[pltpu.VMEM((PAGE_SLOTS, npg * PAGE_SIZE, KV_LORA), F32),
                            pltpu.VMEM((PAGE_SLOTS, MLA_ROPE, npg * PAGE_SIZE), F32),
                            pltpu.SemaphoreType.DMA((2, PAGE_SLOTS)),
                            pltpu.VMEM((npg * PAGE_SIZE, KV_LORA), BF16),
                            pltpu.VMEM((pack, KV_LORA, LANES), BF16),
                            pltpu.VMEM((rows, KV_LORA), BF16),
                            pltpu.VMEM((rows, LANES), BF16),
                            pltpu.VMEM((LANES, KV_LORA), F32),
                            pltpu.VMEM((LANES, MLA_ROPE), F32),
                            pltpu.VMEM((rows, LANES), F32),
                            pltpu.VMEM((rows, LANES), F32),
                            pltpu.VMEM((rows, KV_LORA), F32)]),
        compiler_params=_cparams(("arbitrary", "arbitrary")),
        name="sample_attention",
    )(page_table.reshape(-1), qa, qr, ckv_new, kpe_new, w_uv, cache_ckv, jnp.swapaxes(cache_kpe, 2, 3))


ML_HK = ML_HEADS * ML_QK


def _rows_per_head(vals, rows):
    return jnp.concatenate([jnp.broadcast_to(v, (rows, v.shape[1])) for v in vals], axis=0)


def _head_block_mask(nrows, row_blk, ncols, col_blk):
    r = lax.broadcasted_iota(jnp.int32, (nrows, ncols), 0) // row_blk
    c = lax.broadcasted_iota(jnp.int32, (nrows, ncols), 1) // col_blk
    return r == c


def _mlstm_kernel(mq_ref, mk_ref, mv_ref, g_ref, cum_ref, s0_ref, n0_ref, m0_ref, h_ref, st_ref, n_ref, m_ref,
                  *, unroll_seqs, dot_dtype):
    ci = pl.program_id(1)
    nbs, L = mq_ref.shape[0], mq_ref.shape[1]
    H = ML_HEADS
    R = H * L

    @pl.when(ci == 0)
    def _():
        st_ref[...] = s0_ref[...]
        n_ref[...] = n0_ref[...]
        m_ref[...] = m0_ref[...]

    trow = lax.broadcasted_iota(jnp.int32, (R, L), 0) % L
    tcol = lax.broadcasted_iota(jnp.int32, (R, L), 1)
    causal = tcol <= trow
    q_mask = _head_block_mask(R, L, ML_HK, ML_QK)
    s_mask = _head_block_mask(R, L, R, L)
    k_mask = _head_block_mask(ML_HK, ML_QK, R, L)

    def one_seq(i):
        g = g_ref[i]
        cum = cum_ref[i]
        g_t = g.T
        cum_t = cum.T
        mq = mq_ref[i]
        mk = mk_ref[i]
        mv = mv_ref[i]
        mprev = m_ref[i]
        ig_col = jnp.concatenate([g[:, h:h + 1] for h in range(H)], axis=0)
        b_col = jnp.concatenate([cum[:, H + h:H + h + 1] for h in range(H)], axis=0)
        ig_row = _rows_per_head([g_t[h:h + 1, :] for h in range(H)], L)
        b_row = _rows_per_head([cum_t[H + h:H + h + 1, :] for h in range(H)], L)
        m_prev = _rows_per_head([mprev[h:h + 1, 0:1] for h in range(H)], L)
        b_last = _rows_per_head([cum[L - 1:L, H + h:H + h + 1] for h in range(H)], L)

        log_d = jnp.where(causal, b_col - b_row + ig_row, NEG_INF)
        log_inter = b_col + m_prev
        m_t = jnp.maximum(log_inter, jnp.max(log_d, axis=1, keepdims=True))
        d = jnp.exp(log_d - m_t)
        inter = jnp.exp(log_inter - m_t)

        q_blk = jnp.where(q_mask, jnp.concatenate([mq] * H, axis=0), 0.0)
        k_t = mk.T
        s = _dot(q_blk.astype(dot_dtype), k_t.astype(dot_dtype)) * d
        s_bd = jnp.where(s_mask, jnp.concatenate([s] * H, axis=1), 0.0)
        v_stack = jnp.concatenate([mv[:, h * ML_V:(h + 1) * ML_V] for h in range(H)], axis=0)
        st = st_ref[i]
        n_row = n_ref[i]
        num = (_dot((inter * q_blk).astype(dot_dtype), st.astype(dot_dtype))
               + _dot(s_bd.astype(dot_dtype), v_stack.astype(dot_dtype)))
        den = inter * jnp.sum(q_blk * n_row, axis=1, keepdims=True) + jnp.sum(s, axis=1, keepdims=True)
        h_all = num / jnp.maximum(jnp.abs(den), jnp.exp(-m_t))
        for h in range(H):
            h_ref[i, :, h * ML_V:(h + 1) * ML_V] = h_all[h * L:(h + 1) * L, :]

        m_last = [m_t[(h + 1) * L - 1:(h + 1) * L, :] for h in range(H)]
        m_new = _rows_per_head(m_last, L)
        w_col = jnp.exp(b_last - b_col + ig_col - m_new)
        decay = jnp.exp(b_last + m_prev - m_new)
        kt_blk = jnp.where(k_mask, jnp.concatenate([k_t] * H, axis=1), 0.0)
        decay_k = _rows_per_head([decay[h * L:h * L + 1, :] for h in range(H)], ML_QK)
        st_ref[i] = decay_k * st + _dot(kt_blk.astype(dot_dtype), (w_col * v_stack).astype(dot_dtype))

        def widen(col):
            return jnp.concatenate([jnp.broadcast_to(col[h * L:(h + 1) * L, :], (L, ML_QK)) for h in range(H)],
                                   axis=1)
        n_ref[i] = widen(decay)[0:1, :] * n_row + jnp.sum(widen(w_col) * mk, axis=0, keepdims=True)
        m_ref[i] = jnp.concatenate([jnp.broadcast_to(ml, (1, LANES)) for ml in m_last], axis=0)

    if unroll_seqs:
        for i in range(nbs):
            one_seq(i)
    else:
        def body(i, carry):
            one_seq(i)
            return carry
        lax.fori_loop(0, nbs, body, 0)


def _mlstm(mq, mk, mv, g, cum, c0, n0, m0, nseq, T):
    L = ML_CHUNK if T % ML_CHUNK == 0 else T
    nc = T // L
    nbs = nseq if nseq <= 2 else 16
    assert nseq % nbs == 0
    r3 = lambda a: a.reshape(nseq, T, a.shape[-1])
    tok = lambda gi, ci: (gi, ci, 0)
    st3 = lambda gi, ci: (gi, 0, 0)
    ct0 = jnp.swapaxes(c0.astype(F32), 2, 3).reshape(nseq, ML_HK, ML_V)
    n0r = n0.astype(F32).reshape(nseq, 1, ML_HK)
    m0b = jnp.broadcast_to(m0.astype(F32)[:, :, None], (nseq, ML_HEADS, LANES))
    kern = functools.partial(_mlstm_kernel, unroll_seqs=nbs <= 2, dot_dtype=BF16 if L >= 16 else F32)
    h, st, n, m = pl.pallas_call(
        kern,
        out_shape=[jax.ShapeDtypeStruct((nseq, T, ML_HEADS * ML_V), F32),
                   jax.ShapeDtypeStruct((nseq, ML_HK, ML_V), F32),
                   jax.ShapeDtypeStruct((nseq, 1, ML_HK), F32),
                   jax.ShapeDtypeStruct((nseq, ML_HEADS, LANES), F32)],
        grid=(nseq // nbs, nc),
        in_specs=[pl.BlockSpec((nbs, L, ML_HK), tok),
                  pl.BlockSpec((nbs, L, ML_HK), tok),
                  pl.BlockSpec((nbs, L, ML_HEADS * ML_V), tok),
                  pl.BlockSpec((nbs, L, LANES), tok),
                  pl.BlockSpec((nbs, L, LANES), tok),
                  pl.BlockSpec((nbs, ML_HK, ML_V), st3),
                  pl.BlockSpec((nbs, 1, ML_HK), st3),
                  pl.BlockSpec((nbs, ML_HEADS, LANES), st3)],
        out_specs=[pl.BlockSpec((nbs, L, ML_HEADS * ML_V), tok),
                   pl.BlockSpec((nbs, ML_HK, ML_V), st3),
                   pl.BlockSpec((nbs, 1, ML_HK), st3),
                   pl.BlockSpec((nbs, ML_HEADS, LANES), st3)],
        compiler_params=_cparams(("arbitrary", "arbitrary")),
        name="mlstm_prompt" if nseq <= 2 else "mlstm_sample",
    )(r3(mq), r3(mk), r3(mv), r3(g), r3(cum), ct0, n0r, m0b)
    c_new = jnp.swapaxes(st.reshape(nseq, ML_HEADS, ML_QK, ML_V), 2, 3)
    return h.reshape(nseq * T, ML_HEADS * ML_V), c_new, n.reshape(nseq, ML_HEADS, ML_QK), m[:, :, 0]


def _merge_kernel(x_ref, mod_ref, omla_ref, hml_ref, og_ref, gh_ref, wout_ref, gpm_ref, gpf_ref,
                  wrh_ref, wrl_ref, br_ref, x1_ref, h2_ref, idx_ref, gate_ref):
    nb, t, d = x_ref.shape
    tm = nb * t
    gh = gh_ref[...]
    parts = []
    for hd in range(MLA_HEADS + ML_HEADS):
        if hd < MLA_HEADS:
            xh = omla_ref[:, hd * HEAD_OUT:(hd + 1) * HEAD_OUT]
        else:
            j = hd - MLA_HEADS
            xh = hml_ref[:, j * HEAD_OUT:(j + 1) * HEAD_OUT]
        yh = _rms(xh, gh[:, hd * HEAD_OUT:(hd + 1) * HEAD_OUT])
        if hd >= MLA_HEADS:
            yh = yh * og_ref[:, j * HEAD_OUT:(j + 1) * HEAD_OUT]
        parts.append(yh.astype(BF16))
    merged = jnp.concatenate(parts, axis=1)
    mix = _dot(merged, wout_ref[...])
    mod = mod_ref[...]
    x1 = x_ref[...] + mod[:, 2:3, :] * _rms(mix, gpm_ref[...]).reshape(nb, t, d)
    x1_ref[...] = x1
    h2 = (_rms(x1, gpf_ref[...]) * (1.0 + mod[:, 4:5, :]) + mod[:, 3:4, :]).reshape(tm, d)
    _store_row_tiles(h2_ref, h2)
    hh = h2.astype(BF16)
    hl = (h2 - hh.astype(F32)).astype(BF16)
    wrh = wrh_ref[...]
    logits = (lax.dot_general(wrh, hh, NT_DIMS, preferred_element_type=F32)
              + lax.dot_general(wrh, hl, NT_DIMS, preferred_element_type=F32)
              + lax.dot_general(wrl_ref[...], hh, NT_DIMS, preferred_element_type=F32)) + br_ref[...]
    eidx = lax.broadcasted_iota(jnp.int32, logits.shape, 0)
    work = logits
    vals, idxs = [], []
    for _ in range(TOP_K):
        mx = jnp.max(work, axis=0, keepdims=True)
        sel = jnp.min(jnp.where(work == mx, eidx, N_EXPERTS), axis=0, keepdims=True)
        vals.append(mx)
        idxs.append(sel)
        work = jnp.where(eidx == sel, NEG_INF, work)
    ex = [jnp.exp(v - vals[0]) for v in vals]
    tot = ex[0] + ex[1] + ex[2] + ex[3]
    zi = jnp.zeros((8 - TOP_K, tm), jnp.int32)
    zf = jnp.zeros((8 - TOP_K, tm), F32)
    idx_ref[...] = jnp.concatenate(idxs + [zi], axis=0)
    gate_ref[...] = jnp.concatenate([e / tot for e in ex] + [zf], axis=0)


def _merge(x, mod, o_mla, h_ml, og, g_heads, w_out, g_post_mix, g_pre_ffn, wr_hi, wr_lo, b_router, is_prompt):
    nseq, T, d = x.shape
    n = nseq * T
    if is_prompt:
        nb, t = 1, min(TOKEN_TILE, T)
    else:
        t = T
        nb = max(1, min(nseq, TOKEN_TILE // T))
    tm = nb * t
    tiles_per_seq = T // t
    if is_prompt:
        x_map = lambda i: (i // tiles_per_seq, i % tiles_per_seq, 0)
        mod_map = lambda i: (i // tiles_per_seq, 0, 0)
    else:
        x_map = lambda i: (i, 0, 0)
        mod_map = lambda i: (i, 0, 0)
    const = lambda i: (0, 0)
    row = lambda i: (i, 0)
    col = lambda i: (0, i)
    wh = MLA_HEADS * HEAD_OUT
    return pl.pallas_call(
        _merge_kernel,
        out_shape=[jax.ShapeDtypeStruct((nseq, T, d), F32), jax.ShapeDtypeStruct((n * ROW_TILE, LANES), F32),
                   jax.ShapeDtypeStruct((8, n), jnp.int32), jax.ShapeDtypeStruct((8, n), F32)],
        grid=(n // tm,),
        in_specs=[pl.BlockSpec((nb, t, d), x_map),
                  pl.BlockSpec((nb, 6, d), mod_map),
                  pl.BlockSpec((tm, wh), row),
                  pl.BlockSpec((tm, wh), row),
                  pl.BlockSpec((tm, wh), row),
                  pl.BlockSpec((1, 2 * wh), const),
                  pl.BlockSpec((2 * wh, d), const),
                  pl.BlockSpec((1, d), const),
                  pl.BlockSpec((1, d), const),
                  pl.BlockSpec((N_EXPERTS, d), const),
                  pl.BlockSpec((N_EXPERTS, d), const),
                  pl.BlockSpec((N_EXPERTS, 1), const)],
        out_specs=[pl.BlockSpec((nb, t, d), x_map), pl.BlockSpec((tm * ROW_TILE, LANES), row),
                   pl.BlockSpec((8, tm), col), pl.BlockSpec((8, tm), col)],
        compiler_params=_cparams(("arbitrary",)),
        name="merge_prompt" if is_prompt else "merge_sample",
    )(x, mod, o_mla, h_ml, og, g_heads.reshape(1, 2 * wh), w_out, g_post_mix.reshape(1, d),
      g_pre_ffn.reshape(1, d), wr_hi, wr_lo, b_router.reshape(N_EXPERTS, 1))


def _moe_kernel(be_ref, tok_ref, sh_ref, nu_ref, h2_hbm, wgu_ref, bgu_ref, wd_ref, bd_ref, y_ref,
                xbuf, sem, wgu_s, wd_s):
    blk = pl.program_id(0)
    nblk = pl.num_programs(0)
    nslots = xbuf.shape[0]
    slot = blk % nslots
    tb = y_ref.shape[0] // ROW_TILE
    de = wd_ref.shape[0]
    n_used = nu_ref[0]

    def row_copy(tok8, r, slt):
        src = h2_hbm.at[pl.ds(pl.multiple_of(tok8, ROW_TILE), ROW_TILE), :]
        return pltpu.make_async_copy(src, xbuf.at[slt, pl.ds(r * ROW_TILE, ROW_TILE), :], sem.at[slt])

    def rows_base(bk):
        return bk * tb + sh_ref[be_ref[jnp.minimum(bk, nblk - 1)]]

    def wait_rows(slt):
        pltpu.make_async_copy(h2_hbm.at[pl.ds(0, tb * ROW_TILE), :], xbuf.at[slt], sem.at[slt]).wait()

    @pl.when(blk == 0)
    def _():
        for bk in range(nslots - 1):
            base = rows_base(bk)

            def body(r, carry):
                row_copy(tok_ref[base + r], r, bk).start()
                return carry
            lax.fori_loop(0, tb, body, 0, unroll=8)

    hot = blk < n_used
    e = be_ref[blk]
    e_prev = be_ref[jnp.maximum(blk - 1, 0)]

    @pl.when(jnp.logical_and(hot, jnp.logical_or(blk == 0, e != e_prev)))
    def _():
        wgu_s[...] = wgu_ref[...].astype(BF16)
        wd_s[...] = wd_ref[...].astype(BF16)

    @pl.when(hot)
    def _():
        wait_rows(slot)
        xb = _load_row_tiles(xbuf.at[slot], tb).astype(BF16)
        ahead = blk + nslots - 1
        base = rows_base(ahead)
        for r in range(tb):
            row_copy(tok_ref[base + r], r, ahead % nslots).start()
        gu = _dot(xb, wgu_s[...]) + bgu_ref[...]
        gt = jnp.minimum(gu[:, :de], SWIGLU_LIMIT)
        up = jnp.clip(gu[:, de:], -SWIGLU_LIMIT, SWIGLU_LIMIT)
        act = gt * jax.nn.sigmoid(SWIGLU_ALPHA * gt) * (up + 1.0)
        _store_row_tiles(y_ref, _dot(act.astype(BF16), wd_s[...]) + bd_ref[...])

    @pl.when(blk == n_used)
    def _():
        for k in range(nslots - 1):
            wait_rows((blk + k) % nslots)

    @pl.when(jnp.logical_and(hot, jnp.logical_and(blk == nblk - 1, n_used == nblk)))
    def _():
        for k in range(1, nslots):
            wait_rows((blk + k) % nslots)

    @pl.when(jnp.logical_not(hot))
    def _():
        y_ref[...] = jnp.zeros(y_ref.shape, F32)


def _moe_experts(h2, block_e, tok_sorted, shift, n_used, w_gate_up, b_gate_up, w_down, b_down):
    ne, d, de2 = w_gate_up.shape
    de = w_down.shape[1]
    n_blocks = block_e.shape[0]
    tb = MOE_ROWS
    wmap = lambda i, be, tk, sh, nu: (be[i], 0, 0)
    return pl.pallas_call(
        _moe_kernel,
        out_shape=jax.ShapeDtypeStruct((n_blocks * tb * ROW_TILE, LANES), F32),
        grid_spec=pltpu.PrefetchScalarGridSpec(
            num_scalar_prefetch=4,
            grid=(n_blocks,),
            in_specs=[pl.BlockSpec(memory_space=pl.ANY),
                      pl.BlockSpec((None, d, de2), wmap),
                      pl.BlockSpec((None, 1, de2), wmap),
                      pl.BlockSpec((None, de, d), wmap),
                      pl.BlockSpec((None, 1, d), wmap)],
            out_specs=pl.BlockSpec((tb * ROW_TILE, LANES), lambda i, be, tk, sh, nu: (i, 0)),
            scratch_shapes=[pltpu.VMEM((MOE_LOOKAHEAD + 1, tb * ROW_TILE, LANES), F32),
                            pltpu.SemaphoreType.DMA((MOE_LOOKAHEAD + 1,)),
                            pltpu.VMEM((d, de2), BF16),
                            pltpu.VMEM((de, d), BF16)]),
        compiler_params=_cparams(("arbitrary",)),
        name="moe_experts",
    )(block_e, tok_sorted, shift, n_used, h2, w_gate_up, b_gate_up.reshape(ne, 1, de2), w_down,
      b_down.reshape(ne, 1, d))


def _route(top_idx, n_blocks):
    n = top_idx.shape[0]
    m = n * TOP_K
    tb = MOE_ROWS
    flat_e = top_idx.reshape(m)
    pair = jnp.arange(m, dtype=jnp.int32)
    order = lax.sort(flat_e * m + pair) % m
    tok_sorted = jnp.concatenate([order // TOP_K * ROW_TILE, jnp.zeros(((MOE_LOOKAHEAD + 1) * tb,), jnp.int32)])
    onehot = (flat_e[:, None] == jnp.arange(N_EXPERTS, dtype=jnp.int32)[None, :]).astype(jnp.int32)
    csum = jnp.cumsum(onehot, axis=0)
    counts = csum[-1]
    padded = (counts + tb - 1) // tb * tb
    pends = jnp.cumsum(padded)
    pstarts = pends - padded
    starts = jnp.cumsum(counts) - counts
    dest = (jnp.sum(onehot * (csum - 1 + pstarts[None, :]), axis=1) * ROW_TILE).astype(jnp.int32)
    blk_row = jnp.arange(n_blocks, dtype=jnp.int32) * tb
    block_e = jnp.minimum(jnp.sum((pends[None, :] <= blk_row[:, None]).astype(jnp.int32), axis=1), N_EXPERTS - 1)
    n_used = (pends[-1] // tb).astype(jnp.int32).reshape(1)
    return dest, tok_sorted, (starts - pstarts).astype(jnp.int32), block_e.astype(jnp.int32), n_used


def _combine_kernel(dest_ref, yb_hbm, x1_ref, mod_ref, gate_ref, gpost_ref, o_ref, ybuf, sem, *, base):
    i = pl.program_id(0)
    nsteps = pl.num_programs(0)
    slot = i % 2
    nb, t, d = x1_ref.shape
    tm = nb * t

    def gather(step, slt):
        def body(r, carry):
            for kk in range(TOP_K):
                row8 = pl.multiple_of(dest_ref[base + (step * tm + r) * TOP_K + kk], ROW_TILE)
                dst = ybuf.at[slt, kk, pl.ds(pl.multiple_of(r * ROW_TILE, ROW_TILE), ROW_TILE), :]
                pltpu.make_async_copy(yb_hbm.at[pl.ds(row8, ROW_TILE), :], dst, sem.at[slt]).start()
            return carry
        lax.fori_loop(0, tm, body, 0, unroll=4)

    @pl.when(i == 0)
    def _():
        gather(0, 0)

    @pl.when(i + 1 < nsteps)
    def _():
        gather(i + 1, 1 - slot)

    for kk in range(TOP_K):
        pltpu.make_async_copy(yb_hbm.at[pl.ds(0, tm * ROW_TILE), :], ybuf.at[slot, kk], sem.at[slot]).wait()

    gate = gate_ref[...]
    y = gate[:, 0:1] * _load_row_tiles(ybuf.at[slot, 0], tm)
    for kk in range(1, TOP_K):
        y = y + gate[:, kk:kk + 1] * _load_row_tiles(ybuf.at[slot, kk], tm)
    mod = mod_ref[...]
    o_ref[...] = x1_ref[...] + mod[:, 5:6, :] * _rms(y, gpost_ref[...]).reshape(nb, t, d)


def _combine(yb, dest, base, x1, mod, gates, g_post_ffn, is_prompt):
    nseq, T, d = x1.shape
    n = nseq * T
    if is_prompt:
        nb, t = 1, min(COMBINE_TILE, T)
    else:
        t = T
        nb = max(1, min(nseq, COMBINE_TILE // T))
    tm = nb * t
    tiles_per_seq = T // t
    if is_prompt:
        x_map = lambda i, dr: (i // tiles_per_seq, i % tiles_per_seq, 0)
        mod_map = lambda i, dr: (i // tiles_per_seq, 0, 0)
    else:
        x_map = lambda i, dr: (i, 0, 0)
        mod_map = lambda i, dr: (i, 0, 0)
    return pl.pallas_call(
        functools.partial(_combine_kernel, base=base),
        out_shape=jax.ShapeDtypeStruct((nseq, T, d), F32),
        grid_spec=pltpu.PrefetchScalarGridSpec(
            num_scalar_prefetch=1,
            grid=(n // tm,),
            in_specs=[pl.BlockSpec(memory_space=pl.ANY),
                      pl.BlockSpec((nb, t, d), x_map),
                      pl.BlockSpec((nb, 6, d), mod_map),
                      pl.BlockSpec((tm, TOP_K), lambda i, dr: (i, 0)),
                      pl.BlockSpec((1, d), lambda i, dr: (0, 0))],
            out_specs=pl.BlockSpec((nb, t, d), x_map),
            scratch_shapes=[pltpu.VMEM((2, TOP_K, tm * ROW_TILE, LANES), F32),
                            pltpu.SemaphoreType.DMA((2,))]),
        compiler_params=_cparams(("arbitrary",)),
        name="combine_prompt" if is_prompt else "combine_sample",
    )(dest, yb, x1, mod, gates, g_post_ffn.reshape(1, d))


def _layer(xp, xs, cache_ckv, cache_kpe, layer, page_table, st_c, st_n, st_m, c_p, c_s, w):
    nbp, S, d = xp.shape
    nbs, T, _ = xs.shape
    past_len = page_table.shape[1] * PAGE_SIZE
    dt = xp.dtype

    mod = _modulation(jnp.concatenate([c_p, c_s], axis=0), w["w_ada"], w["b_ada"])
    mod = mod.reshape(nbp + nbs, 6, d)
    mod_p, mod_s = mod[:nbp], mod[nbp:]

    pw = _prep_proj_weights(w["w_in"], w["w_q_up"], w["w_uk"], w["w_uv"], w["b_igate"], w["b_fgate"])
    chunk_p = ML_CHUNK if S % ML_CHUNK == 0 else S
    chunk_s = ML_CHUNK if T % ML_CHUNK == 0 else T

    (q, k, v, ckv_p, kpe_p, mq, mk, mv, g, cum, og_p) = _project(
        xp, mod_p, _rope_tables(jnp.arange(S)), chunk_p, w["g_pre_mix"], w["g_qlat"], w["g_kvlat"], pw, True)
    o_mla_p = _prompt_attention(q, k, v, nbp, S)
    h_ml_p, c_new_p, n_new_p, m_new_p = _mlstm(
        mq, mk, mv, g, cum, jnp.zeros((nbp, ML_HEADS, ML_V, ML_QK), F32), jnp.zeros((nbp, ML_HEADS, ML_QK), F32),
        jnp.zeros((nbp, ML_HEADS), F32), nbp, S)

    (qa, qr, ckv_s, kpe_s, mq, mk, mv, g, cum, og_s) = _project(
        xs, mod_s, _rope_tables(past_len + jnp.arange(T)), chunk_s, w["g_pre_mix"], w["g_qlat"], w["g_kvlat"],
        pw, False)
    o_mla_s = _sample_attention(qa, qr, ckv_s, kpe_s, cache_ckv, cache_kpe, layer, page_table,
                                w["w_uv"].transpose(1, 0, 2))
    h_ml_s, c_new_s, n_new_s, m_new_s = _mlstm(mq, mk, mv, g, cum, st_c, st_n, st_m, nbs, T)

    w_out_b = w["w_out"].astype(BF16)
    wr_t = w["w_router"].T
    wr_hi = wr_t.astype(BF16)
    wr_lo = (wr_t - wr_hi.astype(F32)).astype(BF16)
    margs = (w["g_heads"], w_out_b, w["g_post_mix"], w["g_pre_ffn"], wr_hi, wr_lo, w["b_router"])
    x1_p, h2_p, idx_p, gate_p = _merge(xp, mod_p, o_mla_p, h_ml_p, og_p, *margs, True)
    x1_s, h2_s, idx_s, gate_s = _merge(xs, mod_s, o_mla_s, h_ml_s, og_s, *margs, False)

    n_p, n_s = nbp * S, nbs * T
    h2 = jnp.concatenate([h2_p, h2_s], axis=0)
    top_idx = jnp.concatenate([idx_p[:TOP_K], idx_s[:TOP_K]], axis=1).T
    gates = jnp.concatenate([gate_p[:TOP_K], gate_s[:TOP_K]], axis=1).T
    m_rows = (n_p + n_s) * TOP_K
    n_blocks = -(-m_rows // MOE_ROWS) + N_EXPERTS
    dest, tok_sorted, shift, block_e, n_used = _route(top_idx, n_blocks)
    yb = _moe_experts(h2, block_e, tok_sorted, shift, n_used, w["w_gate_up"], w["b_gate_up"], w["w_down"],
                      w["b_down"])
    y_p = _combine(yb, dest, 0, x1_p, mod_p, gates[:n_p], w["g_post_ffn"], True)
    y_s = _combine(yb, dest, n_p * TOP_K, x1_s, mod_s, gates[n_p:], w["g_post_ffn"], False)

    new_p = (ckv_p.reshape(nbp, S, KV_LORA), kpe_p.reshape(nbp, S, MLA_ROPE), c_new_p.astype(dt),
             n_new_p.astype(dt), m_new_p.astype(dt))
    new_s = (ckv_s.reshape(nbs, T, KV_LORA), kpe_s.reshape(nbs, T, MLA_ROPE), c_new_s.astype(dt),
             n_new_s.astype(dt), m_new_s.astype(dt))
    return y_p, y_s, new_p, new_s


def kernel(x_prompt, x_sample, cache_ckv, cache_kpe, page_table, state_C, state_n, state_m, c_prompt, c_sample,
           w_ada, b_ada, g_pre_mix, g_post_mix, g_pre_ffn, g_post_ffn, w_in, g_qlat, w_q_up, g_kvlat, w_uk, w_uv,
           b_igate, b_fgate, g_heads, w_out, w_router, b_router, w_gate_up, b_gate_up, w_down, b_down):
    weights = dict(w_ada=w_ada, b_ada=b_ada, g_pre_mix=g_pre_mix, g_post_mix=g_post_mix, g_pre_ffn=g_pre_ffn,
                   g_post_ffn=g_post_ffn, w_in=w_in, g_qlat=g_qlat, w_q_up=w_q_up, g_kvlat=g_kvlat, w_uk=w_uk,
                   w_uv=w_uv, b_igate=b_igate, b_fgate=b_fgate, g_heads=g_heads, w_out=w_out, w_router=w_router,
                   b_router=b_router, w_gate_up=w_gate_up, b_gate_up=b_gate_up, w_down=w_down, b_down=b_down)
    depth = w_ada.shape[0]
    xp, xs = x_prompt, x_sample
    news_p, news_s = [], []
    for l in range(depth):
        wl = {name: val[l] for name, val in weights.items()}
        xp, xs, new_p, new_s = _layer(xp, xs, cache_ckv, cache_kpe, l, page_table, state_C[l], state_n[l],
                                      state_m[l], c_prompt, c_sample, wl)
        news_p.append(new_p)
        news_s.append(new_s)
    stack = lambda items, j: jnp.stack([it[j] for it in items])
    return ((xp, xs) + tuple(stack(news_p, j) for j in range(5)) + tuple(stack(news_s, j) for j in range(5)))
```

```python
import functools
import math

import jax
import jax.numpy as jnp
from jax import lax
from jax.experimental import pallas as pl
from jax.experimental.pallas import tpu as pltpu

F32 = jnp.float32
BF16 = jnp.bfloat16

HEAD_OUT = 128
MLA_HEADS = 4
ML_HEADS = 4
MLA_NOPE = 128
MLA_ROPE = 64
Q_LORA = 384
KV_LORA = 256
ROPE_BASE = 10000.0
SM_SCALE = (MLA_NOPE + MLA_ROPE) ** -0.5
ML_QK = 64
ML_V = 128
ML_CHUNK = 64
IGATE_CAP = 15.0
N_EXPERTS = 32
TOP_K = 4
SWIGLU_LIMIT = 7.0
SWIGLU_ALPHA = 1.702
EPS = 1e-6
PAGE_SIZE = 128

LANES = 128
LOG2E = 1.4426950408889634
QK_SCALE = SM_SCALE * LOG2E
NEG_INF = float("-inf")

TOKEN_TILE = 512
ATTN_Q_BLOCK = 1024
ATTN_K_BLOCK = 512
PAGES_PER_STEP = 64
PAGE_SLOTS = 3
ATTN_SUB = 1024
MOE_ROWS = 512
MOE_LOOKAHEAD = 2
COMBINE_TILE = 128
VMEM_LIMIT = 56 * 1024 * 1024

NT_DIMS = (((1,), (1,)), ((), ()))
TN_DIMS = (((0,), (0,)), ((), ()))


def _cparams(sem):
    return pltpu.CompilerParams(dimension_semantics=sem, vmem_limit_bytes=VMEM_LIMIT)


def _rms(x, g):
    return x * lax.rsqrt(jnp.mean(x * x, axis=-1, keepdims=True) + EPS) * g


def _dot(a, b):
    return jnp.dot(a, b, preferred_element_type=F32)


ROW_TILE = 8


def _store_row_tiles(ref, val):
    rows = val.shape[0]
    for c in range(ROW_TILE):
        ref[pl.ds(c, rows, stride=ROW_TILE), :] = val[:, c * LANES:(c + 1) * LANES]


def _load_row_tiles(ref, rows):
    return jnp.concatenate([ref[pl.ds(c, rows, stride=ROW_TILE), :] for c in range(ROW_TILE)], axis=1)


def _mod_kernel(c_ref, w_ref, b_ref, o_ref):
    c = c_ref[...]
    s = c * jax.nn.sigmoid(c)
    o_ref[...] = _dot(s.astype(BF16), w_ref[...].astype(BF16)) + b_ref[...]


def _modulation(c, w_ada, b_ada):
    nseq, d = c.shape
    n_out = w_ada.shape[1]
    return pl.pallas_call(
        _mod_kernel,
        out_shape=jax.ShapeDtypeStruct((nseq, n_out), F32),
        grid=(n_out // d,),
        in_specs=[pl.BlockSpec((nseq, d), lambda j: (0, 0)),
                  pl.BlockSpec((d, d), lambda j: (0, j)),
                  pl.BlockSpec((1, d), lambda j: (0, j))],
        out_specs=pl.BlockSpec((nseq, d), lambda j: (0, j)),
        compiler_params=_cparams(("arbitrary",)),
        name="modulation",
    )(c, w_ada, b_ada.reshape(1, n_out))


_C_QLAT = 0
_C_CKV = _C_QLAT + Q_LORA
_C_KPE = _C_CKV + KV_LORA
_C_KROT = _C_KPE + LANES
_C_MQ = _C_KROT + LANES
_C_MK = _C_MQ + ML_HEADS * ML_QK
_C_MV = _C_MK + ML_HEADS * ML_QK
_C_OG = _C_MV + ML_HEADS * ML_V
_C_GATE = _C_OG + ML_HEADS * ML_V
_C_END = _C_GATE + LANES


def _rot_half_cols(w):
    half = w.shape[-1] // 2
    return jnp.concatenate([-w[..., half:], w[..., :half]], axis=-1)


def _prep_proj_weights(w_in, w_q_up, w_uk, w_uv, b_igate, b_fgate):
    d = w_in.shape[0]
    o = 0
    parts = {}
    for name, size in (("qlat", Q_LORA), ("ckv", KV_LORA), ("kpe", MLA_ROPE), ("mq", ML_HEADS * ML_QK),
                       ("mk", ML_HEADS * ML_QK), ("mv", ML_HEADS * ML_V), ("ig", ML_HEADS), ("fg", ML_HEADS),
                       ("og", ML_HEADS * ML_V)):
        parts[name] = w_in[:, o:o + size]
        o += size
    z64 = jnp.zeros((d, LANES - MLA_ROPE), F32)
    zg = jnp.zeros((d, LANES - 2 * ML_HEADS), F32)
    w_in_r = jnp.concatenate(
        [parts["qlat"], parts["ckv"], parts["kpe"], z64, _rot_half_cols(parts["kpe"]), z64,
         parts["mq"], parts["mk"], parts["mv"], parts["og"], parts["ig"], parts["fg"], zg], axis=1).astype(BF16)
    b_gate = jnp.concatenate([b_igate, b_fgate, jnp.zeros((LANES - 2 * ML_HEADS,), F32)]).reshape(1, LANES)
    w_nope = w_q_up[:, :, :MLA_NOPE].reshape(Q_LORA, MLA_HEADS * MLA_NOPE)
    w_rope = w_q_up[:, :, MLA_NOPE:]
    zr = jnp.zeros((Q_LORA, MLA_HEADS, LANES - MLA_ROPE), F32)
    w_qr = jnp.concatenate([w_rope, zr], axis=-1).reshape(Q_LORA, MLA_HEADS * LANES)
    w_qt = jnp.concatenate([_rot_half_cols(w_rope), zr], axis=-1).reshape(Q_LORA, MLA_HEADS * LANES)
    w_uk2 = w_uk.reshape(KV_LORA, MLA_HEADS * MLA_NOPE).T.astype(BF16)
    w_uv2 = w_uv.reshape(KV_LORA, MLA_HEADS * HEAD_OUT).astype(BF16)
    w_ukT = jnp.transpose(w_uk, (1, 2, 0)).astype(BF16)
    return dict(w_in=w_in_r, b_gate=b_gate, w_qn=w_nope.astype(BF16), w_qr=w_qr.astype(BF16),
                w_qt=w_qt.astype(BF16), w_uk=w_uk2, w_uv=w_uv2, w_ukT=w_ukT)


def _rope_tables(pos):
    half = MLA_ROPE // 2
    inv_freq = ROPE_BASE ** (-jnp.arange(half, dtype=F32) / half)
    ang = pos.astype(F32)[:, None] * inv_freq[None, :]
    z = jnp.zeros((pos.shape[0], LANES - MLA_ROPE), F32)
    cos = jnp.concatenate([jnp.cos(ang), jnp.cos(ang), z], axis=1)
    sin = jnp.concatenate([jnp.sin(ang), jnp.sin(ang), z], axis=1)
    return cos, sin


def _chunk_tril(tm, chunk):
    r = jnp.arange(tm)
    same = (r[:, None] // chunk) == (r[None, :] // chunk)
    return (same & (r[None, :] <= r[:, None])).astype(BF16)


def _split3(x):
    hi = x.astype(BF16)
    r1 = x - hi.astype(F32)
    mid = r1.astype(BF16)
    lo = (r1 - mid.astype(F32)).astype(BF16)
    return hi, mid, lo


def _proj_kernel(*refs, is_prompt):
    (x_ref, mod_ref, gpre_ref, win_ref, bg_ref, gq_ref, wqn_ref, wqr_ref, wqt_ref, gkv_ref,
     cos_ref, sin_ref, tril_ref) = refs[:13]
    if is_prompt:
        wuk_ref, wuv_ref = refs[13:15]
        q_ref, k_ref, v_ref, ckv_ref, kpe_ref, mq_ref, mk_ref, mv_ref, g_ref, cum_ref, og_ref = refs[15:]
    else:
        wukT_ref = refs[13]
        qa_ref, qr_ref, ckv_ref, kpe_ref, mq_ref, mk_ref, mv_ref, g_ref, cum_ref, og_ref = refs[14:]

    x = x_ref[...]
    mod = mod_ref[...]
    nb, t, d = x.shape
    h = _rms(x, gpre_ref[...]) * (1.0 + mod[:, 1:2, :]) + mod[:, 0:1, :]
    hb = h.reshape(nb * t, d).astype(BF16)

    def zcols(a, b):
        return _dot(hb, win_ref[:, a:b])

    cos = cos_ref[...]
    sin = sin_ref[...]

    qn = _rms(zcols(_C_QLAT, _C_CKV), gq_ref[...]).astype(BF16)
    q_nope = _dot(qn, wqn_ref[...]) * QK_SCALE
    q_r = _dot(qn, wqr_ref[...])
    q_t = _dot(qn, wqt_ref[...])
    ckv = _rms(zcols(_C_CKV, _C_KPE), gkv_ref[...])
    ckv_ref[...] = ckv
    kpe = zcols(_C_KPE, _C_KROT) * cos + zcols(_C_KROT, _C_MQ) * sin
    kpe_ref[...] = kpe[:, :MLA_ROPE]
    for hd in range(MLA_HEADS):
        sl = slice(hd * LANES, (hd + 1) * LANES)
        roped = (q_r[:, sl] * cos + q_t[:, sl] * sin) * QK_SCALE
        if is_prompt:
            q_ref[:, 2 * hd * LANES:(2 * hd + 1) * LANES] = q_nope[:, sl].astype(BF16)
            q_ref[:, (2 * hd + 1) * LANES:(2 * hd + 2) * LANES] = roped.astype(BF16)
        else:
            qa_ref[:, hd * KV_LORA:(hd + 1) * KV_LORA] = _dot(q_nope[:, sl].astype(BF16), wukT_ref[hd])
            qr_ref[:, sl] = roped
    if is_prompt:
        cb = ckv.astype(BF16)
        k_nope_t = lax.dot_general(wuk_ref[...], cb, NT_DIMS, preferred_element_type=F32)
        kpe_t = kpe.T.astype(BF16)
        for hd in range(MLA_HEADS):
            k_ref[2 * hd * LANES:(2 * hd + 1) * LANES, :] = k_nope_t[hd * LANES:(hd + 1) * LANES, :].astype(BF16)
            k_ref[(2 * hd + 1) * LANES:(2 * hd + 2) * LANES, :] = kpe_t
        v_ref[...] = _dot(cb, wuv_ref[...]).astype(BF16)

    mq_ref[...] = zcols(_C_MQ, _C_MK) * (ML_QK ** -0.5)
    mk_ref[...] = zcols(_C_MK, _C_MV)
    mv_ref[...] = zcols(_C_MV, _C_OG)
    og_ref[...] = jax.nn.sigmoid(zcols(_C_OG, _C_GATE))
    zg = zcols(_C_GATE, _C_END) + bg_ref[...]
    ig = IGATE_CAP * jnp.tanh(zg / IGATE_CAP)
    lf = -(jnp.maximum(-zg, 0.0) + jnp.log1p(jnp.exp(-jnp.abs(zg))))
    lane = lax.broadcasted_iota(jnp.int32, zg.shape, 1)
    g = jnp.where(lane < ML_HEADS, ig, lf)
    g_ref[...] = g
    tril = tril_ref[...]
    hi, mid, lo = _split3(g)
    cum_ref[...] = _dot(tril, hi) + _dot(tril, mid) + _dot(tril, lo)


def _project(x, mod, pos_tab, chunk, g_pre, g_qlat, g_kvlat, pw, is_prompt):
    nseq, T, d = x.shape
    n = nseq * T
    if is_prompt:
        nb, t = 1, min(TOKEN_TILE, T)
    else:
        t = T
        nb = max(1, min(nseq, TOKEN_TILE // T))
    assert T % t == 0 and nseq % nb == 0
    tm = nb * t
    tiles_per_seq = T // t
    grid = (n // tm,)
    cos, sin = pos_tab
    if is_prompt:
        x_map = lambda i: (i // tiles_per_seq, i % tiles_per_seq, 0)
        mod_map = lambda i: (i // tiles_per_seq, 0, 0)
        tab_map = lambda i: (i % tiles_per_seq, 0)
    else:
        x_map = lambda i: (i, 0, 0)
        mod_map = lambda i: (i, 0, 0)
        tab_map = lambda i: (0, 0)
        cos = jnp.tile(cos, (nb, 1))
        sin = jnp.tile(sin, (nb, 1))
    const = lambda i: (0, 0)
    row = lambda i: (i, 0)
    tril = _chunk_tril(tm, chunk)
    in_specs = [
        pl.BlockSpec((nb, t, d), x_map),
        pl.BlockSpec((nb, 6, d), mod_map),
        pl.BlockSpec((1, d), const),
        pl.BlockSpec((d, _C_END), const),
        pl.BlockSpec((1, LANES), const),
        pl.BlockSpec((1, Q_LORA), const),
        pl.BlockSpec((Q_LORA, MLA_HEADS * MLA_NOPE), const),
        pl.BlockSpec((Q_LORA, MLA_HEADS * LANES), const),
        pl.BlockSpec((Q_LORA, MLA_HEADS * LANES), const),
        pl.BlockSpec((1, KV_LORA), const),
        pl.BlockSpec((tm, LANES), tab_map),
        pl.BlockSpec((tm, LANES), tab_map),
        pl.BlockSpec((tm, tm), const),
    ]
    args = [x, mod, g_pre.reshape(1, d), pw["w_in"], pw["b_gate"], g_qlat.reshape(1, Q_LORA), pw["w_qn"],
            pw["w_qr"], pw["w_qt"], g_kvlat.reshape(1, KV_LORA), cos, sin, tril]

    def out(cols, dtype=F32):
        return jax.ShapeDtypeStruct((n, cols), dtype), pl.BlockSpec((tm, cols), row)

    if is_prompt:
        in_specs += [pl.BlockSpec((MLA_HEADS * MLA_NOPE, KV_LORA), const),
                     pl.BlockSpec((KV_LORA, MLA_HEADS * HEAD_OUT), const)]
        args += [pw["w_uk"], pw["w_uv"]]
        k_t = (jax.ShapeDtypeStruct((MLA_HEADS * 2 * LANES, n), BF16),
               pl.BlockSpec((MLA_HEADS * 2 * LANES, tm), lambda i: (0, i)))
        outs = [out(MLA_HEADS * 2 * LANES, BF16), k_t, out(MLA_HEADS * HEAD_OUT, BF16)]
    else:
        in_specs += [pl.BlockSpec((MLA_HEADS, MLA_NOPE, KV_LORA), lambda i: (0, 0, 0))]
        args += [pw["w_ukT"]]
        outs = [out(MLA_HEADS * KV_LORA), out(MLA_HEADS * LANES)]
    outs += [out(KV_LORA), out(MLA_ROPE), out(ML_HEADS * ML_QK), out(ML_HEADS * ML_QK), out(ML_HEADS * ML_V),
             out(LANES), out(LANES), out(ML_HEADS * ML_V)]
    return pl.pallas_call(
        functools.partial(_proj_kernel, is_prompt=is_prompt),
        out_shape=[o[0] for o in outs],
        grid=grid,
        in_specs=in_specs,
        out_specs=[o[1] for o in outs],
        compiler_params=_cparams(("arbitrary",)),
        name="proj_prompt" if is_prompt else "proj_sample",
    )(*args)


def _attn_kernel(qi_ref, ki_ref, q_ref, k_ref, v_ref, o_ref, m_ref, l_ref, acc_ref):
    p = pl.program_id(1)
    qi = qi_ref[p]
    ki = ki_ref[p]
    tq = q_ref.shape[0]
    tk = k_ref.shape[1]

    @pl.when(ki == 0)
    def _():
        m_ref[...] = jnp.full(m_ref.shape, NEG_INF, F32)
        l_ref[...] = jnp.zeros(l_ref.shape, F32)
        acc_ref[...] = jnp.zeros(acc_ref.shape, F32)

    def update(masked):
        for hd in range(MLA_HEADS):
            qh = q_ref[:, hd * 2 * LANES:(hd + 1) * 2 * LANES]
            s = _dot(qh, k_ref[hd * 2 * LANES:(hd + 1) * 2 * LANES, :])
            if masked:
                row = lax.broadcasted_iota(jnp.int32, s.shape, 0) + qi * tq
                col = lax.broadcasted_iota(jnp.int32, s.shape, 1) + ki * tk
                s = jnp.where(col <= row, s, NEG_INF)
            m_prev = m_ref[hd]
            m_next = jnp.maximum(m_prev, jnp.max(s, axis=1, keepdims=True))
            alpha = jnp.exp2(m_prev - m_next)
            pr = jnp.exp2(s - jnp.concatenate([m_next] * (tk // LANES), axis=1))
            l_ref[hd] = alpha * l_ref[hd] + jnp.sum(pr, axis=1, keepdims=True)
            acc_ref[hd] = alpha * acc_ref[hd] + _dot(pr.astype(BF16), v_ref[:, hd * HEAD_OUT:(hd + 1) * HEAD_OUT])
            m_ref[hd] = m_next

    first_masked = (qi * tq) // tk
    last = (qi * tq + tq - 1) // tk

    @pl.when(ki < first_masked)
    def _():
        update(False)

    @pl.when(ki >= first_masked)
    def _():
        update(True)

    @pl.when(ki == last)
    def _():
        for hd in range(MLA_HEADS):
            o_ref[:, hd * HEAD_OUT:(hd + 1) * HEAD_OUT] = acc_ref[hd] / l_ref[hd]


def _prompt_attention(q, k, v, nseq, S):
    tq = min(ATTN_Q_BLOCK, S)
    tk = min(ATTN_K_BLOCK, S)
    nq, nk = S // tq, S // tk
    pairs = [(i, j) for i in range(nq) for j in range((i * tq + tq - 1) // tk + 1)]
    qi_tab = jnp.asarray([p[0] for p in pairs], jnp.int32)
    ki_tab = jnp.asarray([p[1] for p in pairs], jnp.int32)
    wq = MLA_HEADS * 2 * LANES
    wv = MLA_HEADS * HEAD_OUT
    return pl.pallas_call(
        _attn_kernel,
        out_shape=jax.ShapeDtypeStruct((nseq * S, wv), F32),
        grid_spec=pltpu.PrefetchScalarGridSpec(
            num_scalar_prefetch=2,
            grid=(nseq, len(pairs)),
            in_specs=[pl.BlockSpec((tq, wq), lambda b, p, qi, ki: (b * nq + qi[p], 0)),
                      pl.BlockSpec((wq, tk), lambda b, p, qi, ki: (0, b * nk + ki[p])),
                      pl.BlockSpec((tk, wv), lambda b, p, qi, ki: (b * nk + ki[p], 0))],
            out_specs=pl.BlockSpec((tq, wv), lambda b, p, qi, ki: (b * nq + qi[p], 0)),
            scratch_shapes=[pltpu.VMEM((MLA_HEADS, tq, LANES), F32),
                            pltpu.VMEM((MLA_HEADS, tq, LANES), F32),
                            pltpu.VMEM((MLA_HEADS, tq, HEAD_OUT), F32)]),
        compiler_params=_cparams(("arbitrary", "arbitrary")),
        name="prompt_attention",
    )(qi_tab, ki_tab, q, k, v)


def _sattn_kernel(pt_ref, qa_ref, qr_ref, ckvn_ref, kpen_ref, wuv_ref, cckv_hbm, ckpe_hbm, o_ref,
                  ckv_buf, kpe_buf, sem, kcb, wq_s, q_s, qr_s, kn_s, pn_s, m_ref, l_ref, acc_ref,
                  *, layer, npg, nchunks, sub):
    b = pl.program_id(0)
    c = pl.program_id(1)
    step = b * nchunks + c
    nsteps = pl.num_programs(0) * nchunks
    nslots = ckv_buf.shape[0]
    slot = step % nslots
    T = qa_ref.shape[0]
    rows = MLA_HEADS * T
    npos = npg * PAGE_SIZE
    pack = wq_s.shape[0]

    def page_copies(page_of, slt):
        cps = []
        for j in range(npg):
            pg = page_of(j)
            dst = pl.ds(j * PAGE_SIZE, PAGE_SIZE)
            cps.append(pltpu.make_async_copy(cckv_hbm.at[layer, pg], ckv_buf.at[slt, dst, :], sem.at[0, slt]))
            cps.append(pltpu.make_async_copy(ckpe_hbm.at[layer, pg], kpe_buf.at[slt, :, dst], sem.at[1, slt]))
        return cps

    def fetch(stp, slt):
        base = jnp.minimum(stp, nsteps - 1) * npg
        for cp in page_copies(lambda j: pt_ref[base + j], slt):
            cp.start()

    def wait_pages(slt):
        for cp in page_copies(lambda j: 0, slt):
            cp.wait()

    @pl.when(step == 0)
    def _():
        for k in range(nslots - 1):
            fetch(k, k)

    @pl.when(c == 0)
    def _():
        m_ref[...] = jnp.full(m_ref.shape, NEG_INF, F32)
        l_ref[...] = jnp.zeros(l_ref.shape, F32)
        acc_ref[...] = jnp.zeros(acc_ref.shape, F32)
        qa = qa_ref[...]
        qr = qr_ref[...]
        q32 = jnp.concatenate([qa[:, hd * KV_LORA:(hd + 1) * KV_LORA] for hd in range(MLA_HEADS)], axis=0)
        q_s[...] = q32.astype(BF16)
        qr_s[...] = jnp.concatenate([qr[:, hd * LANES:(hd + 1) * LANES] for hd in range(MLA_HEADS)],
                                    axis=0).astype(BF16)
        qt = jnp.concatenate([q32] * pack, axis=0).T
        lane_blk = lax.broadcasted_iota(jnp.int32, qt.shape, 1) // rows
        for j in range(pack):
            wq_s[j] = jnp.where(lane_blk == j, qt, 0.0).astype(BF16)

    wait_pages(slot)

    q = q_s[...]
    qr = qr_s[:, :MLA_ROPE]

    def softmax_update(s, pv):
        m_prev = m_ref[...]
        m_next = jnp.maximum(m_prev, jnp.max(s, axis=1, keepdims=True))
        alpha = jnp.exp2(m_prev - m_next)
        pr = jnp.exp2(s - jnp.concatenate([m_next] * (s.shape[1] // LANES), axis=1))
        l_ref[...] = alpha * l_ref[...] + jnp.sum(pr, axis=1, keepdims=True)
        acc_ref[...] = jnp.concatenate([alpha] * (KV_LORA // LANES), axis=1) * acc_ref[...] + pv(pr.astype(BF16))
        m_ref[...] = m_next

    s_parts = []
    for grp in range(npos // (sub * pack)):
        st = None
        for j in range(pack):
            sb = grp * pack + j
            sl = slice(sb * sub, (sb + 1) * sub)
            kc = ckv_buf[slot, sl, :].astype(BF16)
            kcb[sl, :] = kc
            t = _dot(kc, wq_s[j])
            st = t if st is None else st + t
        s_t = st.T
        for j in range(pack):
            sb = grp * pack + j
            kp = kpe_buf[slot, :, sb * sub:(sb + 1) * sub].astype(BF16)
            s_parts.append(s_t[j * rows:(j + 1) * rows, :] + _dot(qr, kp))

    fetch(step + nslots - 1, (step + nslots - 1) % nslots)

    def pv_chunk(p):
        parts = []
        for sb in range(npos // sub):
            psb = p[:, sb * sub:(sb + 1) * sub]
            parts.append(jnp.concatenate(
                [_dot(psb, kcb[sb * sub:(sb + 1) * sub, hh * LANES:(hh + 1) * LANES])
                 for hh in range(KV_LORA // LANES)], axis=1))
        while len(parts) > 1:
            parts = [parts[i] + parts[i + 1] for i in range(0, len(parts) - 1, 2)] + parts[len(parts) & ~1:]
        return parts[0]

    softmax_update(jnp.concatenate(s_parts, axis=1), pv_chunk)

    @pl.when(c == nchunks - 1)
    def _():
        kn_s[...] = jnp.zeros(kn_s.shape, F32)
        pn_s[...] = jnp.zeros(pn_s.shape, F32)
        kn_s[0:T, :] = ckvn_ref[...]
        pn_s[0:T, :] = kpen_ref[...]
        row = lax.broadcasted_iota(jnp.int32, (rows, LANES), 0)
        col = lax.broadcasted_iota(jnp.int32, (rows, LANES), 1)
        kn = kn_s[...].astype(BF16)
        s_new = (lax.dot_general(q, kn, NT_DIMS, preferred_element_type=F32)
                 + lax.dot_general(qr, pn_s[...].astype(BF16), NT_DIMS, preferred_element_type=F32))
        softmax_update(jnp.where(col <= row % T, s_new, NEG_INF), lambda p: _dot(p, kn))
        o_lat = acc_ref[...] / jnp.concatenate([l_ref[...]] * (KV_LORA // LANES), axis=1)
        for hd in range(MLA_HEADS):
            o_ref[:, hd * HEAD_OUT:(hd + 1) * HEAD_OUT] = _dot(o_lat[hd * T:(hd + 1) * T, :], wuv_ref[hd])

    @pl.when(step == nsteps - 1)
    def _():
        for k in range(1, nslots):
            wait_pages((step + k) % nslots)


def _sample_attention(qa, qr, ckv_new, kpe_new, cache_ckv, cache_kpe, layer, page_table, w_uv):
    nseq, n_pages = page_table.shape
    T = qa.shape[0] // nseq
    npg = min(PAGES_PER_STEP, n_pages)
    assert n_pages % npg == 0
    nchunks = n_pages // npg
    rows = MLA_HEADS * T
    assert LANES % rows == 0
    pack = LANES // rows
    sub = min(ATTN_SUB, npg * PAGE_SIZE // pack)
    assert (npg * PAGE_SIZE) % (sub * pack) == 0
    seq = lambda b, c, pt: (b, 0)
    kern = functools.partial(_sattn_kernel, layer=layer, npg=npg, nchunks=nchunks, sub=sub)
    return pl.pallas_call(
        kern,
        out_shape=jax.ShapeDtypeStruct((nseq * T, MLA_HEADS * HEAD_OUT), F32),
        grid_spec=pltpu.PrefetchScalarGridSpec(
            num_scalar_prefetch=1,
            grid=(nseq, nchunks),
            in_specs=[pl.BlockSpec((T, MLA_HEADS * KV_LORA), seq),
                      pl.BlockSpec((T, MLA_HEADS * LANES), seq),
                      pl.BlockSpec((T, KV_LORA), seq),
                      pl.BlockSpec((T, MLA_ROPE), seq),
                      pl.BlockSpec((MLA_HEADS, KV_LORA, HEAD_OUT), lambda b, c, pt: (0, 0, 0)),
                      pl.BlockSpec(memory_space=pl.ANY),
                      pl.BlockSpec(memory_space=pl.ANY)],
            out_specs=pl.BlockSpec((T, MLA_HEADS * HEAD_OUT), seq),
            scratch_shapes=[pltpu.VMEM((PAGE_SLOTS, npg * PAGE_SIZE, KV_LORA), F32),
                            pltpu.VMEM((PAGE_SLOTS, MLA_ROPE, npg * PAGE_SIZE), F32),
                            pltpu.SemaphoreType.DMA((2, PAGE_SLOTS)),
                            pltpu.VMEM((npg * PAGE_SIZE, KV_LORA), BF16),
                            pltpu.VMEM((pack, KV_LORA, LANES), BF16),
                            pltpu.VMEM((rows, KV_LORA), BF16),
                            pltpu.VMEM((rows, LANES), BF16),
                            pltpu.VMEM((LANES, KV_LORA), F32),
                            pltpu.VMEM((LANES, MLA_ROPE), F32),
                            pltpu.VMEM((rows, LANES), F32),
                            pltpu.VMEM((rows, LANES), F32),
                            pltpu.VMEM((rows, KV_LORA), F32)]),
        compiler_params=_cparams(("arbitrary", "arbitrary")),
        name="sample_attention",
    )(page_table.reshape(-1), qa, qr, ckv_new, kpe_new, w_uv, cache_ckv, jnp.swapaxes(cache_kpe, 2, 3))


ML_HK = ML_HEADS * ML_QK


def _rows_per_head(vals, rows):
    return jnp.concatenate([jnp.broadcast_to(v, (rows, v.shape[1])) for v in vals], axis=0)


def _head_block_mask(nrows, row_blk, ncols, col_blk):
    r = lax.broadcasted_iota(jnp.int32, (nrows, ncols), 0) // row_blk
    c = lax.broadcasted_iota(jnp.int32, (nrows, ncols), 1) // col_blk
    return r == c


def _mlstm_kernel(mq_ref, mk_ref, mv_ref, g_ref, cum_ref, s0_ref, n0_ref, m0_ref, h_ref, st_ref, n_ref, m_ref,
                  *, unroll_seqs, dot_dtype):
    ci = pl.program_id(1)
    nbs, L = mq_ref.shape[0], mq_ref.shape[1]
    H = ML_HEADS
    R = H * L

    @pl.when(ci == 0)
    def _():
        st_ref[...] = s0_ref[...]
        n_ref[...] = n0_ref[...]
        m_ref[...] = m0_ref[...]

    trow = lax.broadcasted_iota(jnp.int32, (R, L), 0) % L
    tcol = lax.broadcasted_iota(jnp.int32, (R, L), 1)
    causal = tcol <= trow
    q_mask = _head_block_mask(R, L, ML_HK, ML_QK)
    s_mask = _head_block_mask(R, L, R, L)
    k_mask = _head_block_mask(ML_HK, ML_QK, R, L)

    def one_seq(i):
        g = g_ref[i]
        cum = cum_ref[i]
        g_t = g.T
        cum_t = cum.T
        mq = mq_ref[i]
        mk = mk_ref[i]
        mv = mv_ref[i]
        mprev = m_ref[i]
        ig_col = jnp.concatenate([g[:, h:h + 1] for h in range(H)], axis=0)
        b_col = jnp.concatenate([cum[:, H + h:H + h + 1] for h in range(H)], axis=0)
        ig_row = _rows_per_head([g_t[h:h + 1, :] for h in range(H)], L)
        b_row = _rows_per_head([cum_t[H + h:H + h + 1, :] for h in range(H)], L)
        m_prev = _rows_per_head([mprev[h:h + 1, 0:1] for h in range(H)], L)
        b_last = _rows_per_head([cum[L - 1:L, H + h:H + h + 1] for h in range(H)], L)

        log_d = jnp.where(causal, b_col - b_row + ig_row, NEG_INF)
        log_inter = b_col + m_prev
        m_t = jnp.maximum(log_inter, jnp.max(log_d, axis=1, keepdims=True))
        d = jnp.exp(log_d - m_t)
        inter = jnp.exp(log_inter - m_t)

        q_blk = jnp.where(q_mask, jnp.concatenate([mq] * H, axis=0), 0.0)
        k_t = mk.T
        s = _dot(q_blk.astype(dot_dtype), k_t.astype(dot_dtype)) * d
        s_bd = jnp.where(s_mask, jnp.concatenate([s] * H, axis=1), 0.0)
        v_stack = jnp.concatenate([mv[:, h * ML_V:(h + 1) * ML_V] for h in range(H)], axis=0)
        st = st_ref[i]
        n_row = n_ref[i]
        num = (_dot((inter * q_blk).astype(dot_dtype), st.astype(dot_dtype))
               + _dot(s_bd.astype(dot_dtype), v_stack.astype(dot_dtype)))
        den = inter * jnp.sum(q_blk * n_row, axis=1, keepdims=True) + jnp.sum(s, axis=1, keepdims=True)
        h_all = num / jnp.maximum(jnp.abs(den), jnp.exp(-m_t))
        for h in range(H):
            h_ref[i, :, h * ML_V:(h + 1) * ML_V] = h_all[h * L:(h + 1) * L, :]

        m_last = [m_t[(h + 1) * L - 1:(h + 1) * L, :] for h in range(H)]
        m_new = _rows_per_head(m_last, L)
        w_col = jnp.exp(b_last - b_col + ig_col - m_new)
        decay = jnp.exp(b_last + m_prev - m_new)
        kt_blk = jnp.where(k_mask, jnp.concatenate([k_t] * H, axis=1), 0.0)
        decay_k = _rows_per_head([decay[h * L:h * L + 1, :] for h in range(H)], ML_QK)
        st_ref[i] = decay_k * st + _dot(kt_blk.astype(dot_dtype), (w_col * v_stack).astype(dot_dtype))

        def widen(col):
            return jnp.concatenate([jnp.broadcast_to(col[h * L:(h + 1) * L, :], (L, ML_QK)) for h in range(H)],
                                   axis=1)
        n_ref[i] = widen(decay)[0:1, :] * n_row + jnp.sum(widen(w_col) * mk, axis=0, keepdims=True)
        m_ref[i] = jnp.concatenate([jnp.broadcast_to(ml, (1, LANES)) for ml in m_last], axis=0)

    if unroll_seqs:
        for i in range(nbs):
            one_seq(i)
    else:
        def body(i, carry):
            one_seq(i)
            return carry
        lax.fori_loop(0, nbs, body, 0)


def _mlstm(mq, mk, mv, g, cum, c0, n0, m0, nseq, T):
    L = ML_CHUNK if T % ML_CHUNK == 0 else T
    nc = T // L
    nbs = nseq if nseq <= 2 else 16
    assert nseq % nbs == 0
    r3 = lambda a: a.reshape(nseq, T, a.shape[-1])
    tok = lambda gi, ci: (gi, ci, 0)
    st3 = lambda gi, ci: (gi, 0, 0)
    ct0 = jnp.swapaxes(c0.astype(F32), 2, 3).reshape(nseq, ML_HK, ML_V)
    n0r = n0.astype(F32).reshape(nseq, 1, ML_HK)
    m0b = jnp.broadcast_to(m0.astype(F32)[:, :, None], (nseq, ML_HEADS, LANES))
    kern = functools.partial(_mlstm_kernel, unroll_seqs=nbs <= 2, dot_dtype=BF16 if L >= 16 else F32)
    h, st, n, m = pl.pallas_call(
        kern,
        out_shape=[jax.ShapeDtypeStruct((nseq, T, ML_HEADS * ML_V), F32),
                   jax.ShapeDtypeStruct((nseq, ML_HK, ML_V), F32),
                   jax.ShapeDtypeStruct((nseq, 1, ML_HK), F32),
                   jax.ShapeDtypeStruct((nseq, ML_HEADS, LANES), F32)],
        grid=(nseq // nbs, nc),
        in_specs=[pl.BlockSpec((nbs, L, ML_HK), tok),
                  pl.BlockSpec((nbs, L, ML_HK), tok),
                  pl.BlockSpec((nbs, L, ML_HEADS * ML_V), tok),
                  pl.BlockSpec((nbs, L, LANES), tok),
                  pl.BlockSpec((nbs, L, LANES), tok),
                  pl.BlockSpec((nbs, ML_HK, ML_V), st3),
                  pl.BlockSpec((nbs, 1, ML_HK), st3),
                  pl.BlockSpec((nbs, ML_HEADS, LANES), st3)],
        out_specs=[pl.BlockSpec((nbs, L, ML_HEADS * ML_V), tok),
                   pl.BlockSpec((nbs, ML_HK, ML_V), st3),
                   pl.BlockSpec((nbs, 1, ML_HK), st3),
                   pl.BlockSpec((nbs, ML_HEADS, LANES), st3)],
        compiler_params=_cparams(("arbitrary", "arbitrary")),
        name="mlstm_prompt" if nseq <= 2 else "mlstm_sample",
    )(r3(mq), r3(mk), r3(mv), r3(g), r3(cum), ct0, n0r, m0b)
    c_new = jnp.swapaxes(st.reshape(nseq, ML_HEADS, ML_QK, ML_V), 2, 3)
    return h.reshape(nseq * T, ML_HEADS * ML_V), c_new, n.reshape(nseq, ML_HEADS, ML_QK), m[:, :, 0]


def _merge_kernel(x_ref, mod_ref, omla_ref, hml_ref, og_ref, gh_ref, wout_ref, gpm_ref, gpf_ref,
                  wrh_ref, wrl_ref, br_ref, x1_ref, h2_ref, idx_ref, gate_ref):
    nb, t, d = x_ref.shape
    tm = nb * t
    gh = gh_ref[...]
    parts = []
    for hd in range(MLA_HEADS + ML_HEADS):
        if hd < MLA_HEADS:
            xh = omla_ref[:, hd * HEAD_OUT:(hd + 1) * HEAD_OUT]
        else:
            j = hd - MLA_HEADS
            xh = hml_ref[:, j * HEAD_OUT:(j + 1) * HEAD_OUT]
        yh = _rms(xh, gh[:, hd * HEAD_OUT:(hd + 1) * HEAD_OUT])
        if hd >= MLA_HEADS:
            yh = yh * og_ref[:, j * HEAD_OUT:(j + 1) * HEAD_OUT]
        parts.append(yh.astype(BF16))
    merged = jnp.concatenate(parts, axis=1)
    mix = _dot(merged, wout_ref[...])
    mod = mod_ref[...]
    x1 = x_ref[...] + mod[:, 2:3, :] * _rms(mix, gpm_ref[...]).reshape(nb, t, d)
    x1_ref[...] = x1
    h2 = (_rms(x1, gpf_ref[...]) * (1.0 + mod[:, 4:5, :]) + mod[:, 3:4, :]).reshape(tm, d)
    _store_row_tiles(h2_ref, h2)
    hh = h2.astype(BF16)
    hl = (h2 - hh.astype(F32)).astype(BF16)
    wrh = wrh_ref[...]
    logits = (lax.dot_general(wrh, hh, NT_DIMS, preferred_element_type=F32)
              + lax.dot_general(wrh, hl, NT_DIMS, preferred_element_type=F32)
              + lax.dot_general(wrl_ref[...], hh, NT_DIMS, preferred_element_type=F32)) + br_ref[...]
    eidx = lax.broadcasted_iota(jnp.int32, logits.shape, 0)
    work = logits
    vals, idxs = [], []
    for _ in range(TOP_K):
        mx = jnp.max(work, axis=0, keepdims=True)
        sel = jnp.min(jnp.where(work == mx, eidx, N_EXPERTS), axis=0, keepdims=True)
        vals.append(mx)
        idxs.append(sel)
        work = jnp.where(eidx == sel, NEG_INF, work)
    ex = [jnp.exp(v - vals[0]) for v in vals]
    tot = ex[0] + ex[1] + ex[2] + ex[3]
    zi = jnp.zeros((8 - TOP_K, tm), jnp.int32)
    zf = jnp.zeros((8 - TOP_K, tm), F32)
    idx_ref[...] = jnp.concatenate(idxs + [zi], axis=0)
    gate_ref[...] = jnp.concatenate([e / tot for e in ex] + [zf], axis=0)


def _merge(x, mod, o_mla, h_ml, og, g_heads, w_out, g_post_mix, g_pre_ffn, wr_hi, wr_lo, b_router, is_prompt):
    nseq, T, d = x.shape
    n = nseq * T
    if is_prompt:
        nb, t = 1, min(TOKEN_TILE, T)
    else:
        t = T
        nb = max(1, min(nseq, TOKEN_TILE // T))
    tm = nb * t
    tiles_per_seq = T // t
    if is_prompt:
        x_map = lambda i: (i // tiles_per_seq, i % tiles_per_seq, 0)
        mod_map = lambda i: (i // tiles_per_seq, 0, 0)
    else:
        x_map = lambda i: (i, 0, 0)
        mod_map = lambda i: (i, 0, 0)
    const = lambda i: (0, 0)
    row = lambda i: (i, 0)
    col = lambda i: (0, i)
    wh = MLA_HEADS * HEAD_OUT
    return pl.pallas_call(
        _merge_kernel,
        out_shape=[jax.ShapeDtypeStruct((nseq, T, d), F32), jax.ShapeDtypeStruct((n * ROW_TILE, LANES), F32),
                   jax.ShapeDtypeStruct((8, n), jnp.int32), jax.ShapeDtypeStruct((8, n), F32)],
        grid=(n // tm,),
        in_specs=[pl.BlockSpec((nb, t, d), x_map),
                  pl.BlockSpec((nb, 6, d), mod_map),
                  pl.BlockSpec((tm, wh), row),
                  pl.BlockSpec((tm, wh), row),
                  pl.BlockSpec((tm, wh), row),
                  pl.BlockSpec((1, 2 * wh), const),
                  pl.BlockSpec((2 * wh, d), const),
                  pl.BlockSpec((1, d), const),
                  pl.BlockSpec((1, d), const),
                  pl.BlockSpec((N_EXPERTS, d), const),
                  pl.BlockSpec((N_EXPERTS, d), const),
                  pl.BlockSpec((N_EXPERTS, 1), const)],
        out_specs=[pl.BlockSpec((nb, t, d), x_map), pl.BlockSpec((tm * ROW_TILE, LANES), row),
                   pl.BlockSpec((8, tm), col), pl.BlockSpec((8, tm), col)],
        compiler_params=_cparams(("arbitrary",)),
        name="merge_prompt" if is_prompt else "merge_sample",
    )(x, mod, o_mla, h_ml, og, g_heads.reshape(1, 2 * wh), w_out, g_post_mix.reshape(1, d),
      g_pre_ffn.reshape(1, d), wr_hi, wr_lo, b_router.reshape(N_EXPERTS, 1))


def _moe_kernel(be_ref, tok_ref, sh_ref, nu_ref, h2_hbm, wgu_ref, bgu_ref, wd_ref, bd_ref, y_ref,
                xbuf, sem, wgu_s, wd_s):
    blk = pl.program_id(0)
    nblk = pl.num_programs(0)
    nslots = xbuf.shape[0]
    slot = blk % nslots
    tb = y_ref.shape[0] // ROW_TILE
    de = wd_ref.shape[0]
    n_used = nu_ref[0]

    def row_copy(tok8, r, slt):
        src = h2_hbm.at[pl.ds(pl.multiple_of(tok8, ROW_TILE), ROW_TILE), :]
        return pltpu.make_async_copy(src, xbuf.at[slt, pl.ds(r * ROW_TILE, ROW_TILE), :], sem.at[slt])

    def rows_base(bk):
        return bk * tb + sh_ref[be_ref[jnp.minimum(bk, nblk - 1)]]

    def wait_rows(slt):
        pltpu.make_async_copy(h2_hbm.at[pl.ds(0, tb * ROW_TILE), :], xbuf.at[slt], sem.at[slt]).wait()

    @pl.when(blk == 0)
    def _():
        for bk in range(nslots - 1):
            base = rows_base(bk)

            def body(r, carry):
                row_copy(tok_ref[base + r], r, bk).start()
                return carry
            lax.fori_loop(0, tb, body, 0, unroll=8)

    hot = blk < n_used
    e = be_ref[blk]
    e_prev = be_ref[jnp.maximum(blk - 1, 0)]

    @pl.when(jnp.logical_and(hot, jnp.logical_or(blk == 0, e != e_prev)))
    def _():
        wgu_s[...] = wgu_ref[...].astype(BF16)
        wd_s[...] = wd_ref[...].astype(BF16)

    @pl.when(hot)
    def _():
        wait_rows(slot)
        xb = _load_row_tiles(xbuf.at[slot], tb).astype(BF16)
        ahead = blk + nslots - 1
        base = rows_base(ahead)
        for r in range(tb):
            row_copy(tok_ref[base + r], r, ahead % nslots).start()
        gu = _dot(xb, wgu_s[...]) + bgu_ref[...]
        gt = jnp.minimum(gu[:, :de], SWIGLU_LIMIT)
        up = jnp.clip(gu[:, de:], -SWIGLU_LIMIT, SWIGLU_LIMIT)
        act = gt * jax.nn.sigmoid(SWIGLU_ALPHA * gt) * (up + 1.0)
        _store_row_tiles(y_ref, _dot(act.astype(BF16), wd_s[...]) + bd_ref[...])

    @pl.when(blk == n_used)
    def _():
        for k in range(nslots - 1):
            wait_rows((blk + k) % nslots)

    @pl.when(jnp.logical_and(hot, jnp.logical_and(blk == nblk - 1, n_used == nblk)))
    def _():
        for k in range(1, nslots):
            wait_rows((blk + k) % nslots)

    @pl.when(jnp.logical_not(hot))
    def _():
        y_ref[...] = jnp.zeros(y_ref.shape, F32)


def _moe_experts(h2, block_e, tok_sorted, shift, n_used, w_gate_up, b_gate_up, w_down, b_down):
    ne, d, de2 = w_gate_up.shape
    de = w_down.shape[1]
    n_blocks = block_e.shape[0]
    tb = MOE_ROWS
    wmap = lambda i, be, tk, sh, nu: (be[i], 0, 0)
    return pl.pallas_call(
        _moe_kernel,
        out_shape=jax.ShapeDtypeStruct((n_blocks * tb * ROW_TILE, LANES), F32),
        grid_spec=pltpu.PrefetchScalarGridSpec(
            num_scalar_prefetch=4,
            grid=(n_blocks,),
            in_specs=[pl.BlockSpec(memory_space=pl.ANY),
                      pl.BlockSpec((None, d, de2), wmap),
                      pl.BlockSpec((None, 1, de2), wmap),
                      pl.BlockSpec((None, de, d), wmap),
                      pl.BlockSpec((None, 1, d), wmap)],
            out_specs=pl.BlockSpec((tb * ROW_TILE, LANES), lambda i, be, tk, sh, nu: (i, 0)),
            scratch_shapes=[pltpu.VMEM((MOE_LOOKAHEAD + 1, tb * ROW_TILE, LANES), F32),
                            pltpu.SemaphoreType.DMA((MOE_LOOKAHEAD + 1,)),
                            pltpu.VMEM((d, de2), BF16),
                            pltpu.VMEM((de, d), BF16)]),
        compiler_params=_cparams(("arbitrary",)),
        name="moe_experts",
    )(block_e, tok_sorted, shift, n_used, h2, w_gate_up, b_gate_up.reshape(ne, 1, de2), w_down,
      b_down.reshape(ne, 1, d))


def _route(top_idx, n_blocks):
    n = top_idx.shape[0]
    m = n * TOP_K
    tb = MOE_ROWS
    flat_e = top_idx.reshape(m)
    pair = jnp.arange(m, dtype=jnp.int32)
    order = lax.sort(flat_e * m + pair) % m
    tok_sorted = jnp.concatenate([order // TOP_K * ROW_TILE, jnp.zeros(((MOE_LOOKAHEAD + 1) * tb,), jnp.int32)])
    onehot = (flat_e[:, None] == jnp.arange(N_EXPERTS, dtype=jnp.int32)[None, :]).astype(jnp.int32)
    csum = jnp.cumsum(onehot, axis=0)
    counts = csum[-1]
    padded = (counts + tb - 1) // tb * tb
    pends = jnp.cumsum(padded)
    pstarts = pends - padded
    starts = jnp.cumsum(counts) - counts
    dest = (jnp.sum(onehot * (csum - 1 + pstarts[None, :]), axis=1) * ROW_TILE).astype(jnp.int32)
    blk_row = jnp.arange(n_blocks, dtype=jnp.int32) * tb
    block_e = jnp.minimum(jnp.sum((pends[None, :] <= blk_row[:, None]).astype(jnp.int32), axis=1), N_EXPERTS - 1)
    n_used = (pends[-1] // tb).astype(jnp.int32).reshape(1)
    return dest, tok_sorted, (starts - pstarts).astype(jnp.int32), block_e.astype(jnp.int32), n_used


def _combine_kernel(dest_ref, yb_hbm, x1_ref, mod_ref, gate_ref, gpost_ref, o_ref, ybuf, sem, *, base):
    i = pl.program_id(0)
    nsteps = pl.num_programs(0)
    slot = i % 2
    nb, t, d = x1_ref.shape
    tm = nb * t

    def gather(step, slt):
        def body(r, carry):
            for kk in range(TOP_K):
                row8 = pl.multiple_of(dest_ref[base + (step * tm + r) * TOP_K + kk], ROW_TILE)
                dst = ybuf.at[slt, kk, pl.ds(pl.multiple_of(r * ROW_TILE, ROW_TILE), ROW_TILE), :]
                pltpu.make_async_copy(yb_hbm.at[pl.ds(row8, ROW_TILE), :], dst, sem.at[slt]).start()
            return carry
        lax.fori_loop(0, tm, body, 0, unroll=4)

    @pl.when(i == 0)
    def _():
        gather(0, 0)

    @pl.when(i + 1 < nsteps)
    def _():
        gather(i + 1, 1 - slot)

    for kk in range(TOP_K):
        pltpu.make_async_copy(yb_hbm.at[pl.ds(0, tm * ROW_TILE), :], ybuf.at[slot, kk], sem.at[slot]).wait()

    gate = gate_ref[...]
    y = gate[:, 0:1] * _load_row_tiles(ybuf.at[slot, 0], tm)
    for kk in range(1, TOP_K):
        y = y + gate[:, kk:kk + 1] * _load_row_tiles(ybuf.at[slot, kk], tm)
    mod = mod_ref[...]
    o_ref[...] = x1_ref[...] + mod[:, 5:6, :] * _rms(y, gpost_ref[...]).reshape(nb, t, d)


def _combine(yb, dest, base, x1, mod, gates, g_post_ffn, is_prompt):
    nseq, T, d = x1.shape
    n = nseq * T
    if is_prompt:
        nb, t = 1, min(COMBINE_TILE, T)
    else:
        t = T
        nb = max(1, min(nseq, COMBINE_TILE // T))
    tm = nb * t
    tiles_per_seq = T // t
    if is_prompt:
        x_map = lambda i, dr: (i // tiles_per_seq, i % tiles_per_seq, 0)
        mod_map = lambda i, dr: (i // tiles_per_seq, 0, 0)
    else:
        x_map = lambda i, dr: (i, 0, 0)
        mod_map = lambda i, dr: (i, 0, 0)
    return pl.pallas_call(
        functools.partial(_combine_kernel, base=base),
        out_shape=jax.ShapeDtypeStruct((nseq, T, d), F32),
        grid_spec=pltpu.PrefetchScalarGridSpec(
            num_scalar_prefetch=1,
            grid=(n // tm,),
            in_specs=[pl.BlockSpec(memory_space=pl.ANY),
                      pl.BlockSpec((nb, t, d), x_map),
                      pl.BlockSpec((nb, 6, d), mod_map),
                      pl.BlockSpec((tm, TOP_K), lambda i, dr: (i, 0)),
                      pl.BlockSpec((1, d), lambda i, dr: (0, 0))],
            out_specs=pl.BlockSpec((nb, t, d), x_map),
            scratch_shapes=[pltpu.VMEM((2, TOP_K, tm * ROW_TILE, LANES), F32),
                            pltpu.SemaphoreType.DMA((2,))]),
        compiler_params=_cparams(("arbitrary",)),
        name="combine_prompt" if is_prompt else "combine_sample",
    )(dest, yb, x1, mod, gates, g_post_ffn.reshape(1, d))


def _layer(xp, xs, cache_ckv, cache_kpe, layer, page_table, st_c, st_n, st_m, c_p, c_s, w):
    nbp, S, d = xp.shape
    nbs, T, _ = xs.shape
    past_len = page_table.shape[1] * PAGE_SIZE
    dt = xp.dtype

    mod = _modulation(jnp.concatenate([c_p, c_s], axis=0), w["w_ada"], w["b_ada"])
    mod = mod.reshape(nbp + nbs, 6, d)
    mod_p, mod_s = mod[:nbp], mod[nbp:]

    pw = _prep_proj_weights(w["w_in"], w["w_q_up"], w["w_uk"], w["w_uv"], w["b_igate"], w["b_fgate"])
    chunk_p = ML_CHUNK if S % ML_CHUNK == 0 else S
    chunk_s = ML_CHUNK if T % ML_CHUNK == 0 else T

    (q, k, v, ckv_p, kpe_p, mq, mk, mv, g, cum, og_p) = _project(
        xp, mod_p, _rope_tables(jnp.arange(S)), chunk_p, w["g_pre_mix"], w["g_qlat"], w["g_kvlat"], pw, True)
    o_mla_p = _prompt_attention(q, k, v, nbp, S)
    h_ml_p, c_new_p, n_new_p, m_new_p = _mlstm(
        mq, mk, mv, g, cum, jnp.zeros((nbp, ML_HEADS, ML_V, ML_QK), F32), jnp.zeros((nbp, ML_HEADS, ML_QK), F32),
        jnp.zeros((nbp, ML_HEADS), F32), nbp, S)

    (qa, qr, ckv_s, kpe_s, mq, mk, mv, g, cum, og_s) = _project(
        xs, mod_s, _rope_tables(past_len + jnp.arange(T)), chunk_s, w["g_pre_mix"], w["g_qlat"], w["g_kvlat"],
        pw, False)
    o_mla_s = _sample_attention(qa, qr, ckv_s, kpe_s, cache_ckv, cache_kpe, layer, page_table,
                                w["w_uv"].transpose(1, 0, 2))
    h_ml_s, c_new_s, n_new_s, m_new_s = _mlstm(mq, mk, mv, g, cum, st_c, st_n, st_m, nbs, T)

    w_out_b = w["w_out"].astype(BF16)
    wr_t = w["w_router"].T
    wr_hi = wr_t.astype(BF16)
    wr_lo = (wr_t - wr_hi.astype(F32)).astype(BF16)
    margs = (w["g_heads"], w_out_b, w["g_post_mix"], w["g_pre_ffn"], wr_hi, wr_lo, w["b_router"])
    x1_p, h2_p, idx_p, gate_p = _merge(xp, mod_p, o_mla_p, h_ml_p, og_p, *margs, True)
    x1_s, h2_s, idx_s, gate_s = _merge(xs, mod_s, o_mla_s, h_ml_s, og_s, *margs, False)

    n_p, n_s = nbp * S, nbs * T
    h2 = jnp.concatenate([h2_p, h2_s], axis=0)
    top_idx = jnp.concatenate([idx_p[:TOP_K], idx_s[:TOP_K]], axis=1).T
    gates = jnp.concatenate([gate_p[:TOP_K], gate_s[:TOP_K]], axis=1).T
    m_rows = (n_p + n_s) * TOP_K
    n_blocks = -(-m_rows // MOE_ROWS) + N_EXPERTS
    dest, tok_sorted, shift, block_e, n_used = _route(top_idx, n_blocks)
    yb = _moe_experts(h2, block_e, tok_sorted, shift, n_used, w["w_gate_up"], w["b_gate_up"], w["w_down"],
                      w["b_down"])
    y_p = _combine(yb, dest, 0, x1_p, mod_p, gates[:n_p], w["g_post_ffn"], True)
    y_s = _combine(yb, dest, n_p * TOP_K, x1_s, mod_s, gates[n_p:], w["g_post_ffn"], False)

    new_p = (ckv_p.reshape(nbp, S, KV_LORA), kpe_p.reshape(nbp, S, MLA_ROPE), c_new_p.astype(dt),
             n_new_p.astype(dt), m_new_p.astype(dt))
    new_s = (ckv_s.reshape(nbs, T, KV_LORA), kpe_s.reshape(nbs, T, MLA_ROPE), c_new_s.astype(dt),
             n_new_s.astype(dt), m_new_s.astype(dt))
    return y_p, y_s, new_p, new_s


def kernel(x_prompt, x_sample, cache_ckv, cache_kpe, page_table, state_C, state_n, state_m, c_prompt, c_sample,
           w_ada, b_ada, g_pre_mix, g_post_mix, g_pre_ffn, g_post_ffn, w_in, g_qlat, w_q_up, g_kvlat, w_uk, w_uv,
           b_igate, b_fgate, g_heads, w_out, w_router, b_router, w_gate_up, b_gate_up, w_down, b_down):
    weights = dict(w_ada=w_ada, b_ada=b_ada, g_pre_mix=g_pre_mix, g_post_mix=g_post_mix, g_pre_ffn=g_pre_ffn,
                   g_post_ffn=g_post_ffn, w_in=w_in, g_qlat=g_qlat, w_q_up=w_q_up, g_kvlat=g_kvlat, w_uk=w_uk,
                   w_uv=w_uv, b_igate=b_igate, b_fgate=b_fgate, g_heads=g_heads, w_out=w_out, w_router=w_router,
                   b_router=b_router, w_gate_up=w_gate_up, b_gate_up=b_gate_up, w_down=w_down, b_down=b_down)
    depth = w_ada.shape[0]
    xp, xs = x_prompt, x_sample
    news_p, news_s = [], []
    for l in range(depth):
        wl = {name: val[l] for name, val in weights.items()}
        xp, xs, new_p, new_s = _layer(xp, xs, cache_ckv, cache_kpe, l, page_table, state_C[l], state_n[l],
                                      state_m[l], c_prompt, c_sample, wl)
        news_p.append(new_p)
        news_s.append(new_s)
    stack = lambda items, j: jnp.stack([it[j] for it in items])
    return ((xp, xs) + tuple(stack(news_p, j) for j in range(5)) + tuple(stack(news_s, j) for j in range(5)))
```

```python
import functools

import jax
import jax.numpy as jnp
from jax import lax
from jax.experimental import pallas as pl
from jax.experimental.pallas import tpu as pltpu

F32 = jnp.float32
BF16 = jnp.bfloat16

HEAD_OUT = 128
MLA_HEADS = 4
ML_HEADS = 4
MLA_NOPE = 128
MLA_ROPE = 64
Q_LORA = 384
KV_LORA = 256
ROPE_BASE = 10000.0
SM_SCALE = (MLA_NOPE + MLA_ROPE) ** -0.5
ML_QK = 64
ML_V = 128
ML_CHUNK = 64
IGATE_CAP = 15.0
N_EXPERTS = 32
TOP_K = 4
SWIGLU_LIMIT = 7.0
SWIGLU_ALPHA = 1.702
EPS = 1e-6
PAGE_SIZE = 128

LANES = 128
LOG2E = 1.4426950408889634
QK_SCALE = SM_SCALE * LOG2E
NEG_INF = float("-inf")

TOKEN_TILE = 512
ATTN_Q_BLOCK = 1024
ATTN_K_BLOCK = 512
PAGES_PER_STEP = 64
PAGE_SLOTS = 3
ATTN_SUB = 1024
MOE_ROWS = 512
MOE_LOOKAHEAD = 3
COMBINE_TILE = 128
VMEM_LIMIT = 56 * 1024 * 1024

NT_DIMS = (((1,), (1,)), ((), ()))


def _cparams(sem):
    return pltpu.CompilerParams(dimension_semantics=sem, vmem_limit_bytes=VMEM_LIMIT)


def _rms(x, g):
    return x * lax.rsqrt(jnp.mean(x * x, axis=-1, keepdims=True) + EPS) * g


def _dot(a, b):
    return jnp.dot(a, b, preferred_element_type=F32)


ROW_TILE = 8


def _store_row_tiles(ref, val):
    rows = val.shape[0]
    for c in range(ROW_TILE):
        ref[pl.ds(c, rows, stride=ROW_TILE), :] = val[:, c * LANES:(c + 1) * LANES]


def _load_row_tiles(ref, rows):
    return jnp.concatenate([ref[pl.ds(c, rows, stride=ROW_TILE), :] for c in range(ROW_TILE)], axis=1)


def _mod_kernel(c_ref, w_ref, b_ref, o_ref):
    c = c_ref[...]
    s = c * jax.nn.sigmoid(c)
    o_ref[...] = _dot(s.astype(BF16), w_ref[...].astype(BF16)) + b_ref[...]


def _modulation(c, w_ada, b_ada):
    nseq, d = c.shape
    n_out = w_ada.shape[1]
    return pl.pallas_call(
        _mod_kernel,
        out_shape=jax.ShapeDtypeStruct((nseq, n_out), F32),
        grid=(n_out // d,),
        in_specs=[pl.BlockSpec((nseq, d), lambda j: (0, 0)),
                  pl.BlockSpec((d, d), lambda j: (0, j)),
                  pl.BlockSpec((1, d), lambda j: (0, j))],
        out_specs=pl.BlockSpec((nseq, d), lambda j: (0, j)),
        compiler_params=_cparams(("arbitrary",)),
        name="modulation",
    )(c, w_ada, b_ada.reshape(1, n_out))


_C_QLAT = 0
_C_CKV = _C_QLAT + Q_LORA
_C_KPE = _C_CKV + KV_LORA
_C_KROT = _C_KPE + LANES
_C_MQ = _C_KROT + LANES
_C_MK = _C_MQ + ML_HEADS * ML_QK
_C_MV = _C_MK + ML_HEADS * ML_QK
_C_OG = _C_MV + ML_HEADS * ML_V
_C_GATE = _C_OG + ML_HEADS * ML_V
_C_END = _C_GATE + LANES


def _rot_half_cols(w):
    half = w.shape[-1] // 2
    return jnp.concatenate([-w[..., half:], w[..., :half]], axis=-1)


def _prep_proj_weights(w_in, w_q_up, w_uk, w_uv, b_igate, b_fgate):
    d = w_in.shape[0]
    o = 0
    parts = {}
    for name, size in (("qlat", Q_LORA), ("ckv", KV_LORA), ("kpe", MLA_ROPE), ("mq", ML_HEADS * ML_QK),
                       ("mk", ML_HEADS * ML_QK), ("mv", ML_HEADS * ML_V), ("ig", ML_HEADS), ("fg", ML_HEADS),
                       ("og", ML_HEADS * ML_V)):
        parts[name] = w_in[:, o:o + size]
        o += size
    z64 = jnp.zeros((d, LANES - MLA_ROPE), F32)
    zg = jnp.zeros((d, LANES - 2 * ML_HEADS), F32)
    w_in_r = jnp.concatenate(
        [parts["qlat"], parts["ckv"], parts["kpe"], z64, _rot_half_cols(parts["kpe"]), z64,
         parts["mq"], parts["mk"], parts["mv"], parts["og"], parts["ig"], parts["fg"], zg], axis=1).astype(BF16)
    b_gate = jnp.concatenate([b_igate, b_fgate, jnp.zeros((LANES - 2 * ML_HEADS,), F32)]).reshape(1, LANES)
    w_nope = w_q_up[:, :, :MLA_NOPE].reshape(Q_LORA, MLA_HEADS * MLA_NOPE)
    w_rope = w_q_up[:, :, MLA_NOPE:]
    zr = jnp.zeros((Q_LORA, MLA_HEADS, LANES - MLA_ROPE), F32)
    w_qr = jnp.concatenate([w_rope, zr], axis=-1).reshape(Q_LORA, MLA_HEADS * LANES)
    w_qt = jnp.concatenate([_rot_half_cols(w_rope), zr], axis=-1).reshape(Q_LORA, MLA_HEADS * LANES)
    w_uk2 = w_uk.reshape(KV_LORA, MLA_HEADS * MLA_NOPE).T.astype(BF16)
    w_uv2 = w_uv.reshape(KV_LORA, MLA_HEADS * HEAD_OUT).astype(BF16)
    w_ukT = jnp.transpose(w_uk, (1, 2, 0)).astype(BF16)
    return dict(w_in=w_in_r, b_gate=b_gate, w_qn=w_nope.astype(BF16), w_qr=w_qr.astype(BF16),
                w_qt=w_qt.astype(BF16), w_uk=w_uk2, w_uv=w_uv2, w_ukT=w_ukT)


def _rope_tables(pos):
    half = MLA_ROPE // 2
    inv_freq = ROPE_BASE ** (-jnp.arange(half, dtype=F32) / half)
    ang = pos.astype(F32)[:, None] * inv_freq[None, :]
    z = jnp.zeros((pos.shape[0], LANES - MLA_ROPE), F32)
    cos = jnp.concatenate([jnp.cos(ang), jnp.cos(ang), z], axis=1)
    sin = jnp.concatenate([jnp.sin(ang), jnp.sin(ang), z], axis=1)
    return cos, sin


def _chunk_tril(tm, chunk):
    r = jnp.arange(tm)
    same = (r[:, None] // chunk) == (r[None, :] // chunk)
    return (same & (r[None, :] <= r[:, None])).astype(BF16)


def _split3(x):
    hi = x.astype(BF16)
    r1 = x - hi.astype(F32)
    mid = r1.astype(BF16)
    lo = (r1 - mid.astype(F32)).astype(BF16)
    return hi, mid, lo


def _proj_kernel(*refs, is_prompt):
    (x_ref, mod_ref, gpre_ref, win_ref, bg_ref, gq_ref, wqn_ref, wqr_ref, wqt_ref, gkv_ref,
     cos_ref, sin_ref, tril_ref) = refs[:13]
    if is_prompt:
        wuk_ref, wuv_ref = refs[13:15]
        q_ref, k_ref, v_ref, ckv_ref, kpe_ref, mq_ref, mk_ref, mv_ref, g_ref, cum_ref, og_ref = refs[15:]
    else:
        wukT_ref = refs[13]
        qa_ref, qr_ref, ckv_ref, kpe_ref, mq_ref, mk_ref, mv_ref, g_ref, cum_ref, og_ref = refs[14:]

    x = x_ref[...]
    mod = mod_ref[...]
    nb, t, d = x.shape
    h = _rms(x, gpre_ref[...]) * (1.0 + mod[:, 1:2, :]) + mod[:, 0:1, :]
    hb = h.reshape(nb * t, d).astype(BF16)

    def zcols(a, b):
        return _dot(hb, win_ref[:, a:b])

    cos = cos_ref[...]
    sin = sin_ref[...]

    qn = _rms(zcols(_C_QLAT, _C_CKV), gq_ref[...]).astype(BF16)
    q_nope = _dot(qn, wqn_ref[...]) * QK_SCALE
    q_r = _dot(qn, wqr_ref[...])
    q_t = _dot(qn, wqt_ref[...])
    ckv = _rms(zcols(_C_CKV, _C_KPE), gkv_ref[...])
    ckv_ref[...] = ckv
    kpe = zcols(_C_KPE, _C_KROT) * cos + zcols(_C_KROT, _C_MQ) * sin
    kpe_ref[...] = kpe[:, :MLA_ROPE]
    for hd in range(MLA_HEADS):
        sl = slice(hd * LANES, (hd + 1) * LANES)
        roped = (q_r[:, sl] * cos + q_t[:, sl] * sin) * QK_SCALE
        if is_prompt:
            q_ref[:, 2 * hd * LANES:(2 * hd + 1) * LANES] = q_nope[:, sl].astype(BF16)
            q_ref[:, (2 * hd + 1) * LANES:(2 * hd + 2) * LANES] = roped.astype(BF16)
        else:
            qa_ref[:, hd * KV_LORA:(hd + 1) * KV_LORA] = _dot(q_nope[:, sl].astype(BF16), wukT_ref[hd])
            qr_ref[:, sl] = roped
    if is_prompt:
        cb = ckv.astype(BF16)
        k_nope_t = lax.dot_general(wuk_ref[...], cb, NT_DIMS, preferred_element_type=F32)
        kpe_t = kpe.T.astype(BF16)
        for hd in range(MLA_HEADS):
            k_ref[2 * hd * LANES:(2 * hd + 1) * LANES, :] = k_nope_t[hd * LANES:(hd + 1) * LANES, :].astype(BF16)
            k_ref[(2 * hd + 1) * LANES:(2 * hd + 2) * LANES, :] = kpe_t
        v_ref[...] = _dot(cb, wuv_ref[...]).astype(BF16)

    mq_ref[...] = zcols(_C_MQ, _C_MK) * (ML_QK ** -0.5)
    mk_ref[...] = zcols(_C_MK, _C_MV)
    mv_ref[...] = zcols(_C_MV, _C_OG)
    og_ref[...] = jax.nn.sigmoid(zcols(_C_OG, _C_GATE))
    zg = zcols(_C_GATE, _C_END) + bg_ref[...]
    ig = IGATE_CAP * jnp.tanh(zg / IGATE_CAP)
    lf = -(jnp.maximum(-zg, 0.0) + jnp.log1p(jnp.exp(-jnp.abs(zg))))
    lane = lax.broadcasted_iota(jnp.int32, zg.shape, 1)
    g = jnp.where(lane < ML_HEADS, ig, lf)
    g_ref[...] = g
    tril = tril_ref[...]
    hi, mid, lo = _split3(g)
    cum_ref[...] = _dot(tril, hi) + _dot(tril, mid) + _dot(tril, lo)


def _project(x, mod, pos_tab, chunk, g_pre, g_qlat, g_kvlat, pw, is_prompt):
    nseq, T, d = x.shape
    n = nseq * T
    if is_prompt:
        nb, t = 1, min(TOKEN_TILE, T)
    else:
        t = T
        nb = max(1, min(nseq, TOKEN_TILE // T))
    assert T % t == 0 and nseq % nb == 0
    tm = nb * t
    tiles_per_seq = T // t
    grid = (n // tm,)
    cos, sin = pos_tab
    if is_prompt:
        x_map = lambda i: (i // tiles_per_seq, i % tiles_per_seq, 0)
        mod_map = lambda i: (i // tiles_per_seq, 0, 0)
        tab_map = lambda i: (i % tiles_per_seq, 0)
    else:
        x_map = lambda i: (i, 0, 0)
        mod_map = lambda i: (i, 0, 0)
        tab_map = lambda i: (0, 0)
        cos = jnp.tile(cos, (nb, 1))
        sin = jnp.tile(sin, (nb, 1))
    const = lambda i: (0, 0)
    row = lambda i: (i, 0)
    tril = _chunk_tril(tm, chunk)
    in_specs = [
        pl.BlockSpec((nb, t, d), x_map),
        pl.BlockSpec((nb, 6, d), mod_map),
        pl.BlockSpec((1, d), const),
        pl.BlockSpec((d, _C_END), const),
        pl.BlockSpec((1, LANES), const),
        pl.BlockSpec((1, Q_LORA), const),
        pl.BlockSpec((Q_LORA, MLA_HEADS * MLA_NOPE), const),
        pl.BlockSpec((Q_LORA, MLA_HEADS * LANES), const),
        pl.BlockSpec((Q_LORA, MLA_HEADS * LANES), const),
        pl.BlockSpec((1, KV_LORA), const),
        pl.BlockSpec((tm, LANES), tab_map),
        pl.BlockSpec((tm, LANES), tab_map),
        pl.BlockSpec((tm, tm), const),
    ]
    args = [x, mod, g_pre.reshape(1, d), pw["w_in"], pw["b_gate"], g_qlat.reshape(1, Q_LORA), pw["w_qn"],
            pw["w_qr"], pw["w_qt"], g_kvlat.reshape(1, KV_LORA), cos, sin, tril]

    def out(cols, dtype=F32):
        return jax.ShapeDtypeStruct((n, cols), dtype), pl.BlockSpec((tm, cols), row)

    if is_prompt:
        in_specs += [pl.BlockSpec((MLA_HEADS * MLA_NOPE, KV_LORA), const),
                     pl.BlockSpec((KV_LORA, MLA_HEADS * HEAD_OUT), const)]
        args += [pw["w_uk"], pw["w_uv"]]
        k_t = (jax.ShapeDtypeStruct((MLA_HEADS * 2 * LANES, n), BF16),
               pl.BlockSpec((MLA_HEADS * 2 * LANES, tm), lambda i: (0, i)))
        outs = [out(MLA_HEADS * 2 * LANES, BF16), k_t, out(MLA_HEADS * HEAD_OUT, BF16)]
    else:
        in_specs += [pl.BlockSpec((MLA_HEADS, MLA_NOPE, KV_LORA), lambda i: (0, 0, 0))]
        args += [pw["w_ukT"]]
        outs = [out(MLA_HEADS * KV_LORA), out(MLA_HEADS * LANES)]
    outs += [out(KV_LORA), out(MLA_ROPE), out(ML_HEADS * ML_QK), out(ML_HEADS * ML_QK), out(ML_HEADS * ML_V),
             out(LANES), out(LANES), out(ML_HEADS * ML_V)]
    return pl.pallas_call(
        functools.partial(_proj_kernel, is_prompt=is_prompt),
        out_shape=[o[0] for o in outs],
        grid=grid,
        in_specs=in_specs,
        out_specs=[o[1] for o in outs],
        compiler_params=_cparams(("arbitrary",)),
        name="proj_prompt" if is_prompt else "proj_sample",
    )(*args)


def _attn_kernel(qi_ref, ki_ref, q_ref, k_ref, v_ref, o_ref, m_ref, l_ref, acc_ref):
    p = pl.program_id(1)
    qi = qi_ref[p]
    ki = ki_ref[p]
    tq = q_ref.shape[0]
    tk = k_ref.shape[1]

    @pl.when(ki == 0)
    def _():
        m_ref[...] = jnp.full(m_ref.shape, NEG_INF, F32)
        l_ref[...] = jnp.zeros(l_ref.shape, F32)
        acc_ref[...] = jnp.zeros(acc_ref.shape, F32)

    def update(masked):
        for hd in range(MLA_HEADS):
            qh = q_ref[:, hd * 2 * LANES:(hd + 1) * 2 * LANES]
            s = _dot(qh, k_ref[hd * 2 * LANES:(hd + 1) * 2 * LANES, :])
            if masked:
                row = lax.broadcasted_iota(jnp.int32, s.shape, 0) + qi * tq
                col = lax.broadcasted_iota(jnp.int32, s.shape, 1) + ki * tk
                s = jnp.where(col <= row, s, NEG_INF)
            m_prev = m_ref[hd]
            m_next = jnp.maximum(m_prev, jnp.max(s, axis=1, keepdims=True))
            alpha = jnp.exp2(m_prev - m_next)
            pr = jnp.exp2(s - jnp.concatenate([m_next] * (tk // LANES), axis=1))
            l_ref[hd] = alpha * l_ref[hd] + jnp.sum(pr, axis=1, keepdims=True)
            acc_ref[hd] = alpha * acc_ref[hd] + _dot(pr.astype(BF16), v_ref[:, hd * HEAD_OUT:(hd + 1) * HEAD_OUT])
            m_ref[hd] = m_next

    first_masked = (qi * tq) // tk
    last = (qi * tq + tq - 1) // tk

    @pl.when(ki < first_masked)
    def _():
        update(False)

    @pl.when(ki >= first_masked)
    def _():
        update(True)

    @pl.when(ki == last)
    def _():
        for hd in range(MLA_HEADS):
            o_ref[:, hd * HEAD_OUT:(hd + 1) * HEAD_OUT] = acc_ref[hd] / l_ref[hd]


def _prompt_attention(q, k, v, nseq, S):
    tq = min(ATTN_Q_BLOCK, S)
    tk = min(ATTN_K_BLOCK, S)
    nq, nk = S // tq, S // tk
    pairs = [(i, j) for i in range(nq) for j in range((i * tq + tq - 1) // tk + 1)]
    qi_tab = jnp.asarray([p[0] for p in pairs], jnp.int32)
    ki_tab = jnp.asarray([p[1] for p in pairs], jnp.int32)
    wq = MLA_HEADS * 2 * LANES
    wv = MLA_HEADS * HEAD_OUT
    return pl.pallas_call(
        _attn_kernel,
        out_shape=jax.ShapeDtypeStruct((nseq * S, wv), F32),
        grid_spec=pltpu.PrefetchScalarGridSpec(
            num_scalar_prefetch=2,
            grid=(nseq, len(pairs)),
            in_specs=[pl.BlockSpec((tq, wq), lambda b, p, qi, ki: (b * nq + qi[p], 0)),
                      pl.BlockSpec((wq, tk), lambda b, p, qi, ki: (0, b * nk + ki[p])),
                      pl.BlockSpec((tk, wv), lambda b, p, qi, ki: (b * nk + ki[p], 0))],
            out_specs=pl.BlockSpec((tq, wv), lambda b, p, qi, ki: (b * nq + qi[p], 0)),
            scratch_shapes=[pltpu.VMEM((MLA_HEADS, tq, LANES), F32),
                            pltpu.VMEM((MLA_HEADS, tq, LANES), F32),
                            pltpu.VMEM((MLA_HEADS, tq, HEAD_OUT), F32)]),
        compiler_params=_cparams(("arbitrary", "arbitrary")),
        name="prompt_attention",
    )(qi_tab, ki_tab, q, k, v)


def _sattn_kernel(pt_ref, qa_ref, qr_ref, ckvn_ref, kpen_ref, wuv_ref, cckv_hbm, ckpe_hbm, o_ref,
                  ckv_buf, kpe_buf, sem, kcb, wq_s, q_s, qr_s, kn_s, pn_s, m_ref, l_ref, acc_ref,
                  *, layer, npg, nchunks, sub):
    b = pl.program_id(0)
    c = pl.program_id(1)
    step = b * nchunks + c
    nsteps = pl.num_programs(0) * nchunks
    nslots = ckv_buf.shape[0]
    slot = step % nslots
    T = qa_ref.shape[0]
    rows = MLA_HEADS * T
    npos = npg * PAGE_SIZE
    pack = wq_s.shape[0]

    def page_copies(page_of, slt):
        cps = []
        for j in range(npg):
            pg = page_of(j)
            dst = pl.ds(j * PAGE_SIZE, PAGE_SIZE)
            cps.append(pltpu.make_async_copy(cckv_hbm.at[layer, pg], ckv_buf.at[slt, dst, :], sem.at[0, slt]))
            cps.append(pltpu.make_async_copy(ckpe_hbm.at[layer, pg], kpe_buf.at[slt, :, dst], sem.at[1, slt]))
        return cps

    def fetch(stp, slt):
        base = jnp.minimum(stp, nsteps - 1) * npg
        for cp in page_copies(lambda j: pt_ref[base + j], slt):
            cp.start()

    def wait_pages(slt):
        for cp in page_copies(lambda j: 0, slt):
            cp.wait()

    @pl.when(step == 0)
    def _():
        for k in range(nslots - 1):
            fetch(k, k)

    @pl.when(c == 0)
    def _():
        m_ref[...] = jnp.full(m_ref.shape, NEG_INF, F32)
        l_ref[...] = jnp.zeros(l_ref.shape, F32)
        acc_ref[...] = jnp.zeros(acc_ref.shape, F32)
        qa = qa_ref[...]
        qr = qr_ref[...]
        q32 = jnp.concatenate([qa[:, hd * KV_LORA:(hd + 1) * KV_LORA] for hd in range(MLA_HEADS)], axis=0)
        q_s[...] = q32.astype(BF16)
        qr_s[...] = jnp.concatenate([qr[:, hd * LANES:(hd + 1) * LANES] for hd in range(MLA_HEADS)],
                                    axis=0).astype(BF16)
        qt = jnp.concatenate([q32] * pack, axis=0).T
        lane_blk = lax.broadcasted_iota(jnp.int32, qt.shape, 1) // rows
        for j in range(pack):
            wq_s[j] = jnp.where(lane_blk == j, qt, 0.0).astype(BF16)

    wait_pages(slot)

    q = q_s[...]
    qr = qr_s[:, :MLA_ROPE]

    def softmax_update(s, pv):
        m_prev = m_ref[...]
        m_next = jnp.maximum(m_prev, jnp.max(s, axis=1, keepdims=True))
        alpha = jnp.exp2(m_prev - m_next)
        pr = jnp.exp2(s - jnp.concatenate([m_next] * (s.shape[1] // LANES), axis=1))
        l_ref[...] = alpha * l_ref[...] + jnp.sum(pr, axis=1, keepdims=True)
        acc_ref[...] = jnp.concatenate([alpha] * (KV_LORA // LANES), axis=1) * acc_ref[...] + pv(pr.astype(BF16))
        m_ref[...] = m_next

    s_parts = []
    for grp in range(npos // (sub * pack)):
        st = None
        for j in range(pack):
            sb = grp * pack + j
            sl = slice(sb * sub, (sb + 1) * sub)
            kc = ckv_buf[slot, sl, :].astype(BF16)
            kcb[sl, :] = kc
            t = _dot(kc, wq_s[j])
            st = t if st is None else st + t
        s_t = st.T
        for j in range(pack):
            sb = grp * pack + j
            kp = kpe_buf[slot, :, sb * sub:(sb + 1) * sub].astype(BF16)
            s_parts.append(s_t[j * rows:(j + 1) * rows, :] + _dot(qr, kp))

    fetch(step + nslots - 1, (step + nslots - 1) % nslots)

    def pv_chunk(p):
        parts = []
        for sb in range(npos // sub):
            psb = p[:, sb * sub:(sb + 1) * sub]
            parts.append(jnp.concatenate(
                [_dot(psb, kcb[sb * sub:(sb + 1) * sub, hh * LANES:(hh + 1) * LANES])
                 for hh in range(KV_LORA // LANES)], axis=1))
        while len(parts) > 1:
            parts = [parts[i] + parts[i + 1] for i in range(0, len(parts) - 1, 2)] + parts[len(parts) & ~1:]
        return parts[0]

    softmax_update(jnp.concatenate(s_parts, axis=1), pv_chunk)

    @pl.when(c == nchunks - 1)
    def _():
        kn_s[...] = jnp.zeros(kn_s.shape, F32)
        pn_s[...] = jnp.zeros(pn_s.shape, F32)
        kn_s[0:T, :] = ckvn_ref[...]
        pn_s[0:T, :] = kpen_ref[...]
        row = lax.broadcasted_iota(jnp.int32, (rows, LANES), 0)
        col = lax.broadcasted_iota(jnp.int32, (rows, LANES), 1)
        kn = kn_s[...].astype(BF16)
        s_new = (lax.dot_general(q, kn, NT_DIMS, preferred_element_type=F32)
                 + lax.dot_general(qr, pn_s[...].astype(BF16), NT_DIMS, preferred_element_type=F32))
        softmax_update(jnp.where(col <= row % T, s_new, NEG_INF), lambda p: _dot(p, kn))
        o_lat = acc_ref[...] / jnp.concatenate([l_ref[...]] * (KV_LORA // LANES), axis=1)
        for hd in range(MLA_HEADS):
            o_ref[:, hd * HEAD_OUT:(hd + 1) * HEAD_OUT] = _dot(o_lat[hd * T:(hd + 1) * T, :], wuv_ref[hd])

    @pl.when(step == nsteps - 1)
    def _():
        for k in range(1, nslots):
            wait_pages((step + k) % nslots)


def _sample_attention(qa, qr, ckv_new, kpe_new, cache_ckv, cache_kpe, layer, page_table, w_uv):
    nseq, n_pages = page_table.shape
    T = qa.shape[0] // nseq
    npg = min(PAGES_PER_STEP, n_pages)
    assert n_pages % npg == 0
    nchunks = n_pages // npg
    rows = MLA_HEADS * T
    assert LANES % rows == 0
    pack = LANES // rows
    sub = min(ATTN_SUB, npg * PAGE_SIZE // pack)
    assert (npg * PAGE_SIZE) % (sub * pack) == 0
    seq = lambda b, c, pt: (b, 0)
    kern = functools.partial(_sattn_kernel, layer=layer, npg=npg, nchunks=nchunks, sub=sub)
    return pl.pallas_call(
        kern,
        out_shape=jax.ShapeDtypeStruct((nseq * T, MLA_HEADS * HEAD_OUT), F32),
        grid_spec=pltpu.PrefetchScalarGridSpec(
            num_scalar_prefetch=1,
            grid=(nseq, nchunks),
            in_specs=[pl.BlockSpec((T, MLA_HEADS * KV_LORA), seq),
                      pl.BlockSpec((T, MLA_HEADS * LANES), seq),
                      pl.BlockSpec((T, KV_LORA), seq),
                      pl.BlockSpec((T, MLA_ROPE), seq),
                      pl.BlockSpec((MLA_HEADS, KV_LORA, HEAD_OUT), lambda b, c, pt: (0, 0, 0)),
                      pl.BlockSpec(memory_space=pl.ANY),
                      pl.BlockSpec(memory_space=pl.ANY)],
            out_specs=pl.BlockSpec((T, MLA_HEADS * HEAD_OUT), seq),
            scratch_shapes=[pltpu.VMEM((PAGE_SLOTS, npg * PAGE_SIZE, KV_LORA), F32),
                            pltpu.VMEM((PAGE_SLOTS, MLA_ROPE, npg * PAGE_SIZE), F32),
                            pltpu.SemaphoreType.DMA((2, PAGE_SLOTS)),
                            pltpu.VMEM((npg * PAGE_SIZE, KV_LORA), BF16),
                            pltpu.VMEM((pack, KV_LORA, LANES), BF16),
                            pltpu.VMEM((rows, KV_LORA), BF16),
                            pltpu.VMEM((rows, LANES), BF16),
                            pltpu.VMEM((LANES, KV_LORA), F32),
                            pltpu.VMEM((LANES, MLA_ROPE), F32),
                            pltpu.VMEM((rows, LANES), F32),
                            pltpu.VMEM((rows, LANES), F32),
                            pltpu.VMEM((rows, KV_LORA), F32)]),
        compiler_params=_cparams(("arbitrary", "arbitrary")),
        name="sample_attention",
    )(page_table.reshape(-1), qa, qr, ckv_new, kpe_new, w_uv, cache_ckv, jnp.swapaxes(cache_kpe, 2, 3))


ML_HK = ML_HEADS * ML_QK


def _rows_per_head(vals, rows):
    return jnp.concatenate([jnp.broadcast_to(v, (rows, v.shape[1])) for v in vals], axis=0)


def _head_block_mask(nrows, row_blk, ncols, col_blk):
    r = lax.broadcasted_iota(jnp.int32, (nrows, ncols), 0) // row_blk
    c = lax.broadcasted_iota(jnp.int32, (nrows, ncols), 1) // col_blk
    return r == c


def _mlstm_kernel(mq_ref, mk_ref, mv_ref, g_ref, cum_ref, s0_ref, n0_ref, m0_ref, h_ref, st_ref, n_ref, m_ref,
                  *, unroll_seqs, dot_dtype):
    ci = pl.program_id(1)
    nbs, L = mq_ref.shape[0], mq_ref.shape[1]
    H = ML_HEADS
    R = H * L

    @pl.when(ci == 0)
    def _():
        st_ref[...] = s0_ref[...]
        n_ref[...] = n0_ref[...]
        m_ref[...] = m0_ref[...]

    trow = lax.broadcasted_iota(jnp.int32, (R, L), 0) % L
    tcol = lax.broadcasted_iota(jnp.int32, (R, L), 1)
    causal = tcol <= trow
    q_mask = _head_block_mask(R, L, ML_HK, ML_QK)
    s_mask = _head_block_mask(R, L, R, L)
    k_mask = _head_block_mask(ML_HK, ML_QK, R, L)

    def one_seq(i):
        g = g_ref[i]
        cum = cum_ref[i]
        g_t = g.T
        cum_t = cum.T
        mq = mq_ref[i]
        mk = mk_ref[i]
        mv = mv_ref[i]
        mprev = m_ref[i]
        ig_col = jnp.concatenate([g[:, h:h + 1] for h in range(H)], axis=0)
        b_col = jnp.concatenate([cum[:, H + h:H + h + 1] for h in range(H)], axis=0)
        ig_row = _rows_per_head([g_t[h:h + 1, :] for h in range(H)], L)
        b_row = _rows_per_head([cum_t[H + h:H + h + 1, :] for h in range(H)], L)
        m_prev = _rows_per_head([mprev[h:h + 1, 0:1] for h in range(H)], L)
        b_last = _rows_per_head([cum[L - 1:L, H + h:H + h + 1] for h in range(H)], L)

        log_d = jnp.where(causal, b_col - b_row + ig_row, NEG_INF)
        log_inter = b_col + m_prev
        m_t = jnp.maximum(log_inter, jnp.max(log_d, axis=1, keepdims=True))
        d = jnp.exp(log_d - m_t)
        inter = jnp.exp(log_inter - m_t)

        q_blk = jnp.where(q_mask, jnp.concatenate([mq] * H, axis=0), 0.0)
        k_t = mk.T
        s = _dot(q_blk.astype(dot_dtype), k_t.astype(dot_dtype)) * d
        s_bd = jnp.where(s_mask, jnp.concatenate([s] * H, axis=1), 0.0)
        v_stack = jnp.concatenate([mv[:, h * ML_V:(h + 1) * ML_V] for h in range(H)], axis=0)
        st = st_ref[i]
        n_row = n_ref[i]
        num = (_dot((inter * q_blk).astype(dot_dtype), st.astype(dot_dtype))
               + _dot(s_bd.astype(dot_dtype), v_stack.astype(dot_dtype)))
        den = inter * jnp.sum(q_blk * n_row, axis=1, keepdims=True) + jnp.sum(s, axis=1, keepdims=True)
        h_all = num / jnp.maximum(jnp.abs(den), jnp.exp(-m_t))
        for h in range(H):
            h_ref[i, :, h * ML_V:(h + 1) * ML_V] = h_all[h * L:(h + 1) * L, :]

        m_last = [m_t[(h + 1) * L - 1:(h + 1) * L, :] for h in range(H)]
        m_new = _rows_per_head(m_last, L)
        w_col = jnp.exp(b_last - b_col + ig_col - m_new)
        decay = jnp.exp(b_last + m_prev - m_new)
        kt_blk = jnp.where(k_mask, jnp.concatenate([k_t] * H, axis=1), 0.0)
        decay_k = _rows_per_head([decay[h * L:h * L + 1, :] for h in range(H)], ML_QK)
        st_ref[i] = decay_k * st + _dot(kt_blk.astype(dot_dtype), (w_col * v_stack).astype(dot_dtype))

        def widen(col):
            return jnp.concatenate([jnp.broadcast_to(col[h * L:(h + 1) * L, :], (L, ML_QK)) for h in range(H)],
                                   axis=1)
        n_ref[i] = widen(decay)[0:1, :] * n_row + jnp.sum(widen(w_col) * mk, axis=0, keepdims=True)
        m_ref[i] = jnp.concatenate([jnp.broadcast_to(ml, (1, LANES)) for ml in m_last], axis=0)

    if unroll_seqs:
        for i in range(nbs):
            one_seq(i)
    else:
        def body(i, carry):
            one_seq(i)
            return carry
        lax.fori_loop(0, nbs, body, 0)


def _mlstm(mq, mk, mv, g, cum, c0, n0, m0, nseq, T):
    L = ML_CHUNK if T % ML_CHUNK == 0 else T
    nc = T // L
    nbs = nseq if nseq <= 2 else 16
    assert nseq % nbs == 0
    r3 = lambda a: a.reshape(nseq, T, a.shape[-1])
    tok = lambda gi, ci: (gi, ci, 0)
    st3 = lambda gi, ci: (gi, 0, 0)
    ct0 = jnp.swapaxes(c0.astype(F32), 2, 3).reshape(nseq, ML_HK, ML_V)
    n0r = n0.astype(F32).reshape(nseq, 1, ML_HK)
    m0b = jnp.broadcast_to(m0.astype(F32)[:, :, None], (nseq, ML_HEADS, LANES))
    kern = functools.partial(_mlstm_kernel, unroll_seqs=nbs <= 2, dot_dtype=BF16 if L >= 16 else F32)
    h, st, n, m = pl.pallas_call(
        kern,
        out_shape=[jax.ShapeDtypeStruct((nseq, T, ML_HEADS * ML_V), F32),
                   jax.ShapeDtypeStruct((nseq, ML_HK, ML_V), F32),
                   jax.ShapeDtypeStruct((nseq, 1, ML_HK), F32),
                   jax.ShapeDtypeStruct((nseq, ML_HEADS, LANES), F32)],
        grid=(nseq // nbs, nc),
        in_specs=[pl.BlockSpec((nbs, L, ML_HK), tok),
                  pl.BlockSpec((nbs, L, ML_HK), tok),
                  pl.BlockSpec((nbs, L, ML_HEADS * ML_V), tok),
                  pl.BlockSpec((nbs, L, LANES), tok),
                  pl.BlockSpec((nbs, L, LANES), tok),
                  pl.BlockSpec((nbs, ML_HK, ML_V), st3),
                  pl.BlockSpec((nbs, 1, ML_HK), st3),
                  pl.BlockSpec((nbs, ML_HEADS, LANES), st3)],
        out_specs=[pl.BlockSpec((nbs, L, ML_HEADS * ML_V), tok),
                   pl.BlockSpec((nbs, ML_HK, ML_V), st3),
                   pl.BlockSpec((nbs, 1, ML_HK), st3),
                   pl.BlockSpec((nbs, ML_HEADS, LANES), st3)],
        compiler_params=_cparams(("arbitrary", "arbitrary")),
        name="mlstm_prompt" if nseq <= 2 else "mlstm_sample",
    )(r3(mq), r3(mk), r3(mv), r3(g), r3(cum), ct0, n0r, m0b)
    c_new = jnp.swapaxes(st.reshape(nseq, ML_HEADS, ML_QK, ML_V), 2, 3)
    return h.reshape(nseq * T, ML_HEADS * ML_V), c_new, n.reshape(nseq, ML_HEADS, ML_QK), m[:, :, 0]


def _merge_kernel(x_ref, mod_ref, omla_ref, hml_ref, og_ref, gh_ref, wout_ref, gpm_ref, gpf_ref,
                  wrh_ref, wrl_ref, br_ref, x1_ref, h2_ref, idx_ref, gate_ref):
    nb, t, d = x_ref.shape
    tm = nb * t
    gh = gh_ref[...]
    parts = []
    for hd in range(MLA_HEADS + ML_HEADS):
        if hd < MLA_HEADS:
            xh = omla_ref[:, hd * HEAD_OUT:(hd + 1) * HEAD_OUT]
        else:
            j = hd - MLA_HEADS
            xh = hml_ref[:, j * HEAD_OUT:(j + 1) * HEAD_OUT]
        yh = _rms(xh, gh[:, hd * HEAD_OUT:(hd + 1) * HEAD_OUT])
        if hd >= MLA_HEADS:
            yh = yh * og_ref[:, j * HEAD_OUT:(j + 1) * HEAD_OUT]
        parts.append(yh.astype(BF16))
    merged = jnp.concatenate(parts, axis=1)
    mix = _dot(merged, wout_ref[...])
    mod = mod_ref[...]
    x1 = x_ref[...] + mod[:, 2:3, :] * _rms(mix, gpm_ref[...]).reshape(nb, t, d)
    x1_ref[...] = x1
    h2 = (_rms(x1, gpf_ref[...]) * (1.0 + mod[:, 4:5, :]) + mod[:, 3:4, :]).reshape(tm, d)
    _store_row_tiles(h2_ref, h2)
    hh = h2.astype(BF16)
    hl = (h2 - hh.astype(F32)).astype(BF16)
    wrh = wrh_ref[...]
    logits = (lax.dot_general(wrh, hh, NT_DIMS, preferred_element_type=F32)
              + lax.dot_general(wrh, hl, NT_DIMS, preferred_element_type=F32)
              + lax.dot_general(wrl_ref[...], hh, NT_DIMS, preferred_element_type=F32)) + br_ref[...]
    eidx = lax.broadcasted_iota(jnp.int32, logits.shape, 0)
    work = logits
    vals, idxs = [], []
    for _ in range(TOP_K):
        mx = jnp.max(work, axis=0, keepdims=True)
        sel = jnp.min(jnp.where(work == mx, eidx, N_EXPERTS), axis=0, keepdims=True)
        vals.append(mx)
        idxs.append(sel)
        work = jnp.where(eidx == sel, NEG_INF, work)
    ex = [jnp.exp(v - vals[0]) for v in vals]
    tot = ex[0] + ex[1] + ex[2] + ex[3]
    zi = jnp.zeros((8 - TOP_K, tm), jnp.int32)
    zf = jnp.zeros((8 - TOP_K, tm), F32)
    idx_ref[...] = jnp.concatenate(idxs + [zi], axis=0)
    gate_ref[...] = jnp.concatenate([e / tot for e in ex] + [zf], axis=0)


def _merge(x, mod, o_mla, h_ml, og, g_heads, w_out, g_post_mix, g_pre_ffn, wr_hi, wr_lo, b_router, is_prompt):
    nseq, T, d = x.shape
    n = nseq * T
    if is_prompt:
        nb, t = 1, min(TOKEN_TILE, T)
    else:
        t = T
        nb = max(1, min(nseq, TOKEN_TILE // T))
    tm = nb * t
    tiles_per_seq = T // t
    if is_prompt:
        x_map = lambda i: (i // tiles_per_seq, i % tiles_per_seq, 0)
        mod_map = lambda i: (i // tiles_per_seq, 0, 0)
    else:
        x_map = lambda i: (i, 0, 0)
        mod_map = lambda i: (i, 0, 0)
    const = lambda i: (0, 0)
    row = lambda i: (i, 0)
    col = lambda i: (0, i)
    wh = MLA_HEADS * HEAD_OUT
    return pl.pallas_call(
        _merge_kernel,
        out_shape=[jax.ShapeDtypeStruct((nseq, T, d), F32), jax.ShapeDtypeStruct((n * ROW_TILE, LANES), F32),
                   jax.ShapeDtypeStruct((8, n), jnp.int32), jax.ShapeDtypeStruct((8, n), F32)],
        grid=(n // tm,),
        in_specs=[pl.BlockSpec((nb, t, d), x_map),
                  pl.BlockSpec((nb, 6, d), mod_map),
                  pl.BlockSpec((tm, wh), row),
                  pl.BlockSpec((tm, wh), row),
                  pl.BlockSpec((tm, wh), row),
                  pl.BlockSpec((1, 2 * wh), const),
                  pl.BlockSpec((2 * wh, d), const),
                  pl.BlockSpec((1, d), const),
                  pl.BlockSpec((1, d), const),
                  pl.BlockSpec((N_EXPERTS, d), const),
                  pl.BlockSpec((N_EXPERTS, d), const),
                  pl.BlockSpec((N_EXPERTS, 1), const)],
        out_specs=[pl.BlockSpec((nb, t, d), x_map), pl.BlockSpec((tm * ROW_TILE, LANES), row),
                   pl.BlockSpec((8, tm), col), pl.BlockSpec((8, tm), col)],
        compiler_params=_cparams(("arbitrary",)),
        name="merge_prompt" if is_prompt else "merge_sample",
    )(x, mod, o_mla, h_ml, og, g_heads.reshape(1, 2 * wh), w_out, g_post_mix.reshape(1, d),
      g_pre_ffn.reshape(1, d), wr_hi, wr_lo, b_router.reshape(N_EXPERTS, 1))


def _moe_kernel(be_ref, tok_ref, sh_ref, nu_ref, h2_hbm, wgu_ref, bgu_ref, wd_ref, bd_ref, y_ref,
                xbuf, sem, wgu_s, wd_s):
    blk = pl.program_id(0)
    nblk = pl.num_programs(0)
    nslots = xbuf.shape[0]
    slot = blk % nslots
    tb = y_ref.shape[0] // ROW_TILE
    de = wd_ref.shape[0]
    n_used = nu_ref[0]

    def row_copy(tok8, r, slt):
        src = h2_hbm.at[pl.ds(pl.multiple_of(tok8, ROW_TILE), ROW_TILE), :]
        return pltpu.make_async_copy(src, xbuf.at[slt, pl.ds(r * ROW_TILE, ROW_TILE), :], sem.at[slt])

    def rows_base(bk):
        return bk * tb + sh_ref[be_ref[jnp.minimum(bk, nblk - 1)]]

    def wait_rows(slt):
        pltpu.make_async_copy(h2_hbm.at[pl.ds(0, tb * ROW_TILE), :], xbuf.at[slt], sem.at[slt]).wait()

    @pl.when(blk == 0)
    def _():
        for bk in range(nslots - 1):
            base = rows_base(bk)

            def body(r, carry):
                row_copy(tok_ref[base + r], r, bk).start()
                return carry
            lax.fori_loop(0, tb, body, 0, unroll=8)

    hot = blk < n_used
    e = be_ref[blk]
    e_prev = be_ref[jnp.maximum(blk - 1, 0)]

    @pl.when(jnp.logical_and(hot, jnp.logical_or(blk == 0, e != e_prev)))
    def _():
        wgu_s[...] = wgu_ref[...].astype(BF16)
        wd_s[...] = wd_ref[...].astype(BF16)

    @pl.when(hot)
    def _():
        wait_rows(slot)
        xb = _load_row_tiles(xbuf.at[slot], tb).astype(BF16)
        ahead = blk + nslots - 1
        base = rows_base(ahead)
        for r in range(tb):
            row_copy(tok_ref[base + r], r, ahead % nslots).start()
        gu = _dot(xb, wgu_s[...]) + bgu_ref[...]
        gt = jnp.minimum(gu[:, :de], SWIGLU_LIMIT)
        up = jnp.clip(gu[:, de:], -SWIGLU_LIMIT, SWIGLU_LIMIT)
        act = gt * jax.nn.sigmoid(SWIGLU_ALPHA * gt) * (up + 1.0)
        _store_row_tiles(y_ref, _dot(act.astype(BF16), wd_s[...]) + bd_ref[...])

    @pl.when(blk == n_used)
    def _():
        for k in range(nslots - 1):
            wait_rows((blk + k) % nslots)

    @pl.when(jnp.logical_and(hot, jnp.logical_and(blk == nblk - 1, n_used == nblk)))
    def _():
        for k in range(1, nslots):
            wait_rows((blk + k) % nslots)

    @pl.when(jnp.logical_not(hot))
    def _():
        y_ref[...] = jnp.zeros(y_ref.shape, F32)


def _moe_experts(h2, block_e, tok_sorted, shift, n_used, w_gate_up, b_gate_up, w_down, b_down):
    ne, d, de2 = w_gate_up.shape
    de = w_down.shape[1]
    n_blocks = block_e.shape[0]
    tb = MOE_ROWS
    wmap = lambda i, be, tk, sh, nu: (be[i], 0, 0)
    return pl.pallas_call(
        _moe_kernel,
        out_shape=jax.ShapeDtypeStruct((n_blocks * tb * ROW_TILE, LANES), F32),
        grid_spec=pltpu.PrefetchScalarGridSpec(
            num_scalar_prefetch=4,
            grid=(n_blocks,),
            in_specs=[pl.BlockSpec(memory_space=pl.ANY),
                      pl.BlockSpec((None, d, de2), wmap),
                      pl.BlockSpec((None, 1, de2), wmap),
                      pl.BlockSpec((None, de, d), wmap),
                      pl.BlockSpec((None, 1, d), wmap)],
            out_specs=pl.BlockSpec((tb * ROW_TILE, LANES), lambda i, be, tk, sh, nu: (i, 0)),
            scratch_shapes=[pltpu.VMEM((MOE_LOOKAHEAD + 1, tb * ROW_TILE, LANES), F32),
                            pltpu.SemaphoreType.DMA((MOE_LOOKAHEAD + 1,)),
                            pltpu.VMEM((d, de2), BF16),
                            pltpu.VMEM((de, d), BF16)]),
        compiler_params=_cparams(("arbitrary",)),
        name="moe_experts",
    )(block_e, tok_sorted, shift, n_used, h2, w_gate_up, b_gate_up.reshape(ne, 1, de2), w_down,
      b_down.reshape(ne, 1, d))


def _route(top_idx, n_blocks):
    n = top_idx.shape[0]
    m = n * TOP_K
    tb = MOE_ROWS
    flat_e = top_idx.reshape(m)
    pair = jnp.arange(m, dtype=jnp.int32)
    order = lax.sort(flat_e * m + pair) % m
    tok_sorted = jnp.concatenate([order // TOP_K * ROW_TILE, jnp.zeros(((MOE_LOOKAHEAD + 1) * tb,), jnp.int32)])
    onehot = (flat_e[:, None] == jnp.arange(N_EXPERTS, dtype=jnp.int32)[None, :]).astype(jnp.int32)
    csum = jnp.cumsum(onehot, axis=0)
    counts = csum[-1]
    padded = (counts + tb - 1) // tb * tb
    pends = jnp.cumsum(padded)
    pstarts = pends - padded
    starts = jnp.cumsum(counts) - counts
    dest = (jnp.sum(onehot * (csum - 1 + pstarts[None, :]), axis=1) * ROW_TILE).astype(jnp.int32)
    blk_row = jnp.arange(n_blocks, dtype=jnp.int32) * tb
    block_e = jnp.minimum(jnp.sum((pends[None, :] <= blk_row[:, None]).astype(jnp.int32), axis=1), N_EXPERTS - 1)
    n_used = (pends[-1] // tb).astype(jnp.int32).reshape(1)
    return dest, tok_sorted, (starts - pstarts).astype(jnp.int32), block_e.astype(jnp.int32), n_used


def _combine_kernel(dest_ref, yb_hbm, x1_ref, mod_ref, gate_ref, gpost_ref, o_ref, ybuf, sem, *, base):
    i = pl.program_id(0)
    nsteps = pl.num_programs(0)
    slot = i % 2
    nb, t, d = x1_ref.shape
    tm = nb * t

    def gather(step, slt):
        def body(r, carry):
            for kk in range(TOP_K):
                row8 = pl.multiple_of(dest_ref[base + (step * tm + r) * TOP_K + kk], ROW_TILE)
                dst = ybuf.at[slt, kk, pl.ds(pl.multiple_of(r * ROW_TILE, ROW_TILE), ROW_TILE), :]
                pltpu.make_async_copy(yb_hbm.at[pl.ds(row8, ROW_TILE), :], dst, sem.at[slt]).start()
            return carry
        lax.fori_loop(0, tm, body, 0, unroll=4)

    @pl.when(i == 0)
    def _():
        gather(0, 0)

    @pl.when(i + 1 < nsteps)
    def _():
        gather(i + 1, 1 - slot)

    for kk in range(TOP_K):
        pltpu.make_async_copy(yb_hbm.at[pl.ds(0, tm * ROW_TILE), :], ybuf.at[slot, kk], sem.at[slot]).wait()

    gate = gate_ref[...]
    y = gate[:, 0:1] * _load_row_tiles(ybuf.at[slot, 0], tm)
    for kk in range(1, TOP_K):
        y = y + gate[:, kk:kk + 1] * _load_row_tiles(ybuf.at[slot, kk], tm)
    mod = mod_ref[...]
    o_ref[...] = x1_ref[...] + mod[:, 5:6, :] * _rms(y, gpost_ref[...]).reshape(nb, t, d)


def _combine(yb, dest, base, x1, mod, gates, g_post_ffn, is_prompt):
    nseq, T, d = x1.shape
    n = nseq * T
    if is_prompt:
        nb, t = 1, min(COMBINE_TILE, T)
    else:
        t = T
        nb = max(1, min(nseq, COMBINE_TILE // T))
    tm = nb * t
    tiles_per_seq = T // t
    if is_prompt:
        x_map = lambda i, dr: (i // tiles_per_seq, i % tiles_per_seq, 0)
        mod_map = lambda i, dr: (i // tiles_per_seq, 0, 0)
    else:
        x_map = lambda i, dr: (i, 0, 0)
        mod_map = lambda i, dr: (i, 0, 0)
    return pl.pallas_call(
        functools.partial(_combine_kernel, base=base),
        out_shape=jax.ShapeDtypeStruct((nseq, T, d), F32),
        grid_spec=pltpu.PrefetchScalarGridSpec(
            num_scalar_prefetch=1,
            grid=(n // tm,),
            in_specs=[pl.BlockSpec(memory_space=pl.ANY),
                      pl.BlockSpec((nb, t, d), x_map),
                      pl.BlockSpec((nb, 6, d), mod_map),
                      pl.BlockSpec((tm, TOP_K), lambda i, dr: (i, 0)),
                      pl.BlockSpec((1, d), lambda i, dr: (0, 0))],
            out_specs=pl.BlockSpec((nb, t, d), x_map),
            scratch_shapes=[pltpu.VMEM((2, TOP_K, tm * ROW_TILE, LANES), F32),
                            pltpu.SemaphoreType.DMA((2,))]),
        compiler_params=_cparams(("arbitrary",)),
        name="combine_prompt" if is_prompt else "combine_sample",
    )(dest, yb, x1, mod, gates, g_post_ffn.reshape(1, d))


def _layer(xp, xs, cache_ckv, cache_kpe, layer, page_table, st_c, st_n, st_m, c_p, c_s, w):
    nbp, S, d = xp.shape
    nbs, T, _ = xs.shape
    past_len = page_table.shape[1] * PAGE_SIZE
    dt = xp.dtype

    mod = _modulation(jnp.concatenate([c_p, c_s], axis=0), w["w_ada"], w["b_ada"])
    mod = mod.reshape(nbp + nbs, 6, d)
    mod_p, mod_s = mod[:nbp], mod[nbp:]

    pw = _prep_proj_weights(w["w_in"], w["w_q_up"], w["w_uk"], w["w_uv"], w["b_igate"], w["b_fgate"])
    chunk_p = ML_CHUNK if S % ML_CHUNK == 0 else S
    chunk_s = ML_CHUNK if T % ML_CHUNK == 0 else T

    (q, k, v, ckv_p, kpe_p, mq, mk, mv, g, cum, og_p) = _project(
        xp, mod_p, _rope_tables(jnp.arange(S)), chunk_p, w["g_pre_mix"], w["g_qlat"], w["g_kvlat"], pw, True)
    o_mla_p = _prompt_attention(q, k, v, nbp, S)
    h_ml_p, c_new_p, n_new_p, m_new_p = _mlstm(
        mq, mk, mv, g, cum, jnp.zeros((nbp, ML_HEADS, ML_V, ML_QK), F32), jnp.zeros((nbp, ML_HEADS, ML_QK), F32),
        jnp.zeros((nbp, ML_HEADS), F32), nbp, S)

    (qa, qr, ckv_s, kpe_s, mq, mk, mv, g, cum, og_s) = _project(
        xs, mod_s, _rope_tables(past_len + jnp.arange(T)), chunk_s, w["g_pre_mix"], w["g_qlat"], w["g_kvlat"],
        pw, False)
    o_mla_s = _sample_attention(qa, qr, ckv_s, kpe_s, cache_ckv, cache_kpe, layer, page_table,
                                w["w_uv"].transpose(1, 0, 2))
    h_ml_s, c_new_s, n_new_s, m_new_s = _mlstm(mq, mk, mv, g, cum, st_c, st_n, st_m, nbs, T)

    w_out_b = w["w_out"].astype(BF16)
    wr_t = w["w_router"].T
    wr_hi = wr_t.astype(BF16)
    wr_lo = (wr_t - wr_hi.astype(F32)).astype(BF16)
    margs = (w["g_heads"], w_out_b, w["g_post_mix"], w["g_pre_ffn"], wr_hi, wr_lo, w["b_router"])
    x1_p, h2_p, idx_p, gate_p = _merge(xp, mod_p, o_mla_p, h_ml_p, og_p, *margs, True)
    x1_s, h2_s, idx_s, gate_s = _merge(xs, mod_s, o_mla_s, h_ml_s, og_s, *margs, False)

    n_p, n_s = nbp * S, nbs * T
    h2 = jnp.concatenate([h2_p, h2_s], axis=0)
    top_idx = jnp.concatenate([idx_p[:TOP_K], idx_s[:TOP_K]], axis=1).T
    gates = jnp.concatenate([gate_p[:TOP_K], gate_s[:TOP_K]], axis=1).T
    m_rows = (n_p + n_s) * TOP_K
    n_blocks = -(-m_rows // MOE_ROWS) + N_EXPERTS
    dest, tok_sorted, shift, block_e, n_used = _route(top_idx, n_blocks)
    yb = _moe_experts(h2, block_e, tok_sorted, shift, n_used, w["w_gate_up"], w["b_gate_up"], w["w_down"],
                      w["b_down"])
    y_p = _combine(yb, dest, 0, x1_p, mod_p, gates[:n_p], w["g_post_ffn"], True)
    y_s = _combine(yb, dest, n_p * TOP_K, x1_s, mod_s, gates[n_p:], w["g_post_ffn"], False)

    new_p = (ckv_p.reshape(nbp, S, KV_LORA), kpe_p.reshape(nbp, S, MLA_ROPE), c_new_p.astype(dt),
             n_new_p.astype(dt), m_new_p.astype(dt))
    new_s = (ckv_s.reshape(nbs, T, KV_LORA), kpe_s.reshape(nbs, T, MLA_ROPE), c_new_s.astype(dt),
             n_new_s.astype(dt), m_new_s.astype(dt))
    return y_p, y_s, new_p, new_s


def kernel(x_prompt, x_sample, cache_ckv, cache_kpe, page_table, state_C, state_n, state_m, c_prompt, c_sample,
           w_ada, b_ada, g_pre_mix, g_post_mix, g_pre_ffn, g_post_ffn, w_in, g_qlat, w_q_up, g_kvlat, w_uk, w_uv,
           b_igate, b_fgate, g_heads, w_out, w_router, b_router, w_gate_up, b_gate_up, w_down, b_down):
    weights = dict(w_ada=w_ada, b_ada=b_ada, g_pre_mix=g_pre_mix, g_post_mix=g_post_mix, g_pre_ffn=g_pre_ffn,
                   g_post_ffn=g_post_ffn, w_in=w_in, g_qlat=g_qlat, w_q_up=w_q_up, g_kvlat=g_kvlat, w_uk=w_uk,
                   w_uv=w_uv, b_igate=b_igate, b_fgate=b_fgate, g_heads=g_heads, w_out=w_out, w_router=w_router,
                   b_router=b_router, w_gate_up=w_gate_up, b_gate_up=b_gate_up, w_down=w_down, b_down=b_down)
    depth = w_ada.shape[0]
    xp, xs = x_prompt, x_sample
    news_p, news_s = [], []
    for l in range(depth):
        wl = {name: val[l] for name, val in weights.items()}
        xp, xs, new_p, new_s = _layer(xp, xs, cache_ckv, cache_kpe, l, page_table, state_C[l], state_n[l],
                                      state_m[l], c_prompt, c_sample, wl)
        news_p.append(new_p)
        news_s.append(new_s)
    stack = lambda items, j: jnp.stack([it[j] for it in items])
    return ((xp, xs) + tuple(stack(news_p, j) for j in range(5)) + tuple(stack(news_s, j) for j in range(5)))
```

```python
import functools
import math

import jax
import jax.numpy as jnp
from jax import lax
from jax.experimental import pallas as pl
from jax.experimental.pallas import tpu as pltpu

F32 = jnp.float32
BF16 = jnp.bfloat16

HEAD_OUT = 128
MLA_HEADS = 4
ML_HEADS = 4
MLA_NOPE = 128
MLA_ROPE = 64
Q_LORA = 384
KV_LORA = 256
ROPE_BASE = 10000.0
SM_SCALE = (MLA_NOPE + MLA_ROPE) ** -0.5
ML_QK = 64
ML_V = 128
ML_CHUNK = 64
IGATE_CAP = 15.0
N_EXPERTS = 32
TOP_K = 4
SWIGLU_LIMIT = 7.0
SWIGLU_ALPHA = 1.702
EPS = 1e-6
PAGE_SIZE = 128

LANES = 128
LOG2E = 1.4426950408889634
QK_SCALE = SM_SCALE * LOG2E
NEG_INF = float("-inf")

TOKEN_TILE = 512
ATTN_Q_BLOCK = 1024
ATTN_K_BLOCK = 512
PAGES_PER_STEP = 64
PAGE_SLOTS = 3
ATTN_SUB = 1024
MOE_ROWS = 512
MOE_LOOKAHEAD = 2
COMBINE_TILE = 128
VMEM_LIMIT = 56 * 1024 * 1024

NT_DIMS = (((1,), (1,)), ((), ()))
TN_DIMS = (((0,), (0,)), ((), ()))


def _cparams(sem):
    return pltpu.CompilerParams(dimension_semantics=sem, vmem_limit_bytes=VMEM_LIMIT)


def _rms(x, g):
    return x * lax.rsqrt(jnp.mean(x * x, axis=-1, keepdims=True) + EPS) * g


def _dot(a, b):
    return jnp.dot(a, b, preferred_element_type=F32)


ROW_TILE = 8


def _store_row_tiles(ref, val):
    rows = val.shape[0]
    for c in range(ROW_TILE):
        ref[pl.ds(c, rows, stride=ROW_TILE), :] = val[:, c * LANES:(c + 1) * LANES]


def _load_row_tiles(ref, rows):
    return jnp.concatenate([ref[pl.ds(c, rows, stride=ROW_TILE), :] for c in range(ROW_TILE)], axis=1)


def _mod_kernel(c_ref, w_ref, b_ref, o_ref):
    c = c_ref[...]
    s = c * jax.nn.sigmoid(c)
    o_ref[...] = _dot(s.astype(BF16), w_ref[...].astype(BF16)) + b_ref[...]


def _modulation(c, w_ada, b_ada):
    nseq, d = c.shape
    n_out = w_ada.shape[1]
    return pl.pallas_call(
        _mod_kernel,
        out_shape=jax.ShapeDtypeStruct((nseq, n_out), F32),
        grid=(n_out // d,),
        in_specs=[pl.BlockSpec((nseq, d), lambda j: (0, 0)),
                  pl.BlockSpec((d, d), lambda j: (0, j)),
                  pl.BlockSpec((1, d), lambda j: (0, j))],
        out_specs=pl.BlockSpec((nseq, d), lambda j: (0, j)),
        compiler_params=_cparams(("arbitrary",)),
        name="modulation",
    )(c, w_ada, b_ada.reshape(1, n_out))


_C_QLAT = 0
_C_CKV = _C_QLAT + Q_LORA
_C_KPE = _C_CKV + KV_LORA
_C_KROT = _C_KPE + LANES
_C_MQ = _C_KROT + LANES
_C_MK = _C_MQ + ML_HEADS * ML_QK
_C_MV = _C_MK + ML_HEADS * ML_QK
_C_OG = _C_MV + ML_HEADS * ML_V
_C_GATE = _C_OG + ML_HEADS * ML_V
_C_END = _C_GATE + LANES


def _rot_half_cols(w):
    half = w.shape[-1] // 2
    return jnp.concatenate([-w[..., half:], w[..., :half]], axis=-1)


def _prep_proj_weights(w_in, w_q_up, w_uk, w_uv, b_igate, b_fgate):
    d = w_in.shape[0]
    o = 0
    parts = {}
    for name, size in (("qlat", Q_LORA), ("ckv", KV_LORA), ("kpe", MLA_ROPE), ("mq", ML_HEADS * ML_QK),
                       ("mk", ML_HEADS * ML_QK), ("mv", ML_HEADS * ML_V), ("ig", ML_HEADS), ("fg", ML_HEADS),
                       ("og", ML_HEADS * ML_V)):
        parts[name] = w_in[:, o:o + size]
        o += size
    z64 = jnp.zeros((d, LANES - MLA_ROPE), F32)
    zg = jnp.zeros((d, LANES - 2 * ML_HEADS), F32)
    w_in_r = jnp.concatenate(
        [parts["qlat"], parts["ckv"], parts["kpe"], z64, _rot_half_cols(parts["kpe"]), z64,
         parts["mq"], parts["mk"], parts["mv"], parts["og"], parts["ig"], parts["fg"], zg], axis=1).astype(BF16)
    b_gate = jnp.concatenate([b_igate, b_fgate, jnp.zeros((LANES - 2 * ML_HEADS,), F32)]).reshape(1, LANES)
    w_nope = w_q_up[:, :, :MLA_NOPE].reshape(Q_LORA, MLA_HEADS * MLA_NOPE)
    w_rope = w_q_up[:, :, MLA_NOPE:]
    zr = jnp.zeros((Q_LORA, MLA_HEADS, LANES - MLA_ROPE), F32)
    w_qr = jnp.concatenate([w_rope, zr], axis=-1).reshape(Q_LORA, MLA_HEADS * LANES)
    w_qt = jnp.concatenate([_rot_half_cols(w_rope), zr], axis=-1).reshape(Q_LORA, MLA_HEADS * LANES)
    w_uk2 = w_uk.reshape(KV_LORA, MLA_HEADS * MLA_NOPE).T.astype(BF16)
    w_uv2 = w_uv.reshape(KV_LORA, MLA_HEADS * HEAD_OUT).astype(BF16)
    w_ukT = jnp.transpose(w_uk, (1, 2, 0)).astype(BF16)
    return dict(w_in=w_in_r, b_gate=b_gate, w_qn=w_nope.astype(BF16), w_qr=w_qr.astype(BF16),
                w_qt=w_qt.astype(BF16), w_uk=w_uk2, w_uv=w_uv2, w_ukT=w_ukT)


def _rope_tables(pos):
    half = MLA_ROPE // 2
    inv_freq = ROPE_BASE ** (-jnp.arange(half, dtype=F32) / half)
    ang = pos.astype(F32)[:, None] * inv_freq[None, :]
    z = jnp.zeros((pos.shape[0], LANES - MLA_ROPE), F32)
    cos = jnp.concatenate([jnp.cos(ang), jnp.cos(ang), z], axis=1)
    sin = jnp.concatenate([jnp.sin(ang), jnp.sin(ang), z], axis=1)
    return cos, sin


def _chunk_tril(tm, chunk):
    r = jnp.arange(tm)
    same = (r[:, None] // chunk) == (r[None, :] // chunk)
    return (same & (r[None, :] <= r[:, None])).astype(BF16)


def _split3(x):
    hi = x.astype(BF16)
    r1 = x - hi.astype(F32)
    mid = r1.astype(BF16)
    lo = (r1 - mid.astype(F32)).astype(BF16)
    return hi, mid, lo


def _proj_kernel(*refs, is_prompt):
    (x_ref, mod_ref, gpre_ref, win_ref, bg_ref, gq_ref, wqn_ref, wqr_ref, wqt_ref, gkv_ref,
     cos_ref, sin_ref, tril_ref) = refs[:13]
    if is_prompt:
        wuk_ref, wuv_ref = refs[13:15]
        q_ref, k_ref, v_ref, ckv_ref, kpe_ref, mq_ref, mk_ref, mv_ref, g_ref, cum_ref, og_ref = refs[15:]
    else:
        wukT_ref = refs[13]
        qa_ref, qr_ref, ckv_ref, kpe_ref, mq_ref, mk_ref, mv_ref, g_ref, cum_ref, og_ref = refs[14:]

    x = x_ref[...]
    mod = mod_ref[...]
    nb, t, d = x.shape
    h = _rms(x, gpre_ref[...]) * (1.0 + mod[:, 1:2, :]) + mod[:, 0:1, :]
    hb = h.reshape(nb * t, d).astype(BF16)

    def zcols(a, b):
        return _dot(hb, win_ref[:, a:b])

    cos = cos_ref[...]
    sin = sin_ref[...]

    qn = _rms(zcols(_C_QLAT, _C_CKV), gq_ref[...]).astype(BF16)
    q_nope = _dot(qn, wqn_ref[...]) * QK_SCALE
    q_r = _dot(qn, wqr_ref[...])
    q_t = _dot(qn, wqt_ref[...])
    ckv = _rms(zcols(_C_CKV, _C_KPE), gkv_ref[...])
    ckv_ref[...] = ckv
    kpe = zcols(_C_KPE, _C_KROT) * cos + zcols(_C_KROT, _C_MQ) * sin
    kpe_ref[...] = kpe[:, :MLA_ROPE]
    for hd in range(MLA_HEADS):
        sl = slice(hd * LANES, (hd + 1) * LANES)
        roped = (q_r[:, sl] * cos + q_t[:, sl] * sin) * QK_SCALE
        if is_prompt:
            q_ref[:, 2 * hd * LANES:(2 * hd + 1) * LANES] = q_nope[:, sl].astype(BF16)
            q_ref[:, (2 * hd + 1) * LANES:(2 * hd + 2) * LANES] = roped.astype(BF16)
        else:
            qa_ref[:, hd * KV_LORA:(hd + 1) * KV_LORA] = _dot(q_nope[:, sl].astype(BF16), wukT_ref[hd])
            qr_ref[:, sl] = roped
    if is_prompt:
        cb = ckv.astype(BF16)
        k_nope_t = lax.dot_general(wuk_ref[...], cb, NT_DIMS, preferred_element_type=F32)
        kpe_t = kpe.T.astype(BF16)
        for hd in range(MLA_HEADS):
            k_ref[2 * hd * LANES:(2 * hd + 1) * LANES, :] = k_nope_t[hd * LANES:(hd + 1) * LANES, :].astype(BF16)
            k_ref[(2 * hd + 1) * LANES:(2 * hd + 2) * LANES, :] = kpe_t
        v_ref[...] = _dot(cb, wuv_ref[...]).astype(BF16)

    mq_ref[...] = zcols(_C_MQ, _C_MK) * (ML_QK ** -0.5)
    mk_ref[...] = zcols(_C_MK, _C_MV)
    mv_ref[...] = zcols(_C_MV, _C_OG)
    og_ref[...] = jax.nn.sigmoid(zcols(_C_OG, _C_GATE))
    zg = zcols(_C_GATE, _C_END) + bg_ref[...]
    ig = IGATE_CAP * jnp.tanh(zg / IGATE_CAP)
    lf = -(jnp.maximum(-zg, 0.0) + jnp.log1p(jnp.exp(-jnp.abs(zg))))
    lane = lax.broadcasted_iota(jnp.int32, zg.shape, 1)
    g = jnp.where(lane < ML_HEADS, ig, lf)
    g_ref[...] = g
    tril = tril_ref[...]
    hi, mid, lo = _split3(g)
    cum_ref[...] = _dot(tril, hi) + _dot(tril, mid) + _dot(tril, lo)


def _project(x, mod, pos_tab, chunk, g_pre, g_qlat, g_kvlat, pw, is_prompt):
    nseq, T, d = x.shape
    n = nseq * T
    if is_prompt:
        nb, t = 1, min(TOKEN_TILE, T)
    else:
        t = T
        nb = max(1, min(nseq, TOKEN_TILE // T))
    assert T % t == 0 and nseq % nb == 0
    tm = nb * t
    tiles_per_seq = T // t
    grid = (n // tm,)
    cos, sin = pos_tab
    if is_prompt:
        x_map = lambda i: (i // tiles_per_seq, i % tiles_per_seq, 0)
        mod_map = lambda i: (i // tiles_per_seq, 0, 0)
        tab_map = lambda i: (i % tiles_per_seq, 0)
    else:
        x_map = lambda i: (i, 0, 0)
        mod_map = lambda i: (i, 0, 0)
        tab_map = lambda i: (0, 0)
        cos = jnp.tile(cos, (nb, 1))
        sin = jnp.tile(sin, (nb, 1))
    const = lambda i: (0, 0)
    row = lambda i: (i, 0)
    tril = _chunk_tril(tm, chunk)
    in_specs = [
        pl.BlockSpec((nb, t, d), x_map),
        pl.BlockSpec((nb, 6, d), mod_map),
        pl.BlockSpec((1, d), const),
        pl.BlockSpec((d, _C_END), const),
        pl.BlockSpec((1, LANES), const),
        pl.BlockSpec((1, Q_LORA), const),
        pl.BlockSpec((Q_LORA, MLA_HEADS * MLA_NOPE), const),
        pl.BlockSpec((Q_LORA, MLA_HEADS * LANES), const),
        pl.BlockSpec((Q_LORA, MLA_HEADS * LANES), const),
        pl.BlockSpec((1, KV_LORA), const),
        pl.BlockSpec((tm, LANES), tab_map),
        pl.BlockSpec((tm, LANES), tab_map),
        pl.BlockSpec((tm, tm), const),
    ]
    args = [x, mod, g_pre.reshape(1, d), pw["w_in"], pw["b_gate"], g_qlat.reshape(1, Q_LORA), pw["w_qn"],
            pw["w_qr"], pw["w_qt"], g_kvlat.reshape(1, KV_LORA), cos, sin, tril]

    def out(cols, dtype=F32):
        return jax.ShapeDtypeStruct((n, cols), dtype), pl.BlockSpec((tm, cols), row)

    if is_prompt:
        in_specs += [pl.BlockSpec((MLA_HEADS * MLA_NOPE, KV_LORA), const),
                     pl.BlockSpec((KV_LORA, MLA_HEADS * HEAD_OUT), const)]
        args += [pw["w_uk"], pw["w_uv"]]
        k_t = (jax.ShapeDtypeStruct((MLA_HEADS * 2 * LANES, n), BF16),
               pl.BlockSpec((MLA_HEADS * 2 * LANES, tm), lambda i: (0, i)))
        outs = [out(MLA_HEADS * 2 * LANES, BF16), k_t, out(MLA_HEADS * HEAD_OUT, BF16)]
    else:
        in_specs += [pl.BlockSpec((MLA_HEADS, MLA_NOPE, KV_LORA), lambda i: (0, 0, 0))]
        args += [pw["w_ukT"]]
        outs = [out(MLA_HEADS * KV_LORA), out(MLA_HEADS * LANES)]
    outs += [out(KV_LORA), out(MLA_ROPE), out(ML_HEADS * ML_QK), out(ML_HEADS * ML_QK), out(ML_HEADS * ML_V),
             out(LANES), out(LANES), out(ML_HEADS * ML_V)]
    return pl.pallas_call(
        functools.partial(_proj_kernel, is_prompt=is_prompt),
        out_shape=[o[0] for o in outs],
        grid=grid,
        in_specs=in_specs,
        out_specs=[o[1] for o in outs],
        compiler_params=_cparams(("arbitrary",)),
        name="proj_prompt" if is_prompt else "proj_sample",
    )(*args)


def _attn_kernel(qi_ref, ki_ref, q_ref, k_ref, v_ref, o_ref, m_ref, l_ref, acc_ref):
    p = pl.program_id(1)
    qi = qi_ref[p]
    ki = ki_ref[p]
    tq = q_ref.shape[0]
    tk = k_ref.shape[1]

    @pl.when(ki == 0)
    def _():
        m_ref[...] = jnp.full(m_ref.shape, NEG_INF, F32)
        l_ref[...] = jnp.zeros(l_ref.shape, F32)
        acc_ref[...] = jnp.zeros(acc_ref.shape, F32)

    def update(masked):
        for hd in range(MLA_HEADS):
            qh = q_ref[:, hd * 2 * LANES:(hd + 1) * 2 * LANES]
            s = _dot(qh, k_ref[hd * 2 * LANES:(hd + 1) * 2 * LANES, :])
            if masked:
                row = lax.broadcasted_iota(jnp.int32, s.shape, 0) + qi * tq
                col = lax.broadcasted_iota(jnp.int32, s.shape, 1) + ki * tk
                s = jnp.where(col <= row, s, NEG_INF)
            m_prev = m_ref[hd]
            m_next = jnp.maximum(m_prev, jnp.max(s, axis=1, keepdims=True))
            alpha = jnp.exp2(m_prev - m_next)
            pr = jnp.exp2(s - jnp.concatenate([m_next] * (tk // LANES), axis=1))
            l_ref[hd] = alpha * l_ref[hd] + jnp.sum(pr, axis=1, keepdims=True)
            acc_ref[hd] = alpha * acc_ref[hd] + _dot(pr.astype(BF16), v_ref[:, hd * HEAD_OUT:(hd + 1) * HEAD_OUT])
            m_ref[hd] = m_next

    first_masked = (qi * tq) // tk
    last = (qi * tq + tq - 1) // tk

    @pl.when(ki < first_masked)
    def _():
        update(False)

    @pl.when(ki >= first_masked)
    def _():
        update(True)

    @pl.when(ki == last)
    def _():
        for hd in range(MLA_HEADS):
            o_ref[:, hd * HEAD_OUT:(hd + 1) * HEAD_OUT] = acc_ref[hd] / l_ref[hd]


def _prompt_attention(q, k, v, nseq, S):
    tq = min(ATTN_Q_BLOCK, S)
    tk = min(ATTN_K_BLOCK, S)
    nq, nk = S // tq, S // tk
    pairs = [(i, j) for i in range(nq) for j in range((i * tq + tq - 1) // tk + 1)]
    qi_tab = jnp.asarray([p[0] for p in pairs], jnp.int32)
    ki_tab = jnp.asarray([p[1] for p in pairs], jnp.int32)
    wq = MLA_HEADS * 2 * LANES
    wv = MLA_HEADS * HEAD_OUT
    return pl.pallas_call(
        _attn_kernel,
        out_shape=jax.ShapeDtypeStruct((nseq * S, wv), F32),
        grid_spec=pltpu.PrefetchScalarGridSpec(
            num_scalar_prefetch=2,
            grid=(nseq, len(pairs)),
            in_specs=[pl.BlockSpec((tq, wq), lambda b, p, qi, ki: (b * nq + qi[p], 0)),
                      pl.BlockSpec((wq, tk), lambda b, p, qi, ki: (0, b * nk + ki[p])),
                      pl.BlockSpec((tk, wv), lambda b, p, qi, ki: (b * nk + ki[p], 0))],
            out_specs=pl.BlockSpec((tq, wv), lambda b, p, qi, ki: (b * nq + qi[p], 0)),
            scratch_shapes=[pltpu.VMEM((MLA_HEADS, tq, LANES), F32),
                            pltpu.VMEM((MLA_HEADS, tq, LANES), F32),
                            pltpu.VMEM((MLA_HEADS, tq, HEAD_OUT), F32)]),
        compiler_params=_cparams(("arbitrary", "arbitrary")),
        name="prompt_attention",
    )(qi_tab, ki_tab, q, k, v)


def _sattn_kernel(pt_ref, qa_ref, qr_ref, ckvn_ref, kpen_ref, wuv_ref, cckv_hbm, ckpe_hbm, o_ref,
                  ckv_buf, kpe_buf, sem, kcb, wq_s, q_s, qr_s, kn_s, pn_s, m_ref, l_ref, acc_ref,
                  *, layer, npg, nchunks, sub):
    b = pl.program_id(0)
    c = pl.program_id(1)
    step = b * nchunks + c
    nsteps = pl.num_programs(0) * nchunks
    nslots = ckv_buf.shape[0]
    slot = step % nslots
    T = qa_ref.shape[0]
    rows = MLA_HEADS * T
    npos = npg * PAGE_SIZE
    pack = wq_s.shape[0]

    def page_copies(page_of, slt):
        cps = []
        for j in range(npg):
            pg = page_of(j)
            dst = pl.ds(j * PAGE_SIZE, PAGE_SIZE)
            cps.append(pltpu.make_async_copy(cckv_hbm.at[layer, pg], ckv_buf.at[slt, dst, :], sem.at[0, slt]))
            cps.append(pltpu.make_async_copy(ckpe_hbm.at[layer, pg], kpe_buf.at[slt, :, dst], sem.at[1, slt]))
        return cps

    def fetch(stp, slt):
        base = jnp.minimum(stp, nsteps - 1) * npg
        for cp in page_copies(lambda j: pt_ref[base + j], slt):
            cp.start()

    def wait_pages(slt):
        for cp in page_copies(lambda j: 0, slt):
            cp.wait()

    @pl.when(step == 0)
    def _():
        for k in range(nslots - 1):
            fetch(k, k)

    @pl.when(c == 0)
    def _():
        m_ref[...] = jnp.full(m_ref.shape, NEG_INF, F32)
        l_ref[...] = jnp.zeros(l_ref.shape, F32)
        acc_ref[...] = jnp.zeros(acc_ref.shape, F32)
        qa = qa_ref[...]
        qr = qr_ref[...]
        q32 = jnp.concatenate([qa[:, hd * KV_LORA:(hd + 1) * KV_LORA] for hd in range(MLA_HEADS)], axis=0)
        q_s[...] = q32.astype(BF16)
        qr_s[...] = jnp.concatenate([qr[:, hd * LANES:(hd + 1) * LANES] for hd in range(MLA_HEADS)],
                                    axis=0).astype(BF16)
        qt = jnp.concatenate([q32] * pack, axis=0).T
        lane_blk = lax.broadcasted_iota(jnp.int32, qt.shape, 1) // rows
        for j in range(pack):
            wq_s[j] = jnp.where(lane_blk == j, qt, 0.0).astype(BF16)

    wait_pages(slot)

    q = q_s[...]
    qr = qr_s[:, :MLA_ROPE]

    def softmax_update(s, pv):
        m_prev = m_ref[...]
        m_next = jnp.maximum(m_prev, jnp.max(s, axis=1, keepdims=True))
        alpha = jnp.exp2(m_prev - m_next)
        pr = jnp.exp2(s - jnp.concatenate([m_next] * (s.shape[1] // LANES), axis=1))
        l_ref[...] = alpha * l_ref[...] + jnp.sum(pr, axis=1, keepdims=True)
        acc_ref[...] = jnp.concatenate([alpha] * (KV_LORA // LANES), axis=1) * acc_ref[...] + pv(pr.astype(BF16))
        m_ref[...] = m_next

    s_parts = []
    for grp in range(npos // (sub * pack)):
        st = None
        for j in range(pack):
            sb = grp * pack + j
            sl = slice(sb * sub, (sb + 1) * sub)
            kc = ckv_buf[slot, sl, :].astype(BF16)
            kcb[sl, :] = kc
            t = _dot(kc, wq_s[j])
            st = t if st is None else st + t
        s_t = st.T
        for j in range(pack):
            sb = grp * pack + j
            kp = kpe_buf[slot, :, sb * sub:(sb + 1) * sub].astype(BF16)
            s_parts.append(s_t[j * rows:(j + 1) * rows, :] + _dot(qr, kp))

    fetch(step + nslots - 1, (step + nslots - 1) % nslots)

    def pv_chunk(p):
        parts = []
        for sb in range(npos // sub):
            psb = p[:, sb * sub:(sb + 1) * sub]
            parts.append(jnp.concatenate(
                [_dot(psb, kcb[sb * sub:(sb + 1) * sub, hh * LANES:(hh + 1) * LANES])
                 for hh in range(KV_LORA // LANES)], axis=1))
        while len(parts) > 1:
            parts = [parts[i] + parts[i + 1] for i in range(0, len(parts) - 1, 2)] + parts[len(parts) & ~1:]
        return parts[0]

    softmax_update(jnp.concatenate(s_parts, axis=1), pv_chunk)

    @pl.when(c == nchunks - 1)
    def _():
        kn_s[...] = jnp.zeros(kn_s.shape, F32)
        pn_s[...] = jnp.zeros(pn_s.shape, F32)
        kn_s[0:T, :] = ckvn_ref[...]
        pn_s[0:T, :] = kpen_ref[...]
        row = lax.broadcasted_iota(jnp.int32, (rows, LANES), 0)
        col = lax.broadcasted_iota(jnp.int32, (rows, LANES), 1)
        kn = kn_s[...].astype(BF16)
        s_new = (lax.dot_general(q, kn, NT_DIMS, preferred_element_type=F32)
                 + lax.dot_general(qr, pn_s[...].astype(BF16), NT_DIMS, preferred_element_type=F32))
        softmax_update(jnp.where(col <= row % T, s_new, NEG_INF), lambda p: _dot(p, kn))
        o_lat = acc_ref[...] / jnp.concatenate([l_ref[...]] * (KV_LORA // LANES), axis=1)
        for hd in range(MLA_HEADS):
            o_ref[:, hd * HEAD_OUT:(hd + 1) * HEAD_OUT] = _dot(o_lat[hd * T:(hd + 1) * T, :], wuv_ref[hd])

    @pl.when(step == nsteps - 1)
    def _():
        for k in range(1, nslots):
            wait_pages((step + k) % nslots)


def _sample_attention(qa, qr, ckv_new, kpe_new, cache_ckv, cache_kpe, layer, page_table, w_uv):
    nseq, n_pages = page_table.shape
    T = qa.shape[0] // nseq
    npg = min(PAGES_PER_STEP, n_pages)
    assert n_pages % npg == 0
    nchunks = n_pages // npg
    rows = MLA_HEADS * T
    assert LANES % rows == 0
    pack = LANES // rows
    sub = min(ATTN_SUB, npg * PAGE_SIZE // pack)
    assert (npg * PAGE_SIZE) % (sub * pack) == 0
    seq = lambda b, c, pt: (b, 0)
    kern = functools.partial(_sattn_kernel, layer=layer, npg=npg, nchunks=nchunks, sub=sub)
    return pl.pallas_call(
        kern,
        out_shape=jax.ShapeDtypeStruct((nseq * T, MLA_HEADS * HEAD_OUT), F32),
        grid_spec=pltpu.PrefetchScalarGridSpec(
            num_scalar_prefetch=1,
            grid=(nseq, nchunks),
            in_specs=[pl.BlockSpec((T, MLA_HEADS * KV_LORA), seq),
                      pl.BlockSpec((T, MLA_HEADS * LANES), seq),
                      pl.BlockSpec((T, KV_LORA), seq),
                      pl.BlockSpec((T, MLA_ROPE), seq),
                      pl.BlockSpec((MLA_HEADS, KV_LORA, HEAD_OUT), lambda b, c, pt: (0, 0, 0)),
                      pl.BlockSpec(memory_space=pl.ANY),
                      pl.BlockSpec(memory_space=pl.ANY)],
            out_specs=pl.BlockSpec((T, MLA_HEADS * HEAD_OUT), seq),
            scratch_shapes=[pltpu.VMEM((PAGE_SLOTS, npg * PAGE_SIZE, KV_LORA), F32),
                            pltpu.VMEM((PAGE_SLOTS, MLA_ROPE, npg * PAGE_SIZE), F32),
                            pltpu.SemaphoreType.DMA((2, PAGE_SLOTS)),
                            pltpu.VMEM((npg * PAGE_SIZE, KV_LORA), BF16),
                            pltpu.VMEM((pack, KV_LORA, LANES), BF16),
                            pltpu.VMEM((rows, KV_LORA), BF16),
                            pltpu.VMEM((rows, LANES), BF16),
                            pltpu.VMEM((LANES, KV_LORA), F32),
                            pltpu.VMEM((LANES, MLA_ROPE), F32),
                            pltpu.VMEM((rows, LANES), F32),
                            pltpu.VMEM((rows, LANES), F32),
                            pltpu.VMEM((rows, KV_LORA), F32)]),
        compiler_params=_cparams(("arbitrary", "arbitrary")),
        name="sample_attention",
    )(page_table.reshape(-1), qa, qr, ckv_new, kpe_new, w_uv, cache_ckv, jnp.swapaxes(cache_kpe, 2, 3))


ML_HK = ML_HEADS * ML_QK


def _rows_per_head(vals, rows):
    return jnp.concatenate([jnp.broadcast_to(v, (rows, v.shape[1])) for v in vals], axis=0)


def _head_block_mask(nrows, row_blk, ncols, col_blk):
    r = lax.broadcasted_iota(jnp.int32, (nrows, ncols), 0) // row_blk
    c = lax.broadcasted_iota(jnp.int32, (nrows, ncols), 1) // col_blk
    return r == c


def _mlstm_kernel(mq_ref, mk_ref, mv_ref, g_ref, cum_ref, s0_ref, n0_ref, m0_ref, h_ref, st_ref, n_ref, m_ref,
                  *, unroll_seqs, dot_dtype):
    ci = pl.program_id(1)
    nbs, L = mq_ref.shape[0], mq_ref.shape[1]
    H = ML_HEADS
    R = H * L

    @pl.when(ci == 0)
    def _():
        st_ref[...] = s0_ref[...]
        n_ref[...] = n0_ref[...]
        m_ref[...] = m0_ref[...]

    trow = lax.broadcasted_iota(jnp.int32, (R, L), 0) % L
    tcol = lax.broadcasted_iota(jnp.int32, (R, L), 1)
    causal = tcol <= trow
    q_mask = _head_block_mask(R, L, ML_HK, ML_QK)
    s_mask = _head_block_mask(R, L, R, L)
    k_mask = _head_block_mask(ML_HK, ML_QK, R, L)

    def one_seq(i):
        g = g_ref[i]
        cum = cum_ref[i]
        g_t = g.T
        cum_t = cum.T
        mq = mq_ref[i]
        mk = mk_ref[i]
        mv = mv_ref[i]
        mprev = m_ref[i]
        ig_col = jnp.concatenate([g[:, h:h + 1] for h in range(H)], axis=0)
        b_col = jnp.concatenate([cum[:, H + h:H + h + 1] for h in range(H)], axis=0)
        ig_row = _rows_per_head([g_t[h:h + 1, :] for h in range(H)], L)
        b_row = _rows_per_head([cum_t[H + h:H + h + 1, :] for h in range(H)], L)
        m_prev = _rows_per_head([mprev[h:h + 1, 0:1] for h in range(H)], L)
        b_last = _rows_per_head([cum[L - 1:L, H + h:H + h + 1] for h in range(H)], L)

        log_d = jnp.where(causal, b_col - b_row + ig_row, NEG_INF)
        log_inter = b_col + m_prev
        m_t = jnp.maximum(log_inter, jnp.max(log_d, axis=1, keepdims=True))
        d = jnp.exp(log_d - m_t)
        inter = jnp.exp(log_inter - m_t)

        q_blk = jnp.where(q_mask, jnp.concatenate([mq] * H, axis=0), 0.0)
        k_t = mk.T
        q_op = q_blk.astype(dot_dtype)
        s = _dot(q_op, k_t.astype(dot_dtype)) * d
        s_bd = jnp.where(s_mask, jnp.concatenate([s] * H, axis=1), 0.0)
        v_stack = jnp.concatenate([mv[:, h * ML_V:(h + 1) * ML_V] for h in range(H)], axis=0)
        st = st_ref[i]
        n_row = n_ref[i]
        num = (inter * _dot(q_op, st.astype(dot_dtype))
               + _dot(s_bd.astype(dot_dtype), v_stack.astype(dot_dtype)))
        den = inter * jnp.sum(q_blk * n_row, axis=1, keepdims=True) + jnp.sum(s, axis=1, keepdims=True)
        h_all = num / jnp.maximum(jnp.abs(den), jnp.exp(-m_t))
        for h in range(H):
            h_ref[i, :, h * ML_V:(h + 1) * ML_V] = h_all[h * L:(h + 1) * L, :]

        m_last = [m_t[(h + 1) * L - 1:(h + 1) * L, :] for h in range(H)]
        m_new = _rows_per_head(m_last, L)
        w_col = jnp.exp(b_last - b_col + ig_col - m_new)
        decay = jnp.exp(b_last + m_prev - m_new)
        kt_blk = jnp.where(k_mask, jnp.concatenate([k_t] * H, axis=1), 0.0)
        decay_k = _rows_per_head([decay[h * L:h * L + 1, :] for h in range(H)], ML_QK)
        st_ref[i] = decay_k * st + _dot(kt_blk.astype(dot_dtype), (w_col * v_stack).astype(dot_dtype))

        def widen(col):
            return jnp.concatenate([jnp.broadcast_to(col[h * L:(h + 1) * L, :], (L, ML_QK)) for h in range(H)],
                                   axis=1)
        n_ref[i] = widen(decay)[0:1, :] * n_row + jnp.sum(widen(w_col) * mk, axis=0, keepdims=True)
        m_ref[i] = jnp.concatenate([jnp.broadcast_to(ml, (1, LANES)) for ml in m_last], axis=0)

    if unroll_seqs:
        for i in range(nbs):
            one_seq(i)
    else:
        def body(i, carry):
            one_seq(i)
            return carry
        lax.fori_loop(0, nbs, body, 0)


def _mlstm(mq, mk, mv, g, cum, c0, n0, m0, nseq, T):
    L = ML_CHUNK if T % ML_CHUNK == 0 else T
    nc = T // L
    nbs = nseq if nseq <= 2 else 16
    assert nseq % nbs == 0
    r3 = lambda a: a.reshape(nseq, T, a.shape[-1])
    tok = lambda gi, ci: (gi, ci, 0)
    st3 = lambda gi, ci: (gi, 0, 0)
    ct0 = jnp.swapaxes(c0.astype(F32), 2, 3).reshape(nseq, ML_HK, ML_V)
    n0r = n0.astype(F32).reshape(nseq, 1, ML_HK)
    m0b = jnp.broadcast_to(m0.astype(F32)[:, :, None], (nseq, ML_HEADS, LANES))
    kern = functools.partial(_mlstm_kernel, unroll_seqs=nbs <= 2, dot_dtype=BF16 if L >= 16 else F32)
    h, st, n, m = pl.pallas_call(
        kern,
        out_shape=[jax.ShapeDtypeStruct((nseq, T, ML_HEADS * ML_V), F32),
                   jax.ShapeDtypeStruct((nseq, ML_HK, ML_V), F32),
                   jax.ShapeDtypeStruct((nseq, 1, ML_HK), F32),
                   jax.ShapeDtypeStruct((nseq, ML_HEADS, LANES), F32)],
        grid=(nseq // nbs, nc),
        in_specs=[pl.BlockSpec((nbs, L, ML_HK), tok),
                  pl.BlockSpec((nbs, L, ML_HK), tok),
                  pl.BlockSpec((nbs, L, ML_HEADS * ML_V), tok),
                  pl.BlockSpec((nbs, L, LANES), tok),
                  pl.BlockSpec((nbs, L, LANES), tok),
                  pl.BlockSpec((nbs, ML_HK, ML_V), st3),
                  pl.BlockSpec((nbs, 1, ML_HK), st3),
                  pl.BlockSpec((nbs, ML_HEADS, LANES), st3)],
        out_specs=[pl.BlockSpec((nbs, L, ML_HEADS * ML_V), tok),
                   pl.BlockSpec((nbs, ML_HK, ML_V), st3),
                   pl.BlockSpec((nbs, 1, ML_HK), st3),
                   pl.BlockSpec((nbs, ML_HEADS, LANES), st3)],
        compiler_params=_cparams(("arbitrary", "arbitrary")),
        name="mlstm_prompt" if nseq <= 2 else "mlstm_sample",
    )(r3(mq), r3(mk), r3(mv), r3(g), r3(cum), ct0, n0r, m0b)
    c_new = jnp.swapaxes(st.reshape(nseq, ML_HEADS, ML_QK, ML_V), 2, 3)
    return h.reshape(nseq * T, ML_HEADS * ML_V), c_new, n.reshape(nseq, ML_HEADS, ML_QK), m[:, :, 0]


def _merge_kernel(x_ref, mod_ref, omla_ref, hml_ref, og_ref, gh_ref, wout_ref, gpm_ref, gpf_ref,
                  wrh_ref, wrl_ref, br_ref, x1_ref, h2_ref, idx_ref, gate_ref):
    nb, t, d = x_ref.shape
    tm = nb * t
    gh = gh_ref[...]
    parts = []
    for hd in range(MLA_HEADS + ML_HEADS):
        if hd < MLA_HEADS:
            xh = omla_ref[:, hd * HEAD_OUT:(hd + 1) * HEAD_OUT]
        else:
            j = hd - MLA_HEADS
            xh = hml_ref[:, j * HEAD_OUT:(j + 1) * HEAD_OUT]
        yh = _rms(xh, gh[:, hd * HEAD_OUT:(hd + 1) * HEAD_OUT])
        if hd >= MLA_HEADS:
            yh = yh * og_ref[:, j * HEAD_OUT:(j + 1) * HEAD_OUT]
        parts.append(yh.astype(BF16))
    merged = jnp.concatenate(parts, axis=1)
    mix = _dot(merged, wout_ref[...])
    mod = mod_ref[...]
    x1 = x_ref[...] + mod[:, 2:3, :] * _rms(mix, gpm_ref[...]).reshape(nb, t, d)
    x1_ref[...] = x1
    h2 = (_rms(x1, gpf_ref[...]) * (1.0 + mod[:, 4:5, :]) + mod[:, 3:4, :]).reshape(tm, d)
    _store_row_tiles(h2_ref, h2)
    hh = h2.astype(BF16)
    hl = (h2 - hh.astype(F32)).astype(BF16)
    wrh = wrh_ref[...]
    logits = (lax.dot_general(wrh, hh, NT_DIMS, preferred_element_type=F32)
              + lax.dot_general(wrh, hl, NT_DIMS, preferred_element_type=F32)
              + lax.dot_general(wrl_ref[...], hh, NT_DIMS, preferred_element_type=F32)) + br_ref[...]
    eidx = lax.broadcasted_iota(jnp.int32, logits.shape, 0)
    work = logits
    vals, idxs = [], []
    for _ in range(TOP_K):
        mx = jnp.max(work, axis=0, keepdims=True)
        sel = jnp.min(jnp.where(work == mx, eidx, N_EXPERTS), axis=0, keepdims=True)
        vals.append(mx)
        idxs.append(sel)
        work = jnp.where(eidx == sel, NEG_INF, work)
    ex = [jnp.exp(v - vals[0]) for v in vals]
    tot = ex[0] + ex[1] + ex[2] + ex[3]
    zi = jnp.zeros((8 - TOP_K, tm), jnp.int32)
    zf = jnp.zeros((8 - TOP_K, tm), F32)
    idx_ref[...] = jnp.concatenate(idxs + [zi], axis=0)
    gate_ref[...] = jnp.concatenate([e / tot for e in ex] + [zf], axis=0)


def _merge(x, mod, o_mla, h_ml, og, g_heads, w_out, g_post_mix, g_pre_ffn, wr_hi, wr_lo, b_router, is_prompt):
    nseq, T, d = x.shape
    n = nseq * T
    if is_prompt:
        nb, t = 1, min(TOKEN_TILE, T)
    else:
        t = T
        nb = max(1, min(nseq, TOKEN_TILE // T))
    tm = nb * t
    tiles_per_seq = T // t
    if is_prompt:
        x_map = lambda i: (i // tiles_per_seq, i % tiles_per_seq, 0)
        mod_map = lambda i: (i // tiles_per_seq, 0, 0)
    else:
        x_map = lambda i: (i, 0, 0)
        mod_map = lambda i: (i, 0, 0)
    const = lambda i: (0, 0)
    row = lambda i: (i, 0)
    col = lambda i: (0, i)
    wh = MLA_HEADS * HEAD_OUT
    return pl.pallas_call(
        _merge_kernel,
        out_shape=[jax.ShapeDtypeStruct((nseq, T, d), F32), jax.ShapeDtypeStruct((n * ROW_TILE, LANES), F32),
                   jax.ShapeDtypeStruct((8, n), jnp.int32), jax.ShapeDtypeStruct((8, n), F32)],
        grid=(n // tm,),
        in_specs=[pl.BlockSpec((nb, t, d), x_map),
                  pl.BlockSpec((nb, 6, d), mod_map),
                  pl.BlockSpec((tm, wh), row),
                  pl.BlockSpec((tm, wh), row),
                  pl.BlockSpec((tm, wh), row),
                  pl.BlockSpec((1, 2 * wh), const),
                  pl.BlockSpec((2 * wh, d), const),
                  pl.BlockSpec((1, d), const),
                  pl.BlockSpec((1, d), const),
                  pl.BlockSpec((N_EXPERTS, d), const),
                  pl.BlockSpec((N_EXPERTS, d), const),
                  pl.BlockSpec((N_EXPERTS, 1), const)],
        out_specs=[pl.BlockSpec((nb, t, d), x_map), pl.BlockSpec((tm * ROW_TILE, LANES), row),
                   pl.BlockSpec((8, tm), col), pl.BlockSpec((8, tm), col)],
        compiler_params=_cparams(("arbitrary",)),
        name="merge_prompt" if is_prompt else "merge_sample",
    )(x, mod, o_mla, h_ml, og, g_heads.reshape(1, 2 * wh), w_out, g_post_mix.reshape(1, d),
      g_pre_ffn.reshape(1, d), wr_hi, wr_lo, b_router.reshape(N_EXPERTS, 1))


def _moe_kernel(be_ref, tok_ref, sh_ref, nu_ref, h2_hbm, wgu_ref, bgu_ref, wd_ref, bd_ref, y_ref,
                xbuf, sem, wgu_s, wd_s):
    blk = pl.program_id(0)
    nblk = pl.num_programs(0)
    nslots = xbuf.shape[0]
    slot = blk % nslots
    tb = y_ref.shape[0] // ROW_TILE
    de = wd_ref.shape[0]
    n_used = nu_ref[0]

    def row_copy(tok8, r, slt):
        src = h2_hbm.at[pl.ds(pl.multiple_of(tok8, ROW_TILE), ROW_TILE), :]
        return pltpu.make_async_copy(src, xbuf.at[slt, pl.ds(r * ROW_TILE, ROW_TILE), :], sem.at[slt])

    def rows_base(bk):
        return bk * tb + sh_ref[be_ref[jnp.minimum(bk, nblk - 1)]]

    def wait_rows(slt):
        pltpu.make_async_copy(h2_hbm.at[pl.ds(0, tb * ROW_TILE), :], xbuf.at[slt], sem.at[slt]).wait()

    @pl.when(blk == 0)
    def _():
        for bk in range(nslots - 1):
            base = rows_base(bk)

            def body(r, carry):
                row_copy(tok_ref[base + r], r, bk).start()
                return carry
            lax.fori_loop(0, tb, body, 0, unroll=8)

    hot = blk < n_used
    e = be_ref[blk]
    e_prev = be_ref[jnp.maximum(blk - 1, 0)]

    @pl.when(jnp.logical_and(hot, jnp.logical_or(blk == 0, e != e_prev)))
    def _():
        wgu_s[...] = wgu_ref[...].astype(BF16)
        wd_s[...] = wd_ref[...].astype(BF16)

    @pl.when(hot)
    def _():
        wait_rows(slot)
        xb = _load_row_tiles(xbuf.at[slot], tb).astype(BF16)
        ahead = blk + nslots - 1
        base = rows_base(ahead)
        for r in range(tb):
            row_copy(tok_ref[base + r], r, ahead % nslots).start()
        gu = _dot(xb, wgu_s[...]) + bgu_ref[...]
        gt = jnp.minimum(gu[:, :de], SWIGLU_LIMIT)
        up = jnp.clip(gu[:, de:], -SWIGLU_LIMIT, SWIGLU_LIMIT)
        act = gt * jax.nn.sigmoid(SWIGLU_ALPHA * gt) * (up + 1.0)
        _store_row_tiles(y_ref, _dot(act.astype(BF16), wd_s[...]) + bd_ref[...])

    @pl.when(blk == n_used)
    def _():
        for k in range(nslots - 1):
            wait_rows((blk + k) % nslots)

    @pl.when(jnp.logical_and(hot, jnp.logical_and(blk == nblk - 1, n_used == nblk)))
    def _():
        for k in range(1, nslots):
            wait_rows((blk + k) % nslots)

    @pl.when(jnp.logical_not(hot))
    def _():
        y_ref[...] = jnp.zeros(y_ref.shape, F32)


def _moe_experts(h2, block_e, tok_sorted, shift, n_used, w_gate_up, b_gate_up, w_down, b_down):
    ne, d, de2 = w_gate_up.shape
    de = w_down.shape[1]
    n_blocks = block_e.shape[0]
    tb = MOE_ROWS
    wmap = lambda i, be, tk, sh, nu: (be[i], 0, 0)
    return pl.pallas_call(
        _moe_kernel,
        out_shape=jax.ShapeDtypeStruct((n_blocks * tb * ROW_TILE, LANES), F32),
        grid_spec=pltpu.PrefetchScalarGridSpec(
            num_scalar_prefetch=4,
            grid=(n_blocks,),
            in_specs=[pl.BlockSpec(memory_space=pl.ANY),
                      pl.BlockSpec((None, d, de2), wmap),
                      pl.BlockSpec((None, 1, de2), wmap),
                      pl.BlockSpec((None, de, d), wmap),
                      pl.BlockSpec((None, 1, d), wmap)],
            out_specs=pl.BlockSpec((tb * ROW_TILE, LANES), lambda i, be, tk, sh, nu: (i, 0)),
            scratch_shapes=[pltpu.VMEM((MOE_LOOKAHEAD + 1, tb * ROW_TILE, LANES), F32),
                            pltpu.SemaphoreType.DMA((MOE_LOOKAHEAD + 1,)),
                            pltpu.VMEM((d, de2), BF16),
                            pltpu.VMEM((de, d), BF16)]),
        compiler_params=_cparams(("arbitrary",)),
        name="moe_experts",
    )(block_e, tok_sorted, shift, n_used, h2, w_gate_up, b_gate_up.reshape(ne, 1, de2), w_down,
      b_down.reshape(ne, 1, d))


def _route(top_idx, n_blocks):
    n = top_idx.shape[0]
    m = n * TOP_K
    tb = MOE_ROWS
    flat_e = top_idx.reshape(m)
    pair = jnp.arange(m, dtype=jnp.int32)
    order = lax.sort(flat_e * m + pair) % m
    tok_sorted = jnp.concatenate([order // TOP_K * ROW_TILE, jnp.zeros(((MOE_LOOKAHEAD + 1) * tb,), jnp.int32)])
    onehot = (flat_e[:, None] == jnp.arange(N_EXPERTS, dtype=jnp.int32)[None, :]).astype(jnp.int32)
    csum = jnp.cumsum(onehot, axis=0)
    counts = csum[-1]
    padded = (counts + tb - 1) // tb * tb
    pends = jnp.cumsum(padded)
    pstarts = pends - padded
    starts = jnp.cumsum(counts) - counts
    dest = (jnp.sum(onehot * (csum - 1 + pstarts[None, :]), axis=1) * ROW_TILE).astype(jnp.int32)
    blk_row = jnp.arange(n_blocks, dtype=jnp.int32) * tb
    block_e = jnp.minimum(jnp.sum((pends[None, :] <= blk_row[:, None]).astype(jnp.int32), axis=1), N_EXPERTS - 1)
    n_used = (pends[-1] // tb).astype(jnp.int32).reshape(1)
    return dest, tok_sorted, (starts - pstarts).astype(jnp.int32), block_e.astype(jnp.int32), n_used


def _combine_kernel(dest_ref, yb_hbm, x1_ref, mod_ref, gate_ref, gpost_ref, o_ref, ybuf, sem, *, base):
    i = pl.program_id(0)
    nsteps = pl.num_programs(0)
    slot = i % 2
    nb, t, d = x1_ref.shape
    tm = nb * t

    def gather(step, slt):
        def body(r, carry):
            for kk in range(TOP_K):
                row8 = pl.multiple_of(dest_ref[base + (step * tm + r) * TOP_K + kk], ROW_TILE)
                dst = ybuf.at[slt, kk, pl.ds(pl.multiple_of(r * ROW_TILE, ROW_TILE), ROW_TILE), :]
                pltpu.make_async_copy(yb_hbm.at[pl.ds(row8, ROW_TILE), :], dst, sem.at[slt]).start()
            return carry
        lax.fori_loop(0, tm, body, 0, unroll=4)

    @pl.when(i == 0)
    def _():
        gather(0, 0)

    @pl.when(i + 1 < nsteps)
    def _():
        gather(i + 1, 1 - slot)

    for kk in range(TOP_K):
        pltpu.make_async_copy(yb_hbm.at[pl.ds(0, tm * ROW_TILE), :], ybuf.at[slot, kk], sem.at[slot]).wait()

    gate = gate_ref[...]
    y = gate[:, 0:1] * _load_row_tiles(ybuf.at[slot, 0], tm)
    for kk in range(1, TOP_K):
        y = y + gate[:, kk:kk + 1] * _load_row_tiles(ybuf.at[slot, kk], tm)
    mod = mod_ref[...]
    o_ref[...] = x1_ref[...] + mod[:, 5:6, :] * _rms(y, gpost_ref[...]).reshape(nb, t, d)


def _combine(yb, dest, base, x1, mod, gates, g_post_ffn, is_prompt):
    nseq, T, d = x1.shape
    n = nseq * T
    if is_prompt:
        nb, t = 1, min(COMBINE_TILE, T)
    else:
        t = T
        nb = max(1, min(nseq, COMBINE_TILE // T))
    tm = nb * t
    tiles_per_seq = T // t
    if is_prompt:
        x_map = lambda i, dr: (i // tiles_per_seq, i % tiles_per_seq, 0)
        mod_map = lambda i, dr: (i // tiles_per_seq, 0, 0)
    else:
        x_map = lambda i, dr: (i, 0, 0)
        mod_map = lambda i, dr: (i, 0, 0)
    return pl.pallas_call(
        functools.partial(_combine_kernel, base=base),
        out_shape=jax.ShapeDtypeStruct((nseq, T, d), F32),
        grid_spec=pltpu.PrefetchScalarGridSpec(
            num_scalar_prefetch=1,
            grid=(n // tm,),
            in_specs=[pl.BlockSpec(memory_space=pl.ANY),
                      pl.BlockSpec((nb, t, d), x_map),
                      pl.BlockSpec((nb, 6, d), mod_map),
                      pl.BlockSpec((tm, TOP_K), lambda i, dr: (i, 0)),
                      pl.BlockSpec((1, d), lambda i, dr: (0, 0))],
            out_specs=pl.BlockSpec((nb, t, d), x_map),
            scratch_shapes=[pltpu.VMEM((2, TOP_K, tm * ROW_TILE, LANES), F32),
                            pltpu.SemaphoreType.DMA((2,))]),
        compiler_params=_cparams(("arbitrary",)),
        name="combine_prompt" if is_prompt else "combine_sample",
    )(dest, yb, x1, mod, gates, g_post_ffn.reshape(1, d))


def _layer(xp, xs, cache_ckv, cache_kpe, layer, page_table, st_c, st_n, st_m, c_p, c_s, w):
    nbp, S, d = xp.shape
    nbs, T, _ = xs.shape
    past_len = page_table.shape[1] * PAGE_SIZE
    dt = xp.dtype

    mod = _modulation(jnp.concatenate([c_p, c_s], axis=0), w["w_ada"], w["b_ada"])
    mod = mod.reshape(nbp + nbs, 6, d)
    mod_p, mod_s = mod[:nbp], mod[nbp:]

    pw = _prep_proj_weights(w["w_in"], w["w_q_up"], w["w_uk"], w["w_uv"], w["b_igate"], w["b_fgate"])
    chunk_p = ML_CHUNK if S % ML_CHUNK == 0 else S
    chunk_s = ML_CHUNK if T % ML_CHUNK == 0 else T

    (q, k, v, ckv_p, kpe_p, mq, mk, mv, g, cum, og_p) = _project(
        xp, mod_p, _rope_tables(jnp.arange(S)), chunk_p, w["g_pre_mix"], w["g_qlat"], w["g_kvlat"], pw, True)
    o_mla_p = _prompt_attention(q, k, v, nbp, S)
    h_ml_p, c_new_p, n_new_p, m_new_p = _mlstm(
        mq, mk, mv, g, cum, jnp.zeros((nbp, ML_HEADS, ML_V, ML_QK), F32), jnp.zeros((nbp, ML_HEADS, ML_QK), F32),
        jnp.zeros((nbp, ML_HEADS), F32), nbp, S)

    (qa, qr, ckv_s, kpe_s, mq, mk, mv, g, cum, og_s) = _project(
        xs, mod_s, _rope_tables(past_len + jnp.arange(T)), chunk_s, w["g_pre_mix"], w["g_qlat"], w["g_kvlat"],
        pw, False)
    o_mla_s = _sample_attention(qa, qr, ckv_s, kpe_s, cache_ckv, cache_kpe, layer, page_table,
                                w["w_uv"].transpose(1, 0, 2))
    h_ml_s, c_new_s, n_new_s, m_new_s = _mlstm(mq, mk, mv, g, cum, st_c, st_n, st_m, nbs, T)

    w_out_b = w["w_out"].astype(BF16)
    wr_t = w["w_router"].T
    wr_hi = wr_t.astype(BF16)
    wr_lo = (wr_t - wr_hi.astype(F32)).astype(BF16)
    margs = (w["g_heads"], w_out_b, w["g_post_mix"], w["g_pre_ffn"], wr_hi, wr_lo, w["b_router"])
    x1_p, h2_p, idx_p, gate_p = _merge(xp, mod_p, o_mla_p, h_ml_p, og_p, *margs, True)
    x1_s, h2_s, idx_s, gate_s = _merge(xs, mod_s, o_mla_s, h_ml_s, og_s, *margs, False)

    n_p, n_s = nbp * S, nbs * T
    h2 = jnp.concatenate([h2_p, h2_s], axis=0)
    top_idx = jnp.concatenate([idx_p[:TOP_K], idx_s[:TOP_K]], axis=1).T
    gates = jnp.concatenate([gate_p[:TOP_K], gate_s[:TOP_K]], axis=1).T
    m_rows = (n_p + n_s) * TOP_K
    n_blocks = -(-m_rows // MOE_ROWS) + N_EXPERTS
    dest, tok_sorted, shift, block_e, n_used = _route(top_idx, n_blocks)
    yb = _moe_experts(h2, block_e, tok_sorted, shift, n_used, w["w_gate_up"], w["b_gate_up"], w["w_down"],
                      w["b_down"])
    y_p = _combine(yb, dest, 0, x1_p, mod_p, gates[:n_p], w["g_post_ffn"], True)
    y_s = _combine(yb, dest, n_p * TOP_K, x1_s, mod_s, gates[n_p:], w["g_post_ffn"], False)

    new_p = (ckv_p.reshape(nbp, S, KV_LORA), kpe_p.reshape(nbp, S, MLA_ROPE), c_new_p.astype(dt),
             n_new_p.astype(dt), m_new_p.astype(dt))
    new_s = (ckv_s.reshape(nbs, T, KV_LORA), kpe_s.reshape(nbs, T, MLA_ROPE), c_new_s.astype(dt),
             n_new_s.astype(dt), m_new_s.astype(dt))
    return y_p, y_s, new_p, new_s


def kernel(x_prompt, x_sample, cache_ckv, cache_kpe, page_table, state_C, state_n, state_m, c_prompt, c_sample,
           w_ada, b_ada, g_pre_mix, g_post_mix, g_pre_ffn, g_post_ffn, w_in, g_qlat, w_q_up, g_kvlat, w_uk, w_uv,
           b_igate, b_fgate, g_heads, w_out, w_router, b_router, w_gate_up, b_gate_up, w_down, b_down):
    weights = dict(w_ada=w_ada, b_ada=b_ada, g_pre_mix=g_pre_mix, g_post_mix=g_post_mix, g_pre_ffn=g_pre_ffn,
                   g_post_ffn=g_post_ffn, w_in=w_in, g_qlat=g_qlat, w_q_up=w_q_up, g_kvlat=g_kvlat, w_uk=w_uk,
                   w_uv=w_uv, b_igate=b_igate, b_fgate=b_fgate, g_heads=g_heads, w_out=w_out, w_router=w_router,
                   b_router=b_router, w_gate_up=w_gate_up, b_gate_up=b_gate_up, w_down=w_down, b_down=b_down)
    depth = w_ada.shape[0]
    xp, xs = x_prompt, x_sample
    news_p, news_s = [], []
    for l in range(depth):
        wl = {name: val[l] for name, val in weights.items()}
        xp, xs, new_p, new_s = _layer(xp, xs, cache_ckv, cache_kpe, l, page_table, state_C[l], state_n[l],
                                      state_m[l], c_prompt, c_sample, wl)
        news_p.append(new_p)
        news_s.append(new_s)
    stack = lambda items, j: jnp.stack([it[j] for it in items])
    return ((xp, xs) + tuple(stack(news_p, j) for j in range(5)) + tuple(stack(news_s, j) for j in range(5)))
```
